```python
import math
import jax, jax.numpy as jnp
from jax import lax
import numpy as np

D_MODEL = 1024
BATCH = 8
SEQ = 4096
DEPTH = 2

HEAD_DIM = 64
D_ATTN = D_MODEL // 2
N_ATTN_HEADS = D_ATTN // HEAD_DIM
D_CONV = D_MODEL // 4
N_CONV_GROUPS = D_CONV // HEAD_DIM
D_LRU = D_MODEL // 4
N_LRU_HEADS = D_LRU // HEAD_DIM
LRU_BLOCK = D_LRU // N_LRU_HEADS
D_MIX = D_ATTN + D_CONV + D_LRU
D_IN_PROJ = 3 * D_ATTN + 3 * D_CONV + 2 * D_LRU
D_FF = 4 * D_MODEL
MOBA_BLOCK = 256
MOBA_TOPK = 3
Q_CHUNK = 32
SHORT_CONV_W = 3
LRU_CONV_W = 4
LRU_C = 8.0
N_MOD = 6
EPS = 1e-6

kernel_name = "hymba_moba_conv_rglru_hybrid"


def rms_norm(x, g):
    xf = x.astype(jnp.float32)
    y = xf * lax.rsqrt(jnp.mean(xf * xf, axis=-1, keepdims=True) + EPS)
    return (y * g.astype(jnp.float32)).astype(x.dtype)


def causal_depthwise_conv(x, w):
    k_w, ch = w.shape
    return lax.conv_general_dilated(
        x, w[:, None, :].astype(x.dtype), window_strides=(1,),
        padding=((k_w - 1, 0),), dimension_numbers=("NWC", "WIO", "NWC"),
        feature_group_count=ch)


def moba_attention(q, k, v):
    B, H, S, hd = q.shape
    nb = -(-S // MOBA_BLOCK)
    pad = nb * MOBA_BLOCK - S
    kp = jnp.pad(k, ((0, 0), (0, 0), (0, pad), (0, 0)))
    vp = jnp.pad(v, ((0, 0), (0, 0), (0, pad), (0, 0)))
    kb = kp.reshape(B, H, nb, MOBA_BLOCK, hd)
    vb = vp.reshape(B, H, nb, MOBA_BLOCK, hd)
    kmean = jnp.mean(kb.astype(jnp.float32), axis=3)
    topk = min(MOBA_TOPK, nb)
    scale = 1.0 / math.sqrt(hd)
    bi = jnp.arange(B)[:, None, None, None]
    hi = jnp.arange(H)[None, :, None, None]
    blk_ids = jnp.arange(nb)

    def chunk(start):
        qc = lax.dynamic_slice_in_dim(q, start, Q_CHUNK, axis=2)
        own = start // MOBA_BLOCK
        qpos = start + jnp.arange(Q_CHUNK)
        gate = jnp.einsum("bhqd,bhnd->bhqn", qc.astype(jnp.float32), kmean)
        gate = jnp.where(blk_ids < own, gate, -jnp.inf)
        _, idx = lax.top_k(gate, topk)
        valid = idx < own
        ksel = kb[bi, hi, idx]
        vsel = vb[bi, hi, idx]
        s_sel = jnp.einsum("bhqd,bhqjkd->bhqjk", qc, ksel).astype(jnp.float32) * scale
        s_sel = jnp.where(valid[..., None], s_sel, -jnp.inf)
        s_sel = s_sel.reshape(B, H, Q_CHUNK, topk * MOBA_BLOCK)
        kown = lax.dynamic_slice_in_dim(kp, own * MOBA_BLOCK, MOBA_BLOCK, axis=2)
        vown = lax.dynamic_slice_in_dim(vp, own * MOBA_BLOCK, MOBA_BLOCK, axis=2)
        s_own = jnp.einsum("bhqd,bhkd->bhqk", qc, kown).astype(jnp.float32) * scale
        kpos = own * MOBA_BLOCK + jnp.arange(MOBA_BLOCK)
        s_own = jnp.where(kpos[None, :] <= qpos[:, None], s_own, -jnp.inf)
        p = jax.nn.softmax(jnp.concatenate([s_sel, s_own], axis=-1), axis=-1)
        p_sel = p[..., :topk * MOBA_BLOCK].reshape(B, H, Q_CHUNK, topk, MOBA_BLOCK).astype(v.dtype)
        p_own = p[..., topk * MOBA_BLOCK:].astype(v.dtype)
        return (jnp.einsum("bhqjk,bhqjkd->bhqd", p_sel, vsel)
                + jnp.einsum("bhqk,bhkd->bhqd", p_own, vown))

    starts = jnp.arange(S // Q_CHUNK) * Q_CHUNK
    o = lax.map(chunk, starts)
    return o.transpose(1, 0, 3, 2, 4).reshape(B, S, H * hd)


def rg_lru(x, w_a, b_a, w_x, b_x, lam):
    B, S, _ = x.shape
    xh = x.reshape(B, S, N_LRU_HEADS, LRU_BLOCK)
    r = jax.nn.sigmoid(jnp.einsum("bshi,hij->bshj", xh, w_a).reshape(B, S, D_LRU) + b_a)
    i = jax.nn.sigmoid(jnp.einsum("bshi,hij->bshj", xh, w_x).reshape(B, S, D_LRU) + b_x)
    log_a = -LRU_C * r.astype(jnp.float32) * jax.nn.softplus(-lam.astype(jnp.float32))
    a = jnp.exp(log_a)
    u = jnp.sqrt(-jnp.expm1(2.0 * log_a)) * (i * x).astype(jnp.float32)

    def combine(left, right):
        a1, b1 = left
        a2, b2 = right
        return a1 * a2, a2 * b1 + b2

    _, h = lax.associative_scan(combine, (a, u), axis=1)
    return h.astype(x.dtype)


def hybrid_layer(x, mod, ln1_g, ln2_g, w_in, q_norm_g, k_norm_g, sc_w,
                 lru_conv_w, lru_conv_b, lru_wa, lru_ba, lru_wx, lru_bx, lru_lambda,
                 mix_norm_g, w_out, w_up, w_down):
    B, S, _ = x.shape
    shift1, scale1, gate1, shift2, scale2, gate2 = jnp.split(mod, N_MOD, axis=-1)

    h = rms_norm(x, ln1_g) * (1.0 + scale1[:, None, :]) + shift1[:, None, :]
    proj = h @ w_in
    q, k, v, sc_b, sc_c, sc_u, lru_x, lru_gate = jnp.split(
        proj,
        [D_ATTN, 2 * D_ATTN, 3 * D_ATTN,
         3 * D_ATTN + D_CONV, 3 * D_ATTN + 2 * D_CONV, 3 * D_ATTN + 3 * D_CONV,
         3 * D_ATTN + 3 * D_CONV + D_LRU], axis=-1)

    def heads(t):
        return t.reshape(B, S, N_ATTN_HEADS, HEAD_DIM).transpose(0, 2, 1, 3)
    qh = rms_norm(heads(q), q_norm_g)
    kh = rms_norm(heads(k), k_norm_g)
    y_attn = moba_attention(qh, kh, heads(v))

    y_conv = sc_b * causal_depthwise_conv(sc_c * sc_u, sc_w)

    xr = causal_depthwise_conv(lru_x, lru_conv_w) + lru_conv_b
    y_lru = rg_lru(xr, lru_wa, lru_ba, lru_wx, lru_bx, lru_lambda) * jax.nn.gelu(lru_gate)

    y = jnp.concatenate([
        rms_norm(y_attn, mix_norm_g[:D_ATTN]),
        rms_norm(y_conv, mix_norm_g[D_ATTN:D_ATTN + D_CONV]),
        rms_norm(y_lru, mix_norm_g[D_ATTN + D_CONV:]),
    ], axis=-1)
    x = x + gate1[:, None, :] * (y @ w_out)

    h2 = rms_norm(x, ln2_g) * (1.0 + scale2[:, None, :]) + shift2[:, None, :]
    ff = jnp.square(jax.nn.relu(h2 @ w_up)) @ w_down
    return x + gate2[:, None, :] * ff


def setup_inputs(seed: int = 0) -> dict:
    key = jax.random.key(seed)
    ks = jax.random.split(key, 24)
    f32 = jnp.float32
    L = DEPTH

    def nrm(k, shape, scale):
        return jax.random.normal(k, shape, f32) * scale

    a_c = jax.random.uniform(ks[20], (L, D_LRU), f32, 0.9, 0.999)
    a0 = a_c ** (1.0 / LRU_C)
    lru_lambda = jnp.log(a0) - jnp.log1p(-a0)
    return {
        "x": nrm(ks[0], (BATCH, SEQ, D_MODEL), 1.0),
        "c": nrm(ks[1], (BATCH, D_MODEL), 1.0),
        "ln1_g": 1.0 + nrm(ks[2], (L, D_MODEL), 0.02),
        "ln2_g": 1.0 + nrm(ks[3], (L, D_MODEL), 0.02),
        "w_ada": nrm(ks[4], (L, D_MODEL, N_MOD * D_MODEL), D_MODEL ** -0.5),
        "b_ada": nrm(ks[5], (L, N_MOD * D_MODEL), 0.01),
        "w_in": nrm(ks[6], (L, D_MODEL, D_IN_PROJ), D_MODEL ** -0.5),
        "q_norm_g": 1.0 + nrm(ks[7], (L, HEAD_DIM), 0.02),
        "k_norm_g": 1.0 + nrm(ks[8], (L, HEAD_DIM), 0.02),
        "sc_w": nrm(ks[9], (L, SHORT_CONV_W, D_CONV), SHORT_CONV_W ** -0.5),
        "lru_conv_w": nrm(ks[10], (L, LRU_CONV_W, D_LRU), LRU_CONV_W ** -0.5),
        "lru_conv_b": nrm(ks[11], (L, D_LRU), 0.01),
        "lru_wa": nrm(ks[12], (L, N_LRU_HEADS, LRU_BLOCK, LRU_BLOCK), LRU_BLOCK ** -0.5),
        "lru_ba": nrm(ks[13], (L, D_LRU), 0.01),
        "lru_wx": nrm(ks[14], (L, N_LRU_HEADS, LRU_BLOCK, LRU_BLOCK), LRU_BLOCK ** -0.5),
        "lru_bx": nrm(ks[15], (L, D_LRU), 0.01),
        "lru_lambda": lru_lambda,
        "mix_norm_g": 1.0 + nrm(ks[16], (L, D_MIX), 0.02),
        "w_out": nrm(ks[17], (L, D_MIX, D_MODEL), D_MIX ** -0.5),
        "w_up": nrm(ks[18], (L, D_MODEL, D_FF), D_MODEL ** -0.5),
        "w_down": nrm(ks[19], (L, D_FF, D_MODEL), D_FF ** -0.5),
    }


def reference(x, c, ln1_g, ln2_g, w_ada, b_ada, w_in, q_norm_g, k_norm_g, sc_w,
              lru_conv_w, lru_conv_b, lru_wa, lru_ba, lru_wx, lru_bx, lru_lambda,
              mix_norm_g, w_out, w_up, w_down):
    c_act = jax.nn.silu(c)
    for l in range(DEPTH):
        mod = c_act @ w_ada[l] + b_ada[l]
        x = hybrid_layer(x, mod, ln1_g[l], ln2_g[l], w_in[l], q_norm_g[l], k_norm_g[l],
                         sc_w[l], lru_conv_w[l], lru_conv_b[l], lru_wa[l], lru_ba[l],
                         lru_wx[l], lru_bx[l], lru_lambda[l], mix_norm_g[l], w_out[l],
                         w_up[l], w_down[l])
    return x
```

```python
import functools
import math

import jax
import jax.numpy as jnp
from jax import lax
from jax.experimental import pallas as pl
from jax.experimental.pallas import tpu as pltpu

F32 = jnp.float32
BF16 = jnp.bfloat16

HEAD_DIM = 64
MOBA_BLOCK = 256
MOBA_TOPK = 3
LRU_C = 8.0
N_MOD = 6
EPS = 1e-6

TOKEN_TILE = 512
MOD_COL_TILE = 1536
FF_CHUNK = 1024
HALO = 8
VMEM_LIMIT_BYTES = 56 * 1024 * 1024


def _rms(x, axis=-1):
    return lax.rsqrt(jnp.mean(x * x, axis=axis, keepdims=True) + EPS)


def _dot(a, b):
    return jnp.dot(a, b, preferred_element_type=F32)


def _mod_kernel(c_ref, w_ref, b_ref, o_ref):
    c = c_ref[...]
    c_act = (c * jax.nn.sigmoid(c)).astype(BF16)
    o_ref[0] = _dot(c_act, w_ref[0].astype(BF16)) + b_ref[0]


def _modulation(c, w_ada, b_ada):
    depth, d_model, n_out = w_ada.shape
    batch = c.shape[0]
    return pl.pallas_call(
        _mod_kernel,
        grid=(depth, n_out // MOD_COL_TILE),
        in_specs=[
            pl.BlockSpec((batch, d_model), lambda l, j: (0, 0)),
            pl.BlockSpec((1, d_model, MOD_COL_TILE), lambda l, j: (l, 0, j)),
            pl.BlockSpec((1, 1, MOD_COL_TILE), lambda l, j: (l, 0, j)),
        ],
        out_specs=pl.BlockSpec((1, batch, MOD_COL_TILE), lambda l, j: (l, 0, j)),
        out_shape=jax.ShapeDtypeStruct((depth, batch, n_out), F32),
        compiler_params=pltpu.CompilerParams(
            dimension_semantics=("arbitrary", "arbitrary"), vmem_limit_bytes=VMEM_LIMIT_BYTES),
        name="adaln_modulation",
    )(c, w_ada, b_ada.reshape(depth, 1, n_out))


def _scan_linear_recurrence(a, u):
    n = a.shape[0]
    row = lax.broadcasted_iota(jnp.int32, a.shape, 0)
    d = 1
    while d < n:
        keep = row >= d
        a_prev = jnp.where(keep, pltpu.roll(a, d, 0), 1.0)
        u_prev = jnp.where(keep, pltpu.roll(u, d, 0), 0.0)
        u = a * u_prev + u
        a = a * a_prev
        d *= 2
    return a, u


def _gelu_tanh(x):
    return 0.5 * x * (1.0 + jnp.tanh(math.sqrt(2.0 / math.pi) * (x + 0.044715 * (x * x * x))))


def _softplus(z):
    return jnp.maximum(z, 0.0) + jnp.log1p(jnp.exp(-jnp.abs(z)))


def _mixer_in_kernel(x_ref, mod_ref, ln1_ref, w_in_ref, qg_ref, kg_ref, headsum_ref, scw_ref,
                     lcw_ref, lcb_ref, wa_ref, ba_ref, wx_ref, bx_ref, lam_ref, mng_ref,
                     qT_ref, k_ref, vT_ref, kmean_ref, ycl_ref,
                     cu_buf, lx_buf, h_carry, *, d_attn, d_conv, d_lru):
    s = pl.program_id(1)
    tm = x_ref.shape[1]
    n_blk = tm // MOBA_BLOCK

    @pl.when(s == 0)
    def _():
        cu_buf[0:HALO, :] = jnp.zeros((HALO, d_conv), F32)
        lx_buf[0:HALO, :] = jnp.zeros((HALO, d_lru), F32)
        h_carry[...] = jnp.zeros_like(h_carry)

    x = x_ref[0]
    m = mod_ref[0]
    shift1, scale1 = m[0:1], m[1:2]
    h = ((x * _rms(x) * ln1_ref[...]) * (1.0 + scale1) + shift1).astype(BF16)

    o_q, o_k, o_v = 0, d_attn, 2 * d_attn
    o_b = 3 * d_attn
    o_c, o_u = o_b + d_conv, o_b + 2 * d_conv
    o_lx = o_b + 3 * d_conv
    o_lg = o_lx + d_lru

    def head_norm(t, g_ref):
        ms = _dot((t * t).astype(BF16), headsum_ref[...])
        return t * lax.rsqrt(ms + EPS) * g_ref[...]

    q = head_norm(_dot(h, w_in_ref[:, o_q:o_q + d_attn]), qg_ref)
    qT = (q * (1.0 / math.sqrt(HEAD_DIM))).T.astype(BF16)
    for c in range(n_blk):
        qT_ref[0, c] = qT[:, c * MOBA_BLOCK:(c + 1) * MOBA_BLOCK]

    k = head_norm(_dot(h, w_in_ref[:, o_k:o_k + d_attn]), kg_ref)
    k_ref[0] = k.astype(BF16)
    for c in range(n_blk):
        kmean_ref[0, c] = jnp.mean(k[c * MOBA_BLOCK:(c + 1) * MOBA_BLOCK], axis=0, keepdims=True)

    vT = _dot(h, w_in_ref[:, o_v:o_v + d_attn]).T.astype(BF16)
    for c in range(n_blk):
        vT_ref[0, c] = vT[:, c * MOBA_BLOCK:(c + 1) * MOBA_BLOCK]

    sc_b = _dot(h, w_in_ref[:, o_b:o_b + d_conv])
    cu = _dot(h, w_in_ref[:, o_c:o_c + d_conv]) * _dot(h, w_in_ref[:, o_u:o_u + d_conv])
    cu_buf[HALO:HALO + tm, :] = cu
    scw = scw_ref[...]
    conv = (scw[0:1] * cu_buf[HALO - 2:HALO - 2 + tm, :]
            + scw[1:2] * cu_buf[HALO - 1:HALO - 1 + tm, :]
            + scw[2:3] * cu)
    cu_buf[0:HALO, :] = cu[tm - HALO:tm]
    y_conv = sc_b * conv

    lx = _dot(h, w_in_ref[:, o_lx:o_lx + d_lru])
    lx_buf[HALO:HALO + tm, :] = lx
    lcw = lcw_ref[...]
    xr = (lcw[0:1] * lx_buf[HALO - 3:HALO - 3 + tm, :]
          + lcw[1:2] * lx_buf[HALO - 2:HALO - 2 + tm, :]
          + lcw[2:3] * lx_buf[HALO - 1:HALO - 1 + tm, :]
          + lcw[3:4] * lx) + lcb_ref[...]
    lx_buf[0:HALO, :] = lx[tm - HALO:tm]
    xr_b = xr.astype(BF16)
    r = jax.nn.sigmoid(_dot(xr_b, wa_ref[...]) + ba_ref[...])
    i = jax.nn.sigmoid(_dot(xr_b, wx_ref[...]) + bx_ref[...])
    log_a = (-LRU_C) * r * _softplus(-lam_ref[...])
    a = jnp.exp(log_a)
    t = jnp.tanh(log_a)
    u = jnp.sqrt((-2.0 * t) / (1.0 - t)) * (i * xr)
    a_cum, h_loc = _scan_linear_recurrence(a, u)
    hs = h_loc + a_cum * h_carry[...]
    h_carry[...] = hs[tm - 1:tm]
    y_lru = hs * _gelu_tanh(_dot(h, w_in_ref[:, o_lg:o_lg + d_lru]))

    mng = mng_ref[...]
    ycl_ref[0, :, 0:d_conv] = (y_conv * _rms(y_conv) * mng[:, 0:d_conv]).astype(BF16)
    ycl_ref[0, :, d_conv:d_conv + d_lru] = (y_lru * _rms(y_lru) * mng[:, d_conv:]).astype(BF16)


def _mixer_in(x, mod_l, ln1_g, w_in_b, qg, kg, headsum, sc_w, lcw, lcb, wa_bd, ba, wx_bd, bx, lam, mng_cl,
              *, d_attn, d_conv, d_lru):
    batch, seq, d_model = x.shape
    tm = TOKEN_TILE
    n_blk = tm // MOBA_BLOCK
    nb = seq // MOBA_BLOCK

    def const(shape):
        return pl.BlockSpec(shape, lambda b, s: (0,) * len(shape))

    kern = functools.partial(_mixer_in_kernel, d_attn=d_attn, d_conv=d_conv, d_lru=d_lru)
    return pl.pallas_call(
        kern,
        grid=(batch, seq // tm),
        in_specs=[
            pl.BlockSpec((1, tm, d_model), lambda b, s: (b, s, 0)),
            pl.BlockSpec((1, N_MOD, d_model), lambda b, s: (b, 0, 0)),
            const((1, d_model)),
            const(w_in_b.shape),
            const((1, d_attn)), const((1, d_attn)), const((d_attn, d_attn)),
            const(sc_w.shape), const(lcw.shape), const((1, d_lru)),
            const((d_lru, d_lru)), const((1, d_lru)), const((d_lru, d_lru)), const((1, d_lru)),
            const((1, d_lru)), const((1, d_conv + d_lru)),
        ],
        out_specs=[
            pl.BlockSpec((1, n_blk, d_attn, MOBA_BLOCK), lambda b, s: (b, s, 0, 0)),
            pl.BlockSpec((1, tm, d_attn), lambda b, s: (b, s, 0)),
            pl.BlockSpec((1, n_blk, d_attn, MOBA_BLOCK), lambda b, s: (b, s, 0, 0)),
            pl.BlockSpec((1, n_blk, 1, d_attn), lambda b, s: (b, s, 0, 0)),
            pl.BlockSpec((1, tm, d_conv + d_lru), lambda b, s: (b, s, 0)),
        ],
        out_shape=[
            jax.ShapeDtypeStruct((batch, nb, d_attn, MOBA_BLOCK), BF16),
            jax.ShapeDtypeStruct((batch, seq, d_attn), BF16),
            jax.ShapeDtypeStruct((batch, nb, d_attn, MOBA_BLOCK), BF16),
            jax.ShapeDtypeStruct((batch, nb, 1, d_attn), F32),
            jax.ShapeDtypeStruct((batch, seq, d_conv + d_lru), BF16),
        ],
        scratch_shapes=[
            pltpu.VMEM((HALO + tm, d_conv), F32),
            pltpu.VMEM((HALO + tm, d_lru), F32),
            pltpu.VMEM((1, d_lru), F32),
        ],
        compiler_params=pltpu.CompilerParams(
            dimension_semantics=("arbitrary", "arbitrary"), vmem_limit_bytes=VMEM_LIMIT_BYTES),
        name="mixer_in",
    )(x, mod_l, ln1_g, w_in_b, qg, kg, headsum, sc_w, lcw, lcb, wa_bd, ba, wx_bd, bx, lam, mng_cl)


def _moba_kernel(qT_ref, k_ref, vT_ref, kmean_ref, g_ref, o_ref, bias_ref):
    i = pl.program_id(1)
    blk = MOBA_BLOCK
    d_attn = qT_ref.shape[2]
    n_heads = d_attn // HEAD_DIM
    nb = kmean_ref.shape[1]
    pair = 2 * HEAD_DIM

    half = lax.broadcasted_iota(jnp.int32, (pair, blk), 0) < HEAD_DIM
    blk_id = lax.broadcasted_iota(jnp.int32, (nb, blk), 0)
    past = blk_id < i
    causal = (lax.broadcasted_iota(jnp.int32, (blk, blk), 0)
              <= lax.broadcasted_iota(jnp.int32, (blk, blk), 1))
    kmean = kmean_ref[0, :, 0, :]

    outs = []
    for hd in range(n_heads):
        p0 = (hd // 2) * pair
        lo = hd % 2 == 0
        q_pair = qT_ref[0, 0, p0:p0 + pair, :]
        q_m = jnp.where(half if lo else jnp.logical_not(half), q_pair, jnp.zeros_like(q_pair))

        gate = _dot(kmean[:, p0:p0 + pair].astype(BF16), q_m)
        gate = jnp.where(past, gate, -jnp.inf)
        rank = jnp.zeros((nb, blk), F32)
        for jp in range(nb):
            g_j = gate[jp:jp + 1, :]
            beats = jnp.logical_or(g_j > gate, jnp.logical_and(g_j == gate, jp < blk_id))
            rank = rank + jnp.where(beats, 1.0, 0.0)
        keep = jnp.logical_and(past, rank < MOBA_TOPK)
        bias_ref[hd] = jnp.where(keep, 0.0, -jnp.inf)

        r0 = hd * HEAD_DIM
        s_t = _dot(k_ref[0, pl.ds(pl.multiple_of(i * blk, blk), blk), p0:p0 + pair], q_m)
        s_t = jnp.where(causal, s_t, -jnp.inf)
        m0 = jnp.max(s_t, axis=0, keepdims=True)
        p = jnp.exp(s_t - m0)
        l0 = jnp.sum(p, axis=0, keepdims=True)
        acc0 = _dot(vT_ref[0, i, r0:r0 + HEAD_DIM, :], p.astype(BF16))

        def body(j, carry, p0=p0, r0=r0, q_m=q_m, hd=hd):
            m_prev, l_prev, acc = carry
            k_j = k_ref[0, pl.ds(pl.multiple_of(j * blk, blk), blk), p0:p0 + pair]
            s_t = _dot(k_j, q_m) + bias_ref[hd, pl.ds(j, 1), :]
            m_new = jnp.maximum(m_prev, jnp.max(s_t, axis=0, keepdims=True))
            alpha = jnp.exp(m_prev - m_new)
            p = jnp.exp(s_t - m_new)
            l_new = alpha * l_prev + jnp.sum(p, axis=0, keepdims=True)
            acc = alpha * acc + _dot(vT_ref[0, j, r0:r0 + HEAD_DIM, :], p.astype(BF16))
            return m_new, l_new, acc

        _, l_fin, acc = lax.fori_loop(0, i, body, (m0, l0, acc0))
        outs.append(acc / l_fin)

    y_t = jnp.concatenate(outs, axis=0)
    y_t = y_t * _rms(y_t, axis=0)
    o_ref[0] = (y_t.T * g_ref[...]).astype(BF16)


def _moba(qT, k, vT, kmean, mng_attn):
    batch, nb, d_attn, blk = qT.shape
    seq = k.shape[1]
    n_heads = d_attn // HEAD_DIM
    return pl.pallas_call(
        _moba_kernel,
        grid=(batch, nb),
        in_specs=[
            pl.BlockSpec((1, 1, d_attn, blk), lambda b, i: (b, i, 0, 0)),
            pl.BlockSpec((1, seq, d_attn), lambda b, i: (b, 0, 0)),
            pl.BlockSpec((1, nb, d_attn, blk), lambda b, i: (b, 0, 0, 0)),
            pl.BlockSpec((1, nb, 1, d_attn), lambda b, i: (b, 0, 0, 0)),
            pl.BlockSpec((1, d_attn), lambda b, i: (0, 0)),
        ],
        out_specs=pl.BlockSpec((1, blk, d_attn), lambda b, i: (b, i, 0)),
        out_shape=jax.ShapeDtypeStruct((batch, seq, d_attn), BF16),
        scratch_shapes=[pltpu.VMEM((n_heads, nb, blk), F32)],
        compiler_params=pltpu.CompilerParams(
            dimension_semantics=("arbitrary", "arbitrary"), vmem_limit_bytes=VMEM_LIMIT_BYTES),
        name="moba_attention",
    )(qT, k, vT, kmean, mng_attn)


def _out_mlp_kernel(x_ref, ya_ref, ycl_ref, mod_ref, ln2_ref, wout_ref, wup_ref, wdown_ref, o_ref):
    d_attn = ya_ref.shape[2]
    d_ff = wup_ref.shape[1]
    x = x_ref[0]
    m = mod_ref[0]
    gate1, shift2, scale2, gate2 = m[2:3], m[3:4], m[4:5], m[5:6]
    mix = _dot(ya_ref[0], wout_ref[0:d_attn, :]) + _dot(ycl_ref[0], wout_ref[d_attn:, :])
    x1 = x + gate1 * mix
    h2 = ((x1 * _rms(x1) * ln2_ref[...]) * (1.0 + scale2) + shift2).astype(BF16)
    ff = jnp.zeros_like(x1)
    for c0 in range(0, d_ff, FF_CHUNK):
        up = _dot(h2, wup_ref[:, c0:c0 + FF_CHUNK])
        act = jnp.square(jnp.maximum(up, 0.0)).astype(BF16)
        ff = ff + _dot(act, wdown_ref[c0:c0 + FF_CHUNK, :])
    o_ref[0] = x1 + gate2 * ff


def _out_mlp(x, ya, ycl, mod_l, ln2_g, w_out_b, w_up_b, w_down_b):
    batch, seq, d_model = x.shape
    tm = TOKEN_TILE
    d_attn = ya.shape[2]
    d_cl = ycl.shape[2]

    def const(shape):
        return pl.BlockSpec(shape, lambda b, s: (0,) * len(shape), pipeline_mode=pl.Buffered(1))

    return pl.pallas_call(
        _out_mlp_kernel,
        grid=(batch, seq // tm),
        in_specs=[
            pl.BlockSpec((1, tm, d_model), lambda b, s: (b, s, 0)),
            pl.BlockSpec((1, tm, d_attn), lambda b, s: (b, s, 0)),
            pl.BlockSpec((1, tm, d_cl), lambda b, s: (b, s, 0)),
            pl.BlockSpec((1, N_MOD, d_model), lambda b, s: (b, 0, 0)),
            pl.BlockSpec((1, d_model), lambda b, s: (0, 0)),
            const(w_out_b.shape), const(w_up_b.shape), const(w_down_b.shape),
        ],
        out_specs=pl.BlockSpec((1, tm, d_model), lambda b, s: (b, s, 0)),
        out_shape=jax.ShapeDtypeStruct((batch, seq, d_model), F32),
        compiler_params=pltpu.CompilerParams(
            dimension_semantics=("arbitrary", "arbitrary"), vmem_limit_bytes=VMEM_LIMIT_BYTES),
        name="out_mlp",
    )(x, ya, ycl, mod_l, ln2_g, w_out_b, w_up_b, w_down_b)


def _block_diag(w):
    n, r, c = w.shape
    eye = jnp.eye(n, dtype=w.dtype)
    return (eye[:, None, :, None] * w[:, :, None, :]).reshape(n * r, n * c)


def kernel(x, c, ln1_g, ln2_g, w_ada, b_ada, w_in, q_norm_g, k_norm_g, sc_w, lru_conv_w, lru_conv_b,
           lru_wa, lru_ba, lru_wx, lru_bx, lru_lambda, mix_norm_g, w_out, w_up, w_down):
    batch, seq, d_model = x.shape
    depth = w_in.shape[0]
    d_conv = sc_w.shape[2]
    d_lru = lru_conv_w.shape[2]
    d_attn = mix_norm_g.shape[1] - d_conv - d_lru
    n_heads = d_attn // HEAD_DIM
    assert seq % TOKEN_TILE == 0 and TOKEN_TILE % MOBA_BLOCK == 0
    assert w_in.shape[2] == 3 * d_attn + 3 * d_conv + 2 * d_lru

    mod = _modulation(c, w_ada, b_ada).reshape(depth, batch, N_MOD, d_model)
    headsum = _block_diag(jnp.full((n_heads, HEAD_DIM, HEAD_DIM), 1.0 / HEAD_DIM, F32)).astype(BF16)

    for l in range(depth):
        row = lambda v: v.reshape(1, -1)
        qT, k, vT, kmean, ycl = _mixer_in(
            x, mod[l], row(ln1_g[l]), w_in[l].astype(BF16),
            row(jnp.tile(q_norm_g[l], n_heads)), row(jnp.tile(k_norm_g[l], n_heads)), headsum,
            sc_w[l], lru_conv_w[l], row(lru_conv_b[l]),
            _block_diag(lru_wa[l]).astype(BF16), row(lru_ba[l]),
            _block_diag(lru_wx[l]).astype(BF16), row(lru_bx[l]),
            row(lru_lambda[l]), row(mix_norm_g[l, d_attn:]),
            d_attn=d_attn, d_conv=d_conv, d_lru=d_lru)
        ya = _moba(qT, k, vT, kmean, row(mix_norm_g[l, :d_attn]))
        x = _out_mlp(x, ya, ycl, mod[l], row(ln2_g[l]),
                     w_out[l].astype(BF16), w_up[l].astype(BF16), w_down[l].astype(BF16))
    return x
```

```python
import functools
import math

import jax
import jax.numpy as jnp
from jax import lax
from jax.experimental import pallas as pl
from jax.experimental.pallas import tpu as pltpu

F32 = jnp.float32
BF16 = jnp.bfloat16

HEAD_DIM = 64
MOBA_BLOCK = 256
MOBA_TOPK = 3
LRU_C = 8.0
N_MOD = 6
EPS = 1e-6

TOKEN_TILE = 512
MOD_COL_TILE = 1536
FF_CHUNK = 1024
HALO = 8
VMEM_LIMIT_BYTES = 56 * 1024 * 1024


def _rms(x, axis=-1):
    return lax.rsqrt(jnp.mean(x * x, axis=axis, keepdims=True) + EPS)


def _dot(a, b):
    return jnp.dot(a, b, preferred_element_type=F32)


def _mod_kernel(c_ref, w_ref, b_ref, o_ref):
    c = c_ref[...]
    c_act = (c * jax.nn.sigmoid(c)).astype(BF16)
    o_ref[0] = _dot(c_act, w_ref[0].astype(BF16)) + b_ref[0]


def _modulation(c, w_ada, b_ada):
    depth, d_model, n_out = w_ada.shape
    batch = c.shape[0]
    return pl.pallas_call(
        _mod_kernel,
        grid=(depth, n_out // MOD_COL_TILE),
        in_specs=[
            pl.BlockSpec((batch, d_model), lambda l, j: (0, 0)),
            pl.BlockSpec((1, d_model, MOD_COL_TILE), lambda l, j: (l, 0, j)),
            pl.BlockSpec((1, 1, MOD_COL_TILE), lambda l, j: (l, 0, j)),
        ],
        out_specs=pl.BlockSpec((1, batch, MOD_COL_TILE), lambda l, j: (l, 0, j)),
        out_shape=jax.ShapeDtypeStruct((depth, batch, n_out), F32),
        compiler_params=pltpu.CompilerParams(
            dimension_semantics=("arbitrary", "arbitrary"), vmem_limit_bytes=VMEM_LIMIT_BYTES),
        name="adaln_modulation",
    )(c, w_ada, b_ada.reshape(depth, 1, n_out))


def _scan_linear_recurrence(a, u):
    n = a.shape[0]
    row = lax.broadcasted_iota(jnp.int32, a.shape, 0)
    d = 1
    while d < n:
        keep = row >= d
        a_prev = jnp.where(keep, pltpu.roll(a, d, 0), 1.0)
        u_prev = jnp.where(keep, pltpu.roll(u, d, 0), 0.0)
        u = a * u_prev + u
        a = a * a_prev
        d *= 2
    return a, u


def _gelu_tanh(x):
    return 0.5 * x * (1.0 + jnp.tanh(math.sqrt(2.0 / math.pi) * (x + 0.044715 * (x * x * x))))


def _softplus(z):
    return jnp.maximum(z, 0.0) + jnp.log1p(jnp.exp(-jnp.abs(z)))


def _mixer_in_kernel(x_ref, mod_ref, ln1_ref, w_in_ref, qg_ref, kg_ref, headsum_ref, scw_ref,
                     lcw_ref, lcb_ref, wa_ref, ba_ref, wx_ref, bx_ref, lam_ref, mng_ref,
                     qT_ref, k_ref, vT_ref, kmean_ref, ycl_ref,
                     cu_buf, lx_buf, h_carry, *, d_attn, d_conv, d_lru):
    s = pl.program_id(1)
    tm = x_ref.shape[1]
    n_blk = tm // MOBA_BLOCK

    @pl.when(s == 0)
    def _():
        cu_buf[0:HALO, :] = jnp.zeros((HALO, d_conv), F32)
        lx_buf[0:HALO, :] = jnp.zeros((HALO, d_lru), F32)
        h_carry[...] = jnp.zeros_like(h_carry)

    x = x_ref[0]
    m = mod_ref[0]
    shift1, scale1 = m[0:1], m[1:2]
    h = ((x * _rms(x) * ln1_ref[...]) * (1.0 + scale1) + shift1).astype(BF16)

    o_q, o_k, o_v = 0, d_attn, 2 * d_attn
    o_b = 3 * d_attn
    o_c, o_u = o_b + d_conv, o_b + 2 * d_conv
    o_lx = o_b + 3 * d_conv
    o_lg = o_lx + d_lru

    def head_norm(t, g_ref):
        ms = _dot((t * t).astype(BF16), headsum_ref[...])
        return t * lax.rsqrt(ms + EPS) * g_ref[...]

    q = head_norm(_dot(h, w_in_ref[:, o_q:o_q + d_attn]), qg_ref)
    qT = (q * (1.0 / math.sqrt(HEAD_DIM))).T.astype(BF16)
    for c in range(n_blk):
        qT_ref[0, c] = qT[:, c * MOBA_BLOCK:(c + 1) * MOBA_BLOCK]

    k = head_norm(_dot(h, w_in_ref[:, o_k:o_k + d_attn]), kg_ref)
    k_ref[0] = k.astype(BF16)
    for c in range(n_blk):
        kmean_ref[0, c] = jnp.mean(k[c * MOBA_BLOCK:(c + 1) * MOBA_BLOCK], axis=0, keepdims=True)

    vT = _dot(h, w_in_ref[:, o_v:o_v + d_attn]).T.astype(BF16)
    for c in range(n_blk):
        vT_ref[0, c] = vT[:, c * MOBA_BLOCK:(c + 1) * MOBA_BLOCK]

    sc_b = _dot(h, w_in_ref[:, o_b:o_b + d_conv])
    cu = _dot(h, w_in_ref[:, o_c:o_c + d_conv]) * _dot(h, w_in_ref[:, o_u:o_u + d_conv])
    cu_buf[HALO:HALO + tm, :] = cu
    scw = scw_ref[...]
    conv = (scw[0:1] * cu_buf[HALO - 2:HALO - 2 + tm, :]
            + scw[1:2] * cu_buf[HALO - 1:HALO - 1 + tm, :]
            + scw[2:3] * cu)
    cu_buf[0:HALO, :] = cu[tm - HALO:tm]
    y_conv = sc_b * conv

    lx = _dot(h, w_in_ref[:, o_lx:o_lx + d_lru])
    lx_buf[HALO:HALO + tm, :] = lx
    lcw = lcw_ref[...]
    xr = (lcw[0:1] * lx_buf[HALO - 3:HALO - 3 + tm, :]
          + lcw[1:2] * lx_buf[HALO - 2:HALO - 2 + tm, :]
          + lcw[2:3] * lx_buf[HALO - 1:HALO - 1 + tm, :]
          + lcw[3:4] * lx) + lcb_ref[...]
    lx_buf[0:HALO, :] = lx[tm - HALO:tm]
    xr_b = xr.astype(BF16)
    r = jax.nn.sigmoid(_dot(xr_b, wa_ref[...]) + ba_ref[...])
    i = jax.nn.sigmoid(_dot(xr_b, wx_ref[...]) + bx_ref[...])
    log_a = (-LRU_C) * r * _softplus(-lam_ref[...])
    a = jnp.exp(log_a)
    t = jnp.tanh(log_a)
    u = jnp.sqrt((-2.0 * t) / (1.0 - t)) * (i * xr)
    a_cum, h_loc = _scan_linear_recurrence(a, u)
    hs = h_loc + a_cum * h_carry[...]
    h_carry[...] = hs[tm - 1:tm]
    y_lru = hs * _gelu_tanh(_dot(h, w_in_ref[:, o_lg:o_lg + d_lru]))

    mng = mng_ref[...]
    ycl_ref[0, :, 0:d_conv] = (y_conv * _rms(y_conv) * mng[:, 0:d_conv]).astype(BF16)
    ycl_ref[0, :, d_conv:d_conv + d_lru] = (y_lru * _rms(y_lru) * mng[:, d_conv:]).astype(BF16)


def _mixer_in(x, mod_l, ln1_g, w_in_b, qg, kg, headsum, sc_w, lcw, lcb, wa_bd, ba, wx_bd, bx, lam, mng_cl,
              *, d_attn, d_conv, d_lru):
    batch, seq, d_model = x.shape
    tm = TOKEN_TILE
    n_blk = tm // MOBA_BLOCK
    nb = seq // MOBA_BLOCK

    def const(shape):
        return pl.BlockSpec(shape, lambda b, s: (0,) * len(shape))

    kern = functools.partial(_mixer_in_kernel, d_attn=d_attn, d_conv=d_conv, d_lru=d_lru)
    return pl.pallas_call(
        kern,
        grid=(batch, seq // tm),
        in_specs=[
            pl.BlockSpec((1, tm, d_model), lambda b, s: (b, s, 0)),
            pl.BlockSpec((1, N_MOD, d_model), lambda b, s: (b, 0, 0)),
            const((1, d_model)),
            const(w_in_b.shape),
            const((1, d_attn)), const((1, d_attn)), const((d_attn, d_attn)),
            const(sc_w.shape), const(lcw.shape), const((1, d_lru)),
            const((d_lru, d_lru)), const((1, d_lru)), const((d_lru, d_lru)), const((1, d_lru)),
            const((1, d_lru)), const((1, d_conv + d_lru)),
        ],
        out_specs=[
            pl.BlockSpec((1, n_blk, d_attn, MOBA_BLOCK), lambda b, s: (b, s, 0, 0)),
            pl.BlockSpec((1, tm, d_attn), lambda b, s: (b, s, 0)),
            pl.BlockSpec((1, n_blk, d_attn, MOBA_BLOCK), lambda b, s: (b, s, 0, 0)),
            pl.BlockSpec((1, n_blk, 1, d_attn), lambda b, s: (b, s, 0, 0)),
            pl.BlockSpec((1, tm, d_conv + d_lru), lambda b, s: (b, s, 0)),
        ],
        out_shape=[
            jax.ShapeDtypeStruct((batch, nb, d_attn, MOBA_BLOCK), BF16),
            jax.ShapeDtypeStruct((batch, seq, d_attn), BF16),
            jax.ShapeDtypeStruct((batch, nb, d_attn, MOBA_BLOCK), BF16),
            jax.ShapeDtypeStruct((batch, nb, 1, d_attn), F32),
            jax.ShapeDtypeStruct((batch, seq, d_conv + d_lru), BF16),
        ],
        scratch_shapes=[
            pltpu.VMEM((HALO + tm, d_conv), F32),
            pltpu.VMEM((HALO + tm, d_lru), F32),
            pltpu.VMEM((1, d_lru), F32),
        ],
        compiler_params=pltpu.CompilerParams(
            dimension_semantics=("arbitrary", "arbitrary"), vmem_limit_bytes=VMEM_LIMIT_BYTES),
        name="mixer_in",
    )(x, mod_l, ln1_g, w_in_b, qg, kg, headsum, sc_w, lcw, lcb, wa_bd, ba, wx_bd, bx, lam, mng_cl)


def _sublane_fold(x, op):
    tiles = [x[r:r + 8] for r in range(0, x.shape[0], 8)]
    while len(tiles) > 1:
        tiles = [op(a, b) for a, b in zip(tiles[0::2], tiles[1::2])] + tiles[len(tiles) & ~1:]
    return tiles[0]


def _moba_kernel(qT_ref, k_ref, vT_ref, kmean_ref, g_ref, o_ref,
                 qcat_ref, bias_ref, s_ref, mx_ref, l_ref, acc_ref):
    i = pl.program_id(1)
    blk = MOBA_BLOCK
    d_attn = qT_ref.shape[2]
    n_heads = d_attn // HEAD_DIM
    n_pairs = n_heads // 2
    nb = kmean_ref.shape[1]
    pair = 2 * HEAD_DIM
    wide = 2 * blk

    half = lax.broadcasted_iota(jnp.int32, (pair, blk), 0) < HEAD_DIM
    blk_id = lax.broadcasted_iota(jnp.int32, (nb, blk), 0)
    past = blk_id < i
    causal = (lax.broadcasted_iota(jnp.int32, (blk, wide), 0)
              <= (lax.broadcasted_iota(jnp.int32, (blk, wide), 1) & (blk - 1)))
    kmean = kmean_ref[0, :, 0, :]
    own = pl.ds(pl.multiple_of(i * blk, blk), blk)

    for p in range(n_pairs):
        p0 = p * pair
        c0 = p * wide
        q_pair = qT_ref[0, 0, p0:p0 + pair, :]
        zero = jnp.zeros_like(q_pair)
        for hh in range(2):
            hd = 2 * p + hh
            q_m = jnp.where(half if hh == 0 else jnp.logical_not(half), q_pair, zero)
            qcat_ref[p, :, hh * blk:(hh + 1) * blk] = q_m
            gate = _dot(kmean[:, p0:p0 + pair].astype(BF16), q_m)
            gate = jnp.where(past, gate, -jnp.inf)
            rank = jnp.zeros((nb, blk), F32)
            for jp in range(nb):
                g_j = gate[jp:jp + 1, :]
                beats = jnp.logical_or(g_j > gate, jnp.logical_and(g_j == gate, jp < blk_id))
                rank = rank + jnp.where(beats, 1.0, 0.0)
            keep = jnp.logical_and(past, rank < MOBA_TOPK)
            bias_ref[:, hd * blk:(hd + 1) * blk] = jnp.where(keep, 0.0, -jnp.inf)

        s_t = jnp.where(causal, _dot(k_ref[0, own, p0:p0 + pair], qcat_ref[p]), -jnp.inf)
        s_ref[i, :, c0:c0 + wide] = s_t
        mx_ref[:, c0:c0 + wide] = _sublane_fold(s_t, jnp.maximum)

    def score_pass(j, carry):
        rows = pl.ds(pl.multiple_of(j * blk, blk), blk)
        for p in range(n_pairs):
            c0 = p * wide
            s_t = (_dot(k_ref[0, rows, p * pair:(p + 1) * pair], qcat_ref[p])
                   + bias_ref[pl.ds(j, 1), c0:c0 + wide])
            s_ref[j, :, c0:c0 + wide] = s_t
            mx_ref[:, c0:c0 + wide] = jnp.maximum(mx_ref[:, c0:c0 + wide], _sublane_fold(s_t, jnp.maximum))
        return carry

    lax.fori_loop(0, i, score_pass, 0)

    m_fin = jnp.max(mx_ref[...], axis=0, keepdims=True)
    l_ref[...] = jnp.zeros_like(l_ref)
    acc_ref[...] = jnp.zeros_like(acc_ref)

    def value_pass(j, carry):
        for hd in range(n_heads):
            r0 = hd * HEAD_DIM
            c0 = hd * blk
            p_t = jnp.exp(s_ref[j, :, c0:c0 + blk] - m_fin[:, c0:c0 + blk])
            l_ref[:, c0:c0 + blk] += _sublane_fold(p_t, jnp.add)
            acc_ref[r0:r0 + HEAD_DIM, :] += _dot(vT_ref[0, j, r0:r0 + HEAD_DIM, :], p_t.astype(BF16))
        return carry

    lax.fori_loop(0, i + 1, value_pass, 0)

    l_fin = jnp.sum(l_ref[...], axis=0, keepdims=True)
    y_t = jnp.concatenate(
        [acc_ref[hd * HEAD_DIM:(hd + 1) * HEAD_DIM, :] / l_fin[:, hd * blk:(hd + 1) * blk]
         for hd in range(n_heads)], axis=0)
    y_t = y_t * _rms(y_t, axis=0)
    o_ref[0] = (y_t.T * g_ref[...]).astype(BF16)


def _moba(qT, k, vT, kmean, mng_attn):
    batch, nb, d_attn, blk = qT.shape
    seq = k.shape[1]
    n_heads = d_attn // HEAD_DIM
    cols = n_heads * blk

    def resident(shape, index_map):
        return pl.BlockSpec(shape, index_map, pipeline_mode=pl.Buffered(1))

    return pl.pallas_call(
        _moba_kernel,
        grid=(batch, nb),
        in_specs=[
            pl.BlockSpec((1, 1, d_attn, blk), lambda b, i: (b, i, 0, 0)),
            resident((1, seq, d_attn), lambda b, i: (b, 0, 0)),
            resident((1, nb, d_attn, blk), lambda b, i: (b, 0, 0, 0)),
            pl.BlockSpec((1, nb, 1, d_attn), lambda b, i: (b, 0, 0, 0)),
            pl.BlockSpec((1, d_attn), lambda b, i: (0, 0)),
        ],
        out_specs=pl.BlockSpec((1, blk, d_attn), lambda b, i: (b, i, 0)),
        out_shape=jax.ShapeDtypeStruct((batch, seq, d_attn), BF16),
        scratch_shapes=[
            pltpu.VMEM((n_heads // 2, 2 * HEAD_DIM, 2 * blk), BF16),
            pltpu.VMEM((nb, cols), F32),
            pltpu.VMEM((nb, blk, cols), F32),
            pltpu.VMEM((8, cols), F32),
            pltpu.VMEM((8, cols), F32),
            pltpu.VMEM((d_attn, blk), F32),
        ],
        compiler_params=pltpu.CompilerParams(
            dimension_semantics=("arbitrary", "arbitrary"), vmem_limit_bytes=VMEM_LIMIT_BYTES),
        name="moba_attention",
    )(qT, k, vT, kmean, mng_attn)


def _out_mlp_kernel(x_ref, ya_ref, ycl_ref, mod_ref, ln2_ref, wout_ref, wup_ref, wdown_ref, o_ref):
    d_attn = ya_ref.shape[2]
    d_ff = wup_ref.shape[1]
    x = x_ref[0]
    m = mod_ref[0]
    gate1, shift2, scale2, gate2 = m[2:3], m[3:4], m[4:5], m[5:6]
    mix = _dot(ya_ref[0], wout_ref[0:d_attn, :]) + _dot(ycl_ref[0], wout_ref[d_attn:, :])
    x1 = x + gate1 * mix
    h2 = ((x1 * _rms(x1) * ln2_ref[...]) * (1.0 + scale2) + shift2).astype(BF16)
    ff = jnp.zeros_like(x1)
    for c0 in range(0, d_ff, FF_CHUNK):
        up = _dot(h2, wup_ref[:, c0:c0 + FF_CHUNK])
        act = jnp.square(jnp.maximum(up, 0.0)).astype(BF16)
        ff = ff + _dot(act, wdown_ref[c0:c0 + FF_CHUNK, :])
    o_ref[0] = x1 + gate2 * ff


def _out_mlp(x, ya, ycl, mod_l, ln2_g, w_out_b, w_up_b, w_down_b):
    batch, seq, d_model = x.shape
    tm = TOKEN_TILE
    d_attn = ya.shape[2]
    d_cl = ycl.shape[2]

    def const(shape):
        return pl.BlockSpec(shape, lambda b, s: (0,) * len(shape), pipeline_mode=pl.Buffered(1))

    return pl.pallas_call(
        _out_mlp_kernel,
        grid=(batch, seq // tm),
        in_specs=[
            pl.BlockSpec((1, tm, d_model), lambda b, s: (b, s, 0)),
            pl.BlockSpec((1, tm, d_attn), lambda b, s: (b, s, 0)),
            pl.BlockSpec((1, tm, d_cl), lambda b, s: (b, s, 0)),
            pl.BlockSpec((1, N_MOD, d_model), lambda b, s: (b, 0, 0)),
            pl.BlockSpec((1, d_model), lambda b, s: (0, 0)),
            const(w_out_b.shape), const(w_up_b.shape), const(w_down_b.shape),
        ],
        out_specs=pl.BlockSpec((1, tm, d_model), lambda b, s: (b, s, 0)),
        out_shape=jax.ShapeDtypeStruct((batch, seq, d_model), F32),
        compiler_params=pltpu.CompilerParams(
            dimension_semantics=("arbitrary", "arbitrary"), vmem_limit_bytes=VMEM_LIMIT_BYTES),
        name="out_mlp",
    )(x, ya, ycl, mod_l, ln2_g, w_out_b, w_up_b, w_down_b)


def _block_diag(w):
    n, r, c = w.shape
    eye = jnp.eye(n, dtype=w.dtype)
    return (eye[:, None, :, None] * w[:, :, None, :]).reshape(n * r, n * c)


def kernel(x, c, ln1_g, ln2_g, w_ada, b_ada, w_in, q_norm_g, k_norm_g, sc_w, lru_conv_w, lru_conv_b,
           lru_wa, lru_ba, lru_wx, lru_bx, lru_lambda, mix_norm_g, w_out, w_up, w_down):
    batch, seq, d_model = x.shape
    depth = w_in.shape[0]
    d_conv = sc_w.shape[2]
    d_lru = lru_conv_w.shape[2]
    d_attn = mix_norm_g.shape[1] - d_conv - d_lru
    n_heads = d_attn // HEAD_DIM
    assert seq % TOKEN_TILE == 0 and TOKEN_TILE % MOBA_BLOCK == 0
    assert w_in.shape[2] == 3 * d_attn + 3 * d_conv + 2 * d_lru

    mod = _modulation(c, w_ada, b_ada).reshape(depth, batch, N_MOD, d_model)
    headsum = _block_diag(jnp.full((n_heads, HEAD_DIM, HEAD_DIM), 1.0 / HEAD_DIM, F32)).astype(BF16)

    for l in range(depth):
        row = lambda v: v.reshape(1, -1)
        qT, k, vT, kmean, ycl = _mixer_in(
            x, mod[l], row(ln1_g[l]), w_in[l].astype(BF16),
            row(jnp.tile(q_norm_g[l], n_heads)), row(jnp.tile(k_norm_g[l], n_heads)), headsum,
            sc_w[l], lru_conv_w[l], row(lru_conv_b[l]),
            _block_diag(lru_wa[l]).astype(BF16), row(lru_ba[l]),
            _block_diag(lru_wx[l]).astype(BF16), row(lru_bx[l]),
            row(lru_lambda[l]), row(mix_norm_g[l, d_attn:]),
            d_attn=d_attn, d_conv=d_conv, d_lru=d_lru)
        ya = _moba(qT, k, vT, kmean, row(mix_norm_g[l, :d_attn]))
        x = _out_mlp(x, ya, ycl, mod[l], row(ln2_g[l]),
                     w_out[l].astype(BF16), w_up[l].astype(BF16), w_down[l].astype(BF16))
    return x
```

```python
import functools
import math

import jax
import jax.numpy as jnp
from jax import lax
from jax.experimental import pallas as pl
from jax.experimental.pallas import tpu as pltpu

F32 = jnp.float32
BF16 = jnp.bfloat16

HEAD_DIM = 64
MOBA_BLOCK = 256
MOBA_TOPK = 3
LRU_C = 8.0
N_MOD = 6
EPS = 1e-6

TOKEN_TILE = 512
MOD_COL_TILE = 1536
FF_CHUNK = 1024
HALO = 8
ACC_ROWS = HEAD_DIM + 16
LOG2_E = math.log2(math.e)
VMEM_LIMIT_BYTES = 56 * 1024 * 1024


def _rms(x, axis=-1):
    return lax.rsqrt(jnp.mean(x * x, axis=axis, keepdims=True) + EPS)


def _dot(a, b):
    return jnp.dot(a, b, preferred_element_type=F32)


def _mod_kernel(c_ref, w_ref, b_ref, o_ref):
    c = c_ref[...]
    c_act = (c * jax.nn.sigmoid(c)).astype(BF16)
    o_ref[0] = _dot(c_act, w_ref[0].astype(BF16)) + b_ref[0]


def _modulation(c, w_ada, b_ada):
    depth, d_model, n_out = w_ada.shape
    batch = c.shape[0]
    return pl.pallas_call(
        _mod_kernel,
        grid=(depth, n_out // MOD_COL_TILE),
        in_specs=[
            pl.BlockSpec((batch, d_model), lambda l, j: (0, 0)),
            pl.BlockSpec((1, d_model, MOD_COL_TILE), lambda l, j: (l, 0, j)),
            pl.BlockSpec((1, 1, MOD_COL_TILE), lambda l, j: (l, 0, j)),
        ],
        out_specs=pl.BlockSpec((1, batch, MOD_COL_TILE), lambda l, j: (l, 0, j)),
        out_shape=jax.ShapeDtypeStruct((depth, batch, n_out), F32),
        compiler_params=pltpu.CompilerParams(
            dimension_semantics=("arbitrary", "arbitrary"), vmem_limit_bytes=VMEM_LIMIT_BYTES),
        name="adaln_modulation",
    )(c, w_ada, b_ada.reshape(depth, 1, n_out))


def _scan_linear_recurrence(a, u):
    n = a.shape[0]
    row = lax.broadcasted_iota(jnp.int32, a.shape, 0)
    d = 1
    while d < n:
        keep = row >= d
        a_prev = jnp.where(keep, pltpu.roll(a, d, 0), 1.0)
        u_prev = jnp.where(keep, pltpu.roll(u, d, 0), 0.0)
        u = a * u_prev + u
        a = a * a_prev
        d *= 2
    return a, u


def _gelu_tanh(x):
    return 0.5 * x * (1.0 + jnp.tanh(math.sqrt(2.0 / math.pi) * (x + 0.044715 * (x * x * x))))


def _softplus(z):
    return jnp.maximum(z, 0.0) + jnp.log1p(jnp.exp(-jnp.abs(z)))


def _mixer_in_kernel(x_ref, mod_ref, ln1_ref, w_in_ref, qg_ref, kg_ref, headsum_ref, scw_ref,
                     lcw_ref, lcb_ref, wa_ref, ba_ref, wx_ref, bx_ref, lam_ref, mng_ref,
                     qT_ref, k_ref, vT_ref, kmean_ref, ycl_ref,
                     cu_buf, lx_buf, h_carry, *, d_attn, d_conv, d_lru):
    s = pl.program_id(1)
    tm = x_ref.shape[1]
    n_blk = tm // MOBA_BLOCK

    @pl.when(s == 0)
    def _():
        cu_buf[0:HALO, :] = jnp.zeros((HALO, d_conv), F32)
        lx_buf[0:HALO, :] = jnp.zeros((HALO, d_lru), F32)
        h_carry[...] = jnp.zeros_like(h_carry)

    x = x_ref[0]
    m = mod_ref[0]
    shift1, scale1 = m[0:1], m[1:2]
    h = ((x * _rms(x) * ln1_ref[...]) * (1.0 + scale1) + shift1).astype(BF16)

    o_q, o_k, o_v = 0, d_attn, 2 * d_attn
    o_b = 3 * d_attn
    o_c, o_u = o_b + d_conv, o_b + 2 * d_conv
    o_lx = o_b + 3 * d_conv
    o_lg = o_lx + d_lru

    def head_norm(t, g_ref):
        ms = _dot((t * t).astype(BF16), headsum_ref[...])
        return t * lax.rsqrt(ms + EPS) * g_ref[...]

    q = head_norm(_dot(h, w_in_ref[:, o_q:o_q + d_attn]), qg_ref)
    qT = (q * (LOG2_E / math.sqrt(HEAD_DIM))).T.astype(BF16)
    for c in range(n_blk):
        qT_ref[0, c] = qT[:, c * MOBA_BLOCK:(c + 1) * MOBA_BLOCK]

    k = head_norm(_dot(h, w_in_ref[:, o_k:o_k + d_attn]), kg_ref)
    k_ref[0] = k.astype(BF16)
    for c in range(n_blk):
        kmean_ref[0, c] = jnp.mean(k[c * MOBA_BLOCK:(c + 1) * MOBA_BLOCK], axis=0, keepdims=True)

    vT = _dot(h, w_in_ref[:, o_v:o_v + d_attn]).T.astype(BF16)
    for c in range(n_blk):
        vT_ref[0, c] = vT[:, c * MOBA_BLOCK:(c + 1) * MOBA_BLOCK]

    sc_b = _dot(h, w_in_ref[:, o_b:o_b + d_conv])
    cu = _dot(h, w_in_ref[:, o_c:o_c + d_conv]) * _dot(h, w_in_ref[:, o_u:o_u + d_conv])
    cu_buf[HALO:HALO + tm, :] = cu
    scw = scw_ref[...]
    conv = (scw[0:1] * cu_buf[HALO - 2:HALO - 2 + tm, :]
            + scw[1:2] * cu_buf[HALO - 1:HALO - 1 + tm, :]
            + scw[2:3] * cu)
    cu_buf[0:HALO, :] = cu[tm - HALO:tm]
    y_conv = sc_b * conv

    lx = _dot(h, w_in_ref[:, o_lx:o_lx + d_lru])
    lx_buf[HALO:HALO + tm, :] = lx
    lcw = lcw_ref[...]
    xr = (lcw[0:1] * lx_buf[HALO - 3:HALO - 3 + tm, :]
          + lcw[1:2] * lx_buf[HALO - 2:HALO - 2 + tm, :]
          + lcw[2:3] * lx_buf[HALO - 1:HALO - 1 + tm, :]
          + lcw[3:4] * lx) + lcb_ref[...]
    lx_buf[0:HALO, :] = lx[tm - HALO:tm]
    xr_b = xr.astype(BF16)
    r = jax.nn.sigmoid(_dot(xr_b, wa_ref[...]) + ba_ref[...])
    i = jax.nn.sigmoid(_dot(xr_b, wx_ref[...]) + bx_ref[...])
    log_a = (-LRU_C) * r * _softplus(-lam_ref[...])
    a = jnp.exp(log_a)
    t = jnp.tanh(log_a)
    u = jnp.sqrt((-2.0 * t) / (1.0 - t)) * (i * xr)
    a_cum, h_loc = _scan_linear_recurrence(a, u)
    hs = h_loc + a_cum * h_carry[...]
    h_carry[...] = hs[tm - 1:tm]
    y_lru = hs * _gelu_tanh(_dot(h, w_in_ref[:, o_lg:o_lg + d_lru]))

    mng = mng_ref[...]
    ycl_ref[0, :, 0:d_conv] = (y_conv * _rms(y_conv) * mng[:, 0:d_conv]).astype(BF16)
    ycl_ref[0, :, d_conv:d_conv + d_lru] = (y_lru * _rms(y_lru) * mng[:, d_conv:]).astype(BF16)


def _mixer_in(x, mod_l, ln1_g, w_in_b, qg, kg, headsum, sc_w, lcw, lcb, wa_bd, ba, wx_bd, bx, lam, mng_cl,
              *, d_attn, d_conv, d_lru):
    batch, seq, d_model = x.shape
    tm = TOKEN_TILE
    n_blk = tm // MOBA_BLOCK
    nb = seq // MOBA_BLOCK

    def const(shape):
        return pl.BlockSpec(shape, lambda b, s: (0,) * len(shape))

    kern = functools.partial(_mixer_in_kernel, d_attn=d_attn, d_conv=d_conv, d_lru=d_lru)
    return pl.pallas_call(
        kern,
        grid=(batch, seq // tm),
        in_specs=[
            pl.BlockSpec((1, tm, d_model), lambda b, s: (b, s, 0)),
            pl.BlockSpec((1, N_MOD, d_model), lambda b, s: (b, 0, 0)),
            const((1, d_model)),
            const(w_in_b.shape),
            const((1, d_attn)), const((1, d_attn)), const((d_attn, d_attn)),
            const(sc_w.shape), const(lcw.shape), const((1, d_lru)),
            const((d_lru, d_lru)), const((1, d_lru)), const((d_lru, d_lru)), const((1, d_lru)),
            const((1, d_lru)), const((1, d_conv + d_lru)),
        ],
        out_specs=[
            pl.BlockSpec((1, n_blk, d_attn, MOBA_BLOCK), lambda b, s: (b, s, 0, 0)),
            pl.BlockSpec((1, tm, d_attn), lambda b, s: (b, s, 0)),
            pl.BlockSpec((1, n_blk, d_attn, MOBA_BLOCK), lambda b, s: (b, s, 0, 0)),
            pl.BlockSpec((1, n_blk, 1, d_attn), lambda b, s: (b, s, 0, 0)),
            pl.BlockSpec((1, tm, d_conv + d_lru), lambda b, s: (b, s, 0)),
        ],
        out_shape=[
            jax.ShapeDtypeStruct((batch, nb, d_attn, MOBA_BLOCK), BF16),
            jax.ShapeDtypeStruct((batch, seq, d_attn), BF16),
            jax.ShapeDtypeStruct((batch, nb, d_attn, MOBA_BLOCK), BF16),
            jax.ShapeDtypeStruct((batch, nb, 1, d_attn), F32),
            jax.ShapeDtypeStruct((batch, seq, d_conv + d_lru), BF16),
        ],
        scratch_shapes=[
            pltpu.VMEM((HALO + tm, d_conv), F32),
            pltpu.VMEM((HALO + tm, d_lru), F32),
            pltpu.VMEM((1, d_lru), F32),
        ],
        compiler_params=pltpu.CompilerParams(
            dimension_semantics=("arbitrary", "arbitrary"), vmem_limit_bytes=VMEM_LIMIT_BYTES),
        name="mixer_in",
    )(x, mod_l, ln1_g, w_in_b, qg, kg, headsum, sc_w, lcw, lcb, wa_bd, ba, wx_bd, bx, lam, mng_cl)


def _sublane_fold(x, op):
    tiles = [x[r:r + 8] for r in range(0, x.shape[0], 8)]
    while len(tiles) > 1:
        tiles = [op(a, b) for a, b in zip(tiles[0::2], tiles[1::2])] + tiles[len(tiles) & ~1:]
    return tiles[0]


def _moba_kernel(qT_ref, k_ref, vT_ref, kmean_ref, g_ref, o_ref,
                 qcat_ref, bias_ref, s_ref, mx_ref, m_ref, acc_ref):
    i = pl.program_id(1)
    blk = MOBA_BLOCK
    d_attn = qT_ref.shape[2]
    n_heads = d_attn // HEAD_DIM
    n_pairs = n_heads // 2
    nb = kmean_ref.shape[1]
    pair = 2 * HEAD_DIM
    wide = 2 * blk

    half = lax.broadcasted_iota(jnp.int32, (pair, blk), 0) < HEAD_DIM
    blk_id = lax.broadcasted_iota(jnp.int32, (nb, blk), 0)
    past = blk_id < i
    causal = (lax.broadcasted_iota(jnp.int32, (blk, wide), 0)
              <= (lax.broadcasted_iota(jnp.int32, (blk, wide), 1) & (blk - 1)))
    kmean = kmean_ref[0, :, 0, :]
    own = pl.ds(pl.multiple_of(i * blk, blk), blk)

    for p in range(n_pairs):
        p0 = p * pair
        c0 = p * wide
        q_pair = qT_ref[0, 0, p0:p0 + pair, :]
        zero = jnp.zeros_like(q_pair)
        for hh in range(2):
            hd = 2 * p + hh
            q_m = jnp.where(half if hh == 0 else jnp.logical_not(half), q_pair, zero)
            qcat_ref[p, :, hh * blk:(hh + 1) * blk] = q_m
            gate = _dot(kmean[:, p0:p0 + pair].astype(BF16), q_m)
            bias_ref[0:nb, hd * blk:(hd + 1) * blk] = jnp.where(past, gate, -jnp.inf)

        s_t = jnp.where(causal, _dot(k_ref[0, own, p0:p0 + pair], qcat_ref[p]), -jnp.inf)
        s_ref[0, :, c0:c0 + wide] = s_t
        mx8 = _sublane_fold(s_t, jnp.maximum)
        mx_ref[:, c0:c0 + wide] = mx8
        m_own = jnp.max(mx8, axis=0, keepdims=True)
        m_ref[0, :, c0:c0 + wide] = m_own
        m_ref[1, :, c0:c0 + wide] = m_own

    cols = n_heads * blk
    gate = bias_ref[0:nb, :]
    blk_f = lax.broadcasted_iota(jnp.int32, (nb, cols), 0).astype(F32)
    keep = jnp.zeros((nb, cols), F32)
    for _ in range(MOBA_TOPK):
        top = jnp.max(gate, axis=0, keepdims=True)
        first = jnp.min(jnp.where(gate == top, blk_f, float(nb)), axis=0, keepdims=True)
        hit = blk_f == first
        keep = jnp.where(hit, 1.0, keep)
        gate = jnp.where(hit, -jnp.inf, gate)
    keep = jnp.where(blk_f < i.astype(F32), keep, 0.0)
    bias_ref[0:nb, :] = jnp.where(keep > 0.0, 0.0, -jnp.inf)
    bias_ref[nb:nb + 8, :] = jnp.zeros((8, cols), F32)
    acc_ref[...] = jnp.zeros_like(acc_ref)
    ones_rows = jnp.where(lax.broadcasted_iota(jnp.int32, (ACC_ROWS - HEAD_DIM, blk), 0) == 0,
                          1.0, 0.0).astype(BF16)

    def value_step(slot, m_prev, key_blk, bias_row):
        m_cur = m_ref[slot]
        alpha = jnp.exp2(m_prev - m_cur)
        m_eff = m_cur - bias_ref[pl.ds(bias_row, 1), :]
        for hd in range(n_heads):
            r0 = hd * HEAD_DIM
            c0 = hd * blk
            p_t = jnp.exp2(s_ref[slot, :, c0:c0 + blk] - m_eff[:, c0:c0 + blk])
            v_t = jnp.concatenate([vT_ref[0, key_blk, r0:r0 + HEAD_DIM, :], ones_rows], axis=0)
            a0 = hd * ACC_ROWS
            acc_ref[a0:a0 + ACC_ROWS, :] = (alpha[:, c0:c0 + blk] * acc_ref[a0:a0 + ACC_ROWS, :]
                                            + _dot(v_t, p_t.astype(BF16)))

    def fused_step(t, cur):
        m_prev = m_ref[cur]
        rows = pl.ds(pl.multiple_of((t - 1) * blk, blk), blk)
        for p in range(n_pairs):
            c0 = p * wide
            s_t = _dot(k_ref[0, rows, p * pair:(p + 1) * pair], qcat_ref[p])
            s_ref[cur, :, c0:c0 + wide] = s_t
            mx8 = jnp.maximum(mx_ref[:, c0:c0 + wide],
                              _sublane_fold(s_t, jnp.maximum) + bias_ref[pl.ds(t - 1, 1), c0:c0 + wide])
            mx_ref[:, c0:c0 + wide] = mx8
            m_ref[cur, :, c0:c0 + wide] = jnp.max(mx8, axis=0, keepdims=True)
        value_step(1 - cur, m_prev, jnp.where(t == 1, i, t - 2), jnp.where(t == 1, nb, t - 2))

    def loop_body(t, carry):
        for parity in range(2):
            pl.when((t & 1) == parity)(functools.partial(fused_step, t, parity))
        return carry

    lax.fori_loop(1, i + 1, loop_body, 0)
    for parity in range(2):
        @pl.when((i & 1) == parity)
        def _(parity=parity):
            value_step(parity, m_ref[1 - parity], jnp.where(i == 0, i, i - 1), jnp.where(i == 0, nb, i - 1))

    y_t = jnp.concatenate(
        [acc_ref[hd * ACC_ROWS:hd * ACC_ROWS + HEAD_DIM, :]
         / acc_ref[hd * ACC_ROWS + HEAD_DIM:hd * ACC_ROWS + HEAD_DIM + 1, :]
         for hd in range(n_heads)], axis=0)
    y_t = y_t * _rms(y_t, axis=0)
    o_ref[0] = (y_t.T * g_ref[...]).astype(BF16)


def _moba(qT, k, vT, kmean, mng_attn):
    batch, nb, d_attn, blk = qT.shape
    seq = k.shape[1]
    n_heads = d_attn // HEAD_DIM
    cols = n_heads * blk

    def resident(shape, index_map):
        return pl.BlockSpec(shape, index_map, pipeline_mode=pl.Buffered(1))

    return pl.pallas_call(
        _moba_kernel,
        grid=(batch, nb),
        in_specs=[
            pl.BlockSpec((1, 1, d_attn, blk), lambda b, i: (b, i, 0, 0)),
            resident((1, seq, d_attn), lambda b, i: (b, 0, 0)),
            resident((1, nb, d_attn, blk), lambda b, i: (b, 0, 0, 0)),
            pl.BlockSpec((1, nb, 1, d_attn), lambda b, i: (b, 0, 0, 0)),
            pl.BlockSpec((1, d_attn), lambda b, i: (0, 0)),
        ],
        out_specs=pl.BlockSpec((1, blk, d_attn), lambda b, i: (b, i, 0)),
        out_shape=jax.ShapeDtypeStruct((batch, seq, d_attn), BF16),
        scratch_shapes=[
            pltpu.VMEM((n_heads // 2, 2 * HEAD_DIM, 2 * blk), BF16),
            pltpu.VMEM((nb + 8, cols), F32),
            pltpu.VMEM((2, blk, cols), F32),
            pltpu.VMEM((8, cols), F32),
            pltpu.VMEM((2, 1, cols), F32),
            pltpu.VMEM((n_heads * ACC_ROWS, blk), F32),
        ],
        compiler_params=pltpu.CompilerParams(
            dimension_semantics=("arbitrary", "arbitrary"), vmem_limit_bytes=VMEM_LIMIT_BYTES),
        name="moba_attention",
    )(qT, k, vT, kmean, mng_attn)


def _out_mlp_kernel(x_ref, ya_ref, ycl_ref, mod_ref, ln2_ref, wout_ref, wup_ref, wdown_ref, o_ref):
    d_attn = ya_ref.shape[2]
    d_ff = wup_ref.shape[1]
    x = x_ref[0]
    m = mod_ref[0]
    gate1, shift2, scale2, gate2 = m[2:3], m[3:4], m[4:5], m[5:6]
    mix = _dot(ya_ref[0], wout_ref[0:d_attn, :]) + _dot(ycl_ref[0], wout_ref[d_attn:, :])
    x1 = x + gate1 * mix
    h2 = ((x1 * _rms(x1) * ln2_ref[...]) * (1.0 + scale2) + shift2).astype(BF16)
    ff = jnp.zeros_like(x1)
    for c0 in range(0, d_ff, FF_CHUNK):
        up = _dot(h2, wup_ref[:, c0:c0 + FF_CHUNK])
        act = jnp.square(jnp.maximum(up, 0.0)).astype(BF16)
        ff = ff + _dot(act, wdown_ref[c0:c0 + FF_CHUNK, :])
    o_ref[0] = x1 + gate2 * ff


def _out_mlp(x, ya, ycl, mod_l, ln2_g, w_out_b, w_up_b, w_down_b):
    batch, seq, d_model = x.shape
    tm = TOKEN_TILE
    d_attn = ya.shape[2]
    d_cl = ycl.shape[2]

    def const(shape):
        return pl.BlockSpec(shape, lambda b, s: (0,) * len(shape), pipeline_mode=pl.Buffered(1))

    return pl.pallas_call(
        _out_mlp_kernel,
        grid=(batch, seq // tm),
        in_specs=[
            pl.BlockSpec((1, tm, d_model), lambda b, s: (b, s, 0)),
            pl.BlockSpec((1, tm, d_attn), lambda b, s: (b, s, 0)),
            pl.BlockSpec((1, tm, d_cl), lambda b, s: (b, s, 0)),
            pl.BlockSpec((1, N_MOD, d_model), lambda b, s: (b, 0, 0)),
            pl.BlockSpec((1, d_model), lambda b, s: (0, 0)),
            const(w_out_b.shape), const(w_up_b.shape), const(w_down_b.shape),
        ],
        out_specs=pl.BlockSpec((1, tm, d_model), lambda b, s: (b, s, 0)),
        out_shape=jax.ShapeDtypeStruct((batch, seq, d_model), F32),
        compiler_params=pltpu.CompilerParams(
            dimension_semantics=("arbitrary", "arbitrary"), vmem_limit_bytes=VMEM_LIMIT_BYTES),
        name="out_mlp",
    )(x, ya, ycl, mod_l, ln2_g, w_out_b, w_up_b, w_down_b)


def _block_diag(w):
    n, r, c = w.shape
    eye = jnp.eye(n, dtype=w.dtype)
    return (eye[:, None, :, None] * w[:, :, None, :]).reshape(n * r, n * c)


def kernel(x, c, ln1_g, ln2_g, w_ada, b_ada, w_in, q_norm_g, k_norm_g, sc_w, lru_conv_w, lru_conv_b,
           lru_wa, lru_ba, lru_wx, lru_bx, lru_lambda, mix_norm_g, w_out, w_up, w_down):
    batch, seq, d_model = x.shape
    depth = w_in.shape[0]
    d_conv = sc_w.shape[2]
    d_lru = lru_conv_w.shape[2]
    d_attn = mix_norm_g.shape[1] - d_conv - d_lru
    n_heads = d_attn // HEAD_DIM
    assert seq % TOKEN_TILE == 0 and TOKEN_TILE % MOBA_BLOCK == 0
    assert w_in.shape[2] == 3 * d_attn + 3 * d_conv + 2 * d_lru

    mod = _modulation(c, w_ada, b_ada).reshape(depth, batch, N_MOD, d_model)
    headsum = _block_diag(jnp.full((n_heads, HEAD_DIM, HEAD_DIM), 1.0 / HEAD_DIM, F32)).astype(BF16)

    for l in range(depth):
        row = lambda v: v.reshape(1, -1)
        qT, k, vT, kmean, ycl = _mixer_in(
            x, mod[l], row(ln1_g[l]), w_in[l].astype(BF16),
            row(jnp.tile(q_norm_g[l], n_heads)), row(jnp.tile(k_norm_g[l], n_heads)), headsum,
            sc_w[l], lru_conv_w[l], row(lru_conv_b[l]),
            _block_diag(lru_wa[l]).astype(BF16), row(lru_ba[l]),
            _block_diag(lru_wx[l]).astype(BF16), row(lru_bx[l]),
            row(lru_lambda[l]), row(mix_norm_g[l, d_attn:]),
            d_attn=d_attn, d_conv=d_conv, d_lru=d_lru)
        ya = _moba(qT, k, vT, kmean, row(mix_norm_g[l, :d_attn]))
        x = _out_mlp(x, ya, ycl, mod[l], row(ln2_g[l]),
                     w_out[l].astype(BF16), w_up[l].astype(BF16), w_down[l].astype(BF16))
    return x
```

```python
import functools
import math

import jax
import jax.numpy as jnp
from jax import lax
from jax.experimental import pallas as pl
from jax.experimental.pallas import tpu as pltpu

F32 = jnp.float32
BF16 = jnp.bfloat16

HEAD_DIM = 64
MOBA_BLOCK = 256
MOBA_TOPK = 3
LRU_C = 8.0
N_MOD = 6
EPS = 1e-6

TOKEN_TILE = 512
MOD_COL_TILE = 1536
FF_CHUNK = 1024
HALO = 8
ACC_ROWS = HEAD_DIM + 16
LOG2_E = math.log2(math.e)
VMEM_LIMIT_BYTES = 56 * 1024 * 1024


def _rms(x, axis=-1):
    return lax.rsqrt(jnp.mean(x * x, axis=axis, keepdims=True) + EPS)


def _dot(a, b):
    return jnp.dot(a, b, preferred_element_type=F32)


def _mod_kernel(c_ref, w_ref, b_ref, o_ref):
    c = c_ref[...]
    c_act = (c * jax.nn.sigmoid(c)).astype(BF16)
    o_ref[0] = _dot(c_act, w_ref[0].astype(BF16)) + b_ref[0]


def _modulation(c, w_ada, b_ada):
    depth, d_model, n_out = w_ada.shape
    batch = c.shape[0]
    return pl.pallas_call(
        _mod_kernel,
        grid=(depth, n_out // MOD_COL_TILE),
        in_specs=[
            pl.BlockSpec((batch, d_model), lambda l, j: (0, 0)),
            pl.BlockSpec((1, d_model, MOD_COL_TILE), lambda l, j: (l, 0, j)),
            pl.BlockSpec((1, 1, MOD_COL_TILE), lambda l, j: (l, 0, j)),
        ],
        out_specs=pl.BlockSpec((1, batch, MOD_COL_TILE), lambda l, j: (l, 0, j)),
        out_shape=jax.ShapeDtypeStruct((depth, batch, n_out), F32),
        compiler_params=pltpu.CompilerParams(
            dimension_semantics=("arbitrary", "arbitrary"), vmem_limit_bytes=VMEM_LIMIT_BYTES),
        name="adaln_modulation",
    )(c, w_ada, b_ada.reshape(depth, 1, n_out))


def _scan_linear_recurrence(a, u):
    n = a.shape[0]
    row = lax.broadcasted_iota(jnp.int32, a.shape, 0)
    d = 1
    while d < n:
        keep = row >= d
        a_prev = jnp.where(keep, pltpu.roll(a, d, 0), 1.0)
        u_prev = jnp.where(keep, pltpu.roll(u, d, 0), 0.0)
        u = a * u_prev + u
        a = a * a_prev
        d *= 2
    return a, u


def _gelu_tanh(x):
    return 0.5 * x * (1.0 + jnp.tanh(math.sqrt(2.0 / math.pi) * (x + 0.044715 * (x * x * x))))


def _softplus(z):
    return jnp.maximum(z, 0.0) + jnp.log1p(jnp.exp(-jnp.abs(z)))


def _mixer_in_kernel(x_ref, mod_ref, ln1_ref, w_in_ref, qg_ref, kg_ref, headsum_ref, scw_ref,
                     lcw_ref, lcb_ref, wa_ref, ba_ref, wx_ref, bx_ref, lam_ref, mng_ref,
                     qT_ref, k_ref, vT_ref, kmean_ref, ycl_ref,
                     cu_buf, lx_buf, h_carry, *, d_attn, d_conv, d_lru):
    s = pl.program_id(1)
    tm = x_ref.shape[1]
    n_blk = tm // MOBA_BLOCK

    @pl.when(s == 0)
    def _():
        cu_buf[0:HALO, :] = jnp.zeros((HALO, d_conv), F32)
        lx_buf[0:HALO, :] = jnp.zeros((HALO, d_lru), F32)
        h_carry[...] = jnp.zeros_like(h_carry)

    x = x_ref[0]
    m = mod_ref[0]
    shift1, scale1 = m[0:1], m[1:2]
    h = ((x * _rms(x) * ln1_ref[...]) * (1.0 + scale1) + shift1).astype(BF16)

    o_q, o_k, o_v = 0, d_attn, 2 * d_attn
    o_b = 3 * d_attn
    o_c, o_u = o_b + d_conv, o_b + 2 * d_conv
    o_lx = o_b + 3 * d_conv
    o_lg = o_lx + d_lru

    def head_norm(t, g_ref):
        ms = _dot((t * t).astype(BF16), headsum_ref[...])
        return t * lax.rsqrt(ms + EPS) * g_ref[...]

    q = head_norm(_dot(h, w_in_ref[:, o_q:o_q + d_attn]), qg_ref)
    qT = (q * (LOG2_E / math.sqrt(HEAD_DIM))).T.astype(BF16)
    for c in range(n_blk):
        qT_ref[0, c] = qT[:, c * MOBA_BLOCK:(c + 1) * MOBA_BLOCK]

    k = head_norm(_dot(h, w_in_ref[:, o_k:o_k + d_attn]), kg_ref)
    k_ref[0] = k.astype(BF16)
    for c in range(n_blk):
        kmean_ref[0, c] = jnp.mean(k[c * MOBA_BLOCK:(c + 1) * MOBA_BLOCK], axis=0, keepdims=True)

    vT = _dot(h, w_in_ref[:, o_v:o_v + d_attn]).T.astype(BF16)
    for c in range(n_blk):
        vT_ref[0, c] = vT[:, c * MOBA_BLOCK:(c + 1) * MOBA_BLOCK]

    sc_b = _dot(h, w_in_ref[:, o_b:o_b + d_conv])
    cu = _dot(h, w_in_ref[:, o_c:o_c + d_conv]) * _dot(h, w_in_ref[:, o_u:o_u + d_conv])
    cu_buf[HALO:HALO + tm, :] = cu
    scw = scw_ref[...]
    conv = (scw[0:1] * cu_buf[HALO - 2:HALO - 2 + tm, :]
            + scw[1:2] * cu_buf[HALO - 1:HALO - 1 + tm, :]
            + scw[2:3] * cu)
    cu_buf[0:HALO, :] = cu[tm - HALO:tm]
    y_conv = sc_b * conv

    lx = _dot(h, w_in_ref[:, o_lx:o_lx + d_lru])
    lx_buf[HALO:HALO + tm, :] = lx
    lcw = lcw_ref[...]
    xr = (lcw[0:1] * lx_buf[HALO - 3:HALO - 3 + tm, :]
          + lcw[1:2] * lx_buf[HALO - 2:HALO - 2 + tm, :]
          + lcw[2:3] * lx_buf[HALO - 1:HALO - 1 + tm, :]
          + lcw[3:4] * lx) + lcb_ref[...]
    lx_buf[0:HALO, :] = lx[tm - HALO:tm]
    xr_b = xr.astype(BF16)
    r = jax.nn.sigmoid(_dot(xr_b, wa_ref[...]) + ba_ref[...])
    i = jax.nn.sigmoid(_dot(xr_b, wx_ref[...]) + bx_ref[...])
    log_a = (-LRU_C) * r * _softplus(-lam_ref[...])
    a = jnp.exp(log_a)
    t = jnp.tanh(log_a)
    u = jnp.sqrt((-2.0 * t) / (1.0 - t)) * (i * xr)
    a_cum, h_loc = _scan_linear_recurrence(a, u)
    hs = h_loc + a_cum * h_carry[...]
    h_carry[...] = hs[tm - 1:tm]
    y_lru = hs * _gelu_tanh(_dot(h, w_in_ref[:, o_lg:o_lg + d_lru]))

    mng = mng_ref[...]
    ycl_ref[0, :, 0:d_conv] = (y_conv * _rms(y_conv) * mng[:, 0:d_conv]).astype(BF16)
    ycl_ref[0, :, d_conv:d_conv + d_lru] = (y_lru * _rms(y_lru) * mng[:, d_conv:]).astype(BF16)


def _mixer_in(x, mod_l, ln1_g, w_in_b, qg, kg, headsum, sc_w, lcw, lcb, wa_bd, ba, wx_bd, bx, lam, mng_cl,
              *, d_attn, d_conv, d_lru):
    batch, seq, d_model = x.shape
    tm = TOKEN_TILE
    n_blk = tm // MOBA_BLOCK
    nb = seq // MOBA_BLOCK

    def const(shape):
        return pl.BlockSpec(shape, lambda b, s: (0,) * len(shape))

    kern = functools.partial(_mixer_in_kernel, d_attn=d_attn, d_conv=d_conv, d_lru=d_lru)
    return pl.pallas_call(
        kern,
        grid=(batch, seq // tm),
        in_specs=[
            pl.BlockSpec((1, tm, d_model), lambda b, s: (b, s, 0)),
            pl.BlockSpec((1, N_MOD, d_model), lambda b, s: (b, 0, 0)),
            const((1, d_model)),
            const(w_in_b.shape),
            const((1, d_attn)), const((1, d_attn)), const((d_attn, d_attn)),
            const(sc_w.shape), const(lcw.shape), const((1, d_lru)),
            const((d_lru, d_lru)), const((1, d_lru)), const((d_lru, d_lru)), const((1, d_lru)),
            const((1, d_lru)), const((1, d_conv + d_lru)),
        ],
        out_specs=[
            pl.BlockSpec((1, n_blk, d_attn, MOBA_BLOCK), lambda b, s: (b, s, 0, 0)),
            pl.BlockSpec((1, tm, d_attn), lambda b, s: (b, s, 0)),
            pl.BlockSpec((1, n_blk, d_attn, MOBA_BLOCK), lambda b, s: (b, s, 0, 0)),
            pl.BlockSpec((1, n_blk, 1, d_attn), lambda b, s: (b, s, 0, 0)),
            pl.BlockSpec((1, tm, d_conv + d_lru), lambda b, s: (b, s, 0)),
        ],
        out_shape=[
            jax.ShapeDtypeStruct((batch, nb, d_attn, MOBA_BLOCK), BF16),
            jax.ShapeDtypeStruct((batch, seq, d_attn), BF16),
            jax.ShapeDtypeStruct((batch, nb, d_attn, MOBA_BLOCK), BF16),
            jax.ShapeDtypeStruct((batch, nb, 1, d_attn), F32),
            jax.ShapeDtypeStruct((batch, seq, d_conv + d_lru), BF16),
        ],
        scratch_shapes=[
            pltpu.VMEM((HALO + tm, d_conv), F32),
            pltpu.VMEM((HALO + tm, d_lru), F32),
            pltpu.VMEM((1, d_lru), F32),
        ],
        compiler_params=pltpu.CompilerParams(
            dimension_semantics=("arbitrary", "arbitrary"), vmem_limit_bytes=VMEM_LIMIT_BYTES),
        name="mixer_in",
    )(x, mod_l, ln1_g, w_in_b, qg, kg, headsum, sc_w, lcw, lcb, wa_bd, ba, wx_bd, bx, lam, mng_cl)


def _sublane_fold(x, op):
    tiles = [x[r:r + 8] for r in range(0, x.shape[0], 8)]
    while len(tiles) > 1:
        tiles = [op(a, b) for a, b in zip(tiles[0::2], tiles[1::2])] + tiles[len(tiles) & ~1:]
    return tiles[0]


def _moba_kernel(qT_ref, k_ref, vT_ref, kmean_ref, g_ref, o_ref,
                 qcat_ref, bias_ref, s_ref, mx_ref, m_ref, acc_ref):
    i = pl.program_id(1)
    blk = MOBA_BLOCK
    d_attn = qT_ref.shape[2]
    n_heads = d_attn // HEAD_DIM
    n_pairs = n_heads // 2
    nb = kmean_ref.shape[1]
    pair = 2 * HEAD_DIM
    wide = 2 * blk

    half = lax.broadcasted_iota(jnp.int32, (pair, blk), 0) < HEAD_DIM
    blk_id = lax.broadcasted_iota(jnp.int32, (nb, blk), 0)
    past = blk_id < i
    causal = (lax.broadcasted_iota(jnp.int32, (blk, wide), 0)
              <= (lax.broadcasted_iota(jnp.int32, (blk, wide), 1) & (blk - 1)))
    kmean = kmean_ref[0, :, 0, :]
    own = pl.ds(pl.multiple_of(i * blk, blk), blk)

    for p in range(n_pairs):
        p0 = p * pair
        c0 = p * wide
        q_pair = qT_ref[0, 0, p0:p0 + pair, :]
        zero = jnp.zeros_like(q_pair)
        for hh in range(2):
            hd = 2 * p + hh
            q_m = jnp.where(half if hh == 0 else jnp.logical_not(half), q_pair, zero)
            qcat_ref[p, :, hh * blk:(hh + 1) * blk] = q_m
            gate = _dot(kmean[:, p0:p0 + pair].astype(BF16), q_m)
            bias_ref[0:nb, hd * blk:(hd + 1) * blk] = jnp.where(past, gate, -jnp.inf)

        s_t = jnp.where(causal, _dot(k_ref[0, own, p0:p0 + pair], qcat_ref[p]), -jnp.inf)
        s_ref[0, :, c0:c0 + wide] = s_t
        mx8 = _sublane_fold(s_t, jnp.maximum)
        mx_ref[:, c0:c0 + wide] = mx8
        m_own = jnp.max(mx8, axis=0, keepdims=True)
        m_ref[0, :, c0:c0 + wide] = m_own
        m_ref[1, :, c0:c0 + wide] = m_own

    cols = n_heads * blk
    gate = bias_ref[0:nb, :]
    blk_f = lax.broadcasted_iota(jnp.int32, (nb, cols), 0).astype(F32)
    keep = jnp.zeros((nb, cols), F32)
    for _ in range(MOBA_TOPK):
        top = jnp.max(gate, axis=0, keepdims=True)
        first = jnp.min(jnp.where(gate == top, blk_f, float(nb)), axis=0, keepdims=True)
        hit = blk_f == first
        keep = jnp.where(hit, 1.0, keep)
        gate = jnp.where(hit, -jnp.inf, gate)
    keep = jnp.where(blk_f < i.astype(F32), keep, 0.0)
    bias_ref[0:nb, :] = jnp.where(keep > 0.0, 0.0, -jnp.inf)
    bias_ref[nb:nb + 8, :] = jnp.zeros((8, cols), F32)
    acc_ref[...] = jnp.zeros_like(acc_ref)
    ones_rows = jnp.where(lax.broadcasted_iota(jnp.int32, (ACC_ROWS - HEAD_DIM, blk), 0) == 0,
                          1.0, 0.0).astype(BF16)

    def value_parts(slot, m_prev, key_blk, bias_row):
        m_cur = m_ref[slot]
        alpha = jnp.exp2(m_prev - m_cur)
        m_eff = m_cur - bias_ref[pl.ds(bias_row, 1), :]

        def head(hd):
            r0 = hd * HEAD_DIM
            c0 = hd * blk
            p_t = jnp.exp2(s_ref[slot, :, c0:c0 + blk] - m_eff[:, c0:c0 + blk])
            v_t = jnp.concatenate([vT_ref[0, key_blk, r0:r0 + HEAD_DIM, :], ones_rows], axis=0)
            a0 = hd * ACC_ROWS
            acc_ref[a0:a0 + ACC_ROWS, :] = (alpha[:, c0:c0 + blk] * acc_ref[a0:a0 + ACC_ROWS, :]
                                            + _dot(v_t, p_t.astype(BF16)))

        return [functools.partial(head, hd) for hd in range(n_heads)]

    def fused_step(t, cur):
        m_prev = m_ref[cur]
        rows = pl.ds(pl.multiple_of((t - 1) * blk, blk), blk)
        values = value_parts(1 - cur, m_prev, jnp.where(t == 1, i, t - 2), jnp.where(t == 1, nb, t - 2))
        for p in range(n_pairs):
            c0 = p * wide
            s_t = _dot(k_ref[0, rows, p * pair:(p + 1) * pair], qcat_ref[p])
            s_ref[cur, :, c0:c0 + wide] = s_t
            mx8 = jnp.maximum(mx_ref[:, c0:c0 + wide],
                              _sublane_fold(s_t, jnp.maximum) + bias_ref[pl.ds(t - 1, 1), c0:c0 + wide])
            mx_ref[:, c0:c0 + wide] = mx8
            m_ref[cur, :, c0:c0 + wide] = jnp.max(mx8, axis=0, keepdims=True)
            values[2 * p]()
            values[2 * p + 1]()

    def loop_body(t, carry):
        for parity in range(2):
            pl.when((t & 1) == parity)(functools.partial(fused_step, t, parity))
        return carry

    lax.fori_loop(1, i + 1, loop_body, 0)
    for parity in range(2):
        @pl.when((i & 1) == parity)
        def _(parity=parity):
            for value in value_parts(parity, m_ref[1 - parity],
                                     jnp.where(i == 0, i, i - 1), jnp.where(i == 0, nb, i - 1)):
                value()

    y_t = jnp.concatenate(
        [acc_ref[hd * ACC_ROWS:hd * ACC_ROWS + HEAD_DIM, :]
         / acc_ref[hd * ACC_ROWS + HEAD_DIM:hd * ACC_ROWS + HEAD_DIM + 1, :]
         for hd in range(n_heads)], axis=0)
    y_t = y_t * _rms(y_t, axis=0)
    o_ref[0] = (y_t.T * g_ref[...]).astype(BF16)


def _moba(qT, k, vT, kmean, mng_attn):
    batch, nb, d_attn, blk = qT.shape
    seq = k.shape[1]
    n_heads = d_attn // HEAD_DIM
    cols = n_heads * blk

    def resident(shape, index_map):
        return pl.BlockSpec(shape, index_map, pipeline_mode=pl.Buffered(1))

    return pl.pallas_call(
        _moba_kernel,
        grid=(batch, nb),
        in_specs=[
            pl.BlockSpec((1, 1, d_attn, blk), lambda b, i: (b, i, 0, 0)),
            resident((1, seq, d_attn), lambda b, i: (b, 0, 0)),
            resident((1, nb, d_attn, blk), lambda b, i: (b, 0, 0, 0)),
            pl.BlockSpec((1, nb, 1, d_attn), lambda b, i: (b, 0, 0, 0)),
            pl.BlockSpec((1, d_attn), lambda b, i: (0, 0)),
        ],
        out_specs=pl.BlockSpec((1, blk, d_attn), lambda b, i: (b, i, 0)),
        out_shape=jax.ShapeDtypeStruct((batch, seq, d_attn), BF16),
        scratch_shapes=[
            pltpu.VMEM((n_heads // 2, 2 * HEAD_DIM, 2 * blk), BF16),
            pltpu.VMEM((nb + 8, cols), F32),
            pltpu.VMEM((2, blk, cols), F32),
            pltpu.VMEM((8, cols), F32),
            pltpu.VMEM((2, 1, cols), F32),
            pltpu.VMEM((n_heads * ACC_ROWS, blk), F32),
        ],
        compiler_params=pltpu.CompilerParams(
            dimension_semantics=("arbitrary", "arbitrary"), vmem_limit_bytes=VMEM_LIMIT_BYTES),
        name="moba_attention",
    )(qT, k, vT, kmean, mng_attn)


def _out_mlp_kernel(x_ref, ya_ref, ycl_ref, mod_ref, ln2_ref, wout_ref, wup_ref, wdown_ref, o_ref):
    d_attn = ya_ref.shape[2]
    d_ff = wup_ref.shape[1]
    x = x_ref[0]
    m = mod_ref[0]
    gate1, shift2, scale2, gate2 = m[2:3], m[3:4], m[4:5], m[5:6]
    mix = _dot(ya_ref[0], wout_ref[0:d_attn, :]) + _dot(ycl_ref[0], wout_ref[d_attn:, :])
    x1 = x + gate1 * mix
    h2 = ((x1 * _rms(x1) * ln2_ref[...]) * (1.0 + scale2) + shift2).astype(BF16)
    ff = jnp.zeros_like(x1)
    for c0 in range(0, d_ff, FF_CHUNK):
        up = _dot(h2, wup_ref[:, c0:c0 + FF_CHUNK])
        act = jnp.square(jnp.maximum(up, 0.0)).astype(BF16)
        ff = ff + _dot(act, wdown_ref[c0:c0 + FF_CHUNK, :])
    o_ref[0] = x1 + gate2 * ff


def _out_mlp(x, ya, ycl, mod_l, ln2_g, w_out_b, w_up_b, w_down_b):
    batch, seq, d_model = x.shape
    tm = TOKEN_TILE
    d_attn = ya.shape[2]
    d_cl = ycl.shape[2]

    def const(shape):
        return pl.BlockSpec(shape, lambda b, s: (0,) * len(shape), pipeline_mode=pl.Buffered(1))

    return pl.pallas_call(
        _out_mlp_kernel,
        grid=(batch, seq // tm),
        in_specs=[
            pl.BlockSpec((1, tm, d_model), lambda b, s: (b, s, 0)),
            pl.BlockSpec((1, tm, d_attn), lambda b, s: (b, s, 0)),
            pl.BlockSpec((1, tm, d_cl), lambda b, s: (b, s, 0)),
            pl.BlockSpec((1, N_MOD, d_model), lambda b, s: (b, 0, 0)),
            pl.BlockSpec((1, d_model), lambda b, s: (0, 0)),
            const(w_out_b.shape), const(w_up_b.shape), const(w_down_b.shape),
        ],
        out_specs=pl.BlockSpec((1, tm, d_model), lambda b, s: (b, s, 0)),
        out_shape=jax.ShapeDtypeStruct((batch, seq, d_model), F32),
        compiler_params=pltpu.CompilerParams(
            dimension_semantics=("arbitrary", "arbitrary"), vmem_limit_bytes=VMEM_LIMIT_BYTES),
        name="out_mlp",
    )(x, ya, ycl, mod_l, ln2_g, w_out_b, w_up_b, w_down_b)


def _block_diag(w):
    n, r, c = w.shape
    eye = jnp.eye(n, dtype=w.dtype)
    return (eye[:, None, :, None] * w[:, :, None, :]).reshape(n * r, n * c)


def kernel(x, c, ln1_g, ln2_g, w_ada, b_ada, w_in, q_norm_g, k_norm_g, sc_w, lru_conv_w, lru_conv_b,
           lru_wa, lru_ba, lru_wx, lru_bx, lru_lambda, mix_norm_g, w_out, w_up, w_down):
    batch, seq, d_model = x.shape
    depth = w_in.shape[0]
    d_conv = sc_w.shape[2]
    d_lru = lru_conv_w.shape[2]
    d_attn = mix_norm_g.shape[1] - d_conv - d_lru
    n_heads = d_attn // HEAD_DIM
    assert seq % TOKEN_TILE == 0 and TOKEN_TILE % MOBA_BLOCK == 0
    assert w_in.shape[2] == 3 * d_attn + 3 * d_conv + 2 * d_lru

    mod = _modulation(c, w_ada, b_ada).reshape(depth, batch, N_MOD, d_model)
    headsum = _block_diag(jnp.full((n_heads, HEAD_DIM, HEAD_DIM), 1.0 / HEAD_DIM, F32)).astype(BF16)

    for l in range(depth):
        row = lambda v: v.reshape(1, -1)
        qT, k, vT, kmean, ycl = _mixer_in(
            x, mod[l], row(ln1_g[l]), w_in[l].astype(BF16),
            row(jnp.tile(q_norm_g[l], n_heads)), row(jnp.tile(k_norm_g[l], n_heads)), headsum,
            sc_w[l], lru_conv_w[l], row(lru_conv_b[l]),
            _block_diag(lru_wa[l]).astype(BF16), row(lru_ba[l]),
            _block_diag(lru_wx[l]).astype(BF16), row(lru_bx[l]),
            row(lru_lambda[l]), row(mix_norm_g[l, d_attn:]),
            d_attn=d_attn, d_conv=d_conv, d_lru=d_lru)
        ya = _moba(qT, k, vT, kmean, row(mix_norm_g[l, :d_attn]))
        x = _out_mlp(x, ya, ycl, mod[l], row(ln2_g[l]),
                     w_out[l].astype(BF16), w_up[l].astype(BF16), w_down[l].astype(BF16))
    return x
```

```python
import functools
import math

import jax
import jax.numpy as jnp
from jax import lax
from jax.experimental import pallas as pl
from jax.experimental.pallas import tpu as pltpu

F32 = jnp.float32
BF16 = jnp.bfloat16

HEAD_DIM = 64
MOBA_BLOCK = 256
MOBA_TOPK = 3
LRU_C = 8.0
N_MOD = 6
EPS = 1e-6

TOKEN_TILE = 512
MOD_COL_TILE = 1536
FF_CHUNK = 1024
HALO = 8
ACC_ROWS = HEAD_DIM + 16
LOG2_E = math.log2(math.e)
VMEM_LIMIT_BYTES = 56 * 1024 * 1024


def _rms(x, axis=-1):
    return lax.rsqrt(jnp.mean(x * x, axis=axis, keepdims=True) + EPS)


def _dot(a, b):
    return jnp.dot(a, b, preferred_element_type=F32)


def _mod_kernel(c_ref, w_ref, b_ref, o_ref):
    c = c_ref[...]
    c_act = (c * jax.nn.sigmoid(c)).astype(BF16)
    o_ref[0] = _dot(c_act, w_ref[0].astype(BF16)) + b_ref[0]


def _modulation(c, w_ada, b_ada):
    depth, d_model, n_out = w_ada.shape
    batch = c.shape[0]
    return pl.pallas_call(
        _mod_kernel,
        grid=(depth, n_out // MOD_COL_TILE),
        in_specs=[
            pl.BlockSpec((batch, d_model), lambda l, j: (0, 0)),
            pl.BlockSpec((1, d_model, MOD_COL_TILE), lambda l, j: (l, 0, j)),
            pl.BlockSpec((1, 1, MOD_COL_TILE), lambda l, j: (l, 0, j)),
        ],
        out_specs=pl.BlockSpec((1, batch, MOD_COL_TILE), lambda l, j: (l, 0, j)),
        out_shape=jax.ShapeDtypeStruct((depth, batch, n_out), F32),
        compiler_params=pltpu.CompilerParams(
            dimension_semantics=("arbitrary", "arbitrary"), vmem_limit_bytes=VMEM_LIMIT_BYTES),
        name="adaln_modulation",
    )(c, w_ada, b_ada.reshape(depth, 1, n_out))


def _scan_linear_recurrence(a, u):
    n = a.shape[0]
    row = lax.broadcasted_iota(jnp.int32, a.shape, 0)
    d = 1
    while d < n:
        keep = row >= d
        a_prev = jnp.where(keep, pltpu.roll(a, d, 0), 1.0)
        u_prev = jnp.where(keep, pltpu.roll(u, d, 0), 0.0)
        u = a * u_prev + u
        a = a * a_prev
        d *= 2
    return a, u


def _gelu_tanh(x):
    return 0.5 * x * (1.0 + jnp.tanh(math.sqrt(2.0 / math.pi) * (x + 0.044715 * (x * x * x))))


def _softplus(z):
    return jnp.maximum(z, 0.0) + jnp.log1p(jnp.exp(-jnp.abs(z)))


def _mixer_in_kernel(x_ref, mod_ref, ln1_ref, w_in_ref, qg_ref, kg_ref, headsum_ref, scw_ref,
                     lcw_ref, lcb_ref, wa_ref, ba_ref, wx_ref, bx_ref, lam_ref, mng_ref,
                     qT_ref, k_ref, vT_ref, kmean_ref, ycl_ref,
                     cu_buf, lx_buf, h_carry, *, d_attn, d_conv, d_lru):
    s = pl.program_id(1)
    tm = x_ref.shape[1]
    n_blk = tm // MOBA_BLOCK

    @pl.when(s == 0)
    def _():
        cu_buf[0:HALO, :] = jnp.zeros((HALO, d_conv), F32)
        lx_buf[0:HALO, :] = jnp.zeros((HALO, d_lru), F32)
        h_carry[...] = jnp.zeros_like(h_carry)

    x = x_ref[0]
    m = mod_ref[0, 0]
    shift1, scale1 = m[0:1], m[1:2]
    h = ((x * _rms(x) * ln1_ref[...]) * (1.0 + scale1) + shift1).astype(BF16)

    o_q, o_k, o_v = 0, d_attn, 2 * d_attn
    o_b = 3 * d_attn
    o_c, o_u = o_b + d_conv, o_b + 2 * d_conv
    o_lx = o_b + 3 * d_conv
    o_lg = o_lx + d_lru

    def head_norm(t, g_ref):
        ms = _dot((t * t).astype(BF16), headsum_ref[...])
        return t * lax.rsqrt(ms + EPS) * g_ref[...]

    q = head_norm(_dot(h, w_in_ref[0, :, o_q:o_q + d_attn]), qg_ref)
    qT = (q * (LOG2_E / math.sqrt(HEAD_DIM))).T.astype(BF16)
    for c in range(n_blk):
        qT_ref[0, c] = qT[:, c * MOBA_BLOCK:(c + 1) * MOBA_BLOCK]

    k = head_norm(_dot(h, w_in_ref[0, :, o_k:o_k + d_attn]), kg_ref)
    k_ref[0] = k.astype(BF16)
    for c in range(n_blk):
        kmean_ref[0, c] = jnp.mean(k[c * MOBA_BLOCK:(c + 1) * MOBA_BLOCK], axis=0, keepdims=True)

    vT = _dot(h, w_in_ref[0, :, o_v:o_v + d_attn]).T.astype(BF16)
    for c in range(n_blk):
        vT_ref[0, c] = vT[:, c * MOBA_BLOCK:(c + 1) * MOBA_BLOCK]

    sc_b = _dot(h, w_in_ref[0, :, o_b:o_b + d_conv])
    cu = _dot(h, w_in_ref[0, :, o_c:o_c + d_conv]) * _dot(h, w_in_ref[0, :, o_u:o_u + d_conv])
    cu_buf[HALO:HALO + tm, :] = cu
    scw = scw_ref[...]
    conv = (scw[0:1] * cu_buf[HALO - 2:HALO - 2 + tm, :]
            + scw[1:2] * cu_buf[HALO - 1:HALO - 1 + tm, :]
            + scw[2:3] * cu)
    cu_buf[0:HALO, :] = cu[tm - HALO:tm]
    y_conv = sc_b * conv

    lx = _dot(h, w_in_ref[0, :, o_lx:o_lx + d_lru])
    lx_buf[HALO:HALO + tm, :] = lx
    lcw = lcw_ref[...]
    xr = (lcw[0:1] * lx_buf[HALO - 3:HALO - 3 + tm, :]
          + lcw[1:2] * lx_buf[HALO - 2:HALO - 2 + tm, :]
          + lcw[2:3] * lx_buf[HALO - 1:HALO - 1 + tm, :]
          + lcw[3:4] * lx) + lcb_ref[...]
    lx_buf[0:HALO, :] = lx[tm - HALO:tm]
    xr_b = xr.astype(BF16)
    r = jax.nn.sigmoid(_dot(xr_b, wa_ref[...]) + ba_ref[...])
    i = jax.nn.sigmoid(_dot(xr_b, wx_ref[...]) + bx_ref[...])
    log_a = (-LRU_C) * r * _softplus(-lam_ref[...])
    a = jnp.exp(log_a)
    t = jnp.tanh(log_a)
    u = jnp.sqrt((-2.0 * t) / (1.0 - t)) * (i * xr)
    a_cum, h_loc = _scan_linear_recurrence(a, u)
    hs = h_loc + a_cum * h_carry[...]
    h_carry[...] = hs[tm - 1:tm]
    y_lru = hs * _gelu_tanh(_dot(h, w_in_ref[0, :, o_lg:o_lg + d_lru]))

    mng = mng_ref[...]
    ycl_ref[0, :, 0:d_conv] = (y_conv * _rms(y_conv) * mng[:, 0:d_conv]).astype(BF16)
    ycl_ref[0, :, d_conv:d_conv + d_lru] = (y_lru * _rms(y_lru) * mng[:, d_conv:]).astype(BF16)


def _mixer_in(layer, x, mod, ln1_g, w_in_b, qg, kg, headsum, sc_w, lcw, lcb, wa_bd, ba, wx_bd, bx, lam, mng_cl,
              *, d_attn, d_conv, d_lru):
    batch, seq, d_model = x.shape
    tm = TOKEN_TILE
    n_blk = tm // MOBA_BLOCK
    nb = seq // MOBA_BLOCK

    def const(shape):
        return pl.BlockSpec(shape, lambda b, s: (0,) * len(shape))

    kern = functools.partial(_mixer_in_kernel, d_attn=d_attn, d_conv=d_conv, d_lru=d_lru)
    return pl.pallas_call(
        kern,
        grid=(batch, seq // tm),
        in_specs=[
            pl.BlockSpec((1, tm, d_model), lambda b, s: (b, s, 0)),
            pl.BlockSpec((1, 1, N_MOD, d_model), lambda b, s: (layer, b, 0, 0)),
            const((1, d_model)),
            pl.BlockSpec((1,) + w_in_b.shape[1:], lambda b, s: (layer, 0, 0)),
            const((1, d_attn)), const((1, d_attn)), const((d_attn, d_attn)),
            const(sc_w.shape), const(lcw.shape), const((1, d_lru)),
            const((d_lru, d_lru)), const((1, d_lru)), const((d_lru, d_lru)), const((1, d_lru)),
            const((1, d_lru)), const((1, d_conv + d_lru)),
        ],
        out_specs=[
            pl.BlockSpec((1, n_blk, d_attn, MOBA_BLOCK), lambda b, s: (b, s, 0, 0)),
            pl.BlockSpec((1, tm, d_attn), lambda b, s: (b, s, 0)),
            pl.BlockSpec((1, n_blk, d_attn, MOBA_BLOCK), lambda b, s: (b, s, 0, 0)),
            pl.BlockSpec((1, n_blk, 1, d_attn), lambda b, s: (b, s, 0, 0)),
            pl.BlockSpec((1, tm, d_conv + d_lru), lambda b, s: (b, s, 0)),
        ],
        out_shape=[
            jax.ShapeDtypeStruct((batch, nb, d_attn, MOBA_BLOCK), BF16),
            jax.ShapeDtypeStruct((batch, seq, d_attn), BF16),
            jax.ShapeDtypeStruct((batch, nb, d_attn, MOBA_BLOCK), BF16),
            jax.ShapeDtypeStruct((batch, nb, 1, d_attn), F32),
            jax.ShapeDtypeStruct((batch, seq, d_conv + d_lru), BF16),
        ],
        scratch_shapes=[
            pltpu.VMEM((HALO + tm, d_conv), F32),
            pltpu.VMEM((HALO + tm, d_lru), F32),
            pltpu.VMEM((1, d_lru), F32),
        ],
        compiler_params=pltpu.CompilerParams(
            dimension_semantics=("arbitrary", "arbitrary"), vmem_limit_bytes=VMEM_LIMIT_BYTES),
        name="mixer_in",
    )(x, mod, ln1_g, w_in_b, qg, kg, headsum, sc_w, lcw, lcb, wa_bd, ba, wx_bd, bx, lam, mng_cl)


def _sublane_fold(x, op):
    tiles = [x[r:r + 8] for r in range(0, x.shape[0], 8)]
    while len(tiles) > 1:
        tiles = [op(a, b) for a, b in zip(tiles[0::2], tiles[1::2])] + tiles[len(tiles) & ~1:]
    return tiles[0]


def _moba_kernel(qT_ref, k_ref, vT_ref, kmean_ref, g_ref, o_ref,
                 qcat_ref, bias_ref, s_ref, mx_ref, m_ref, acc_ref):
    i = pl.program_id(1)
    blk = MOBA_BLOCK
    d_attn = qT_ref.shape[2]
    n_heads = d_attn // HEAD_DIM
    n_pairs = n_heads // 2
    nb = kmean_ref.shape[1]
    pair = 2 * HEAD_DIM
    wide = 2 * blk

    half = lax.broadcasted_iota(jnp.int32, (pair, blk), 0) < HEAD_DIM
    blk_id = lax.broadcasted_iota(jnp.int32, (nb, blk), 0)
    past = blk_id < i
    causal = (lax.broadcasted_iota(jnp.int32, (blk, wide), 0)
              <= (lax.broadcasted_iota(jnp.int32, (blk, wide), 1) & (blk - 1)))
    kmean = kmean_ref[0, :, 0, :]
    own = pl.ds(pl.multiple_of(i * blk, blk), blk)

    for p in range(n_pairs):
        p0 = p * pair
        c0 = p * wide
        q_pair = qT_ref[0, 0, p0:p0 + pair, :]
        zero = jnp.zeros_like(q_pair)
        for hh in range(2):
            hd = 2 * p + hh
            q_m = jnp.where(half if hh == 0 else jnp.logical_not(half), q_pair, zero)
            qcat_ref[p, :, hh * blk:(hh + 1) * blk] = q_m
            gate = _dot(kmean[:, p0:p0 + pair].astype(BF16), q_m)
            bias_ref[0:nb, hd * blk:(hd + 1) * blk] = jnp.where(past, gate, -jnp.inf)

        s_t = jnp.where(causal, _dot(k_ref[0, own, p0:p0 + pair], qcat_ref[p]), -jnp.inf)
        s_ref[0, :, c0:c0 + wide] = s_t
        mx8 = _sublane_fold(s_t, jnp.maximum)
        mx_ref[:, c0:c0 + wide] = mx8
        m_own = jnp.max(mx8, axis=0, keepdims=True)
        m_ref[0, :, c0:c0 + wide] = m_own
        m_ref[1, :, c0:c0 + wide] = m_own

    cols = n_heads * blk
    gate = bias_ref[0:nb, :]
    blk_f = lax.broadcasted_iota(jnp.int32, (nb, cols), 0).astype(F32)
    keep = jnp.zeros((nb, cols), F32)
    for _ in range(MOBA_TOPK):
        top = jnp.max(gate, axis=0, keepdims=True)
        first = jnp.min(jnp.where(gate == top, blk_f, float(nb)), axis=0, keepdims=True)
        hit = blk_f == first
        keep = jnp.where(hit, 1.0, keep)
        gate = jnp.where(hit, -jnp.inf, gate)
    keep = jnp.where(blk_f < i.astype(F32), keep, 0.0)
    bias_ref[0:nb, :] = jnp.where(keep > 0.0, 0.0, -jnp.inf)
    bias_ref[nb:nb + 8, :] = jnp.zeros((8, cols), F32)
    acc_ref[...] = jnp.zeros_like(acc_ref)
    ones_rows = jnp.where(lax.broadcasted_iota(jnp.int32, (ACC_ROWS - HEAD_DIM, blk), 0) == 0,
                          1.0, 0.0).astype(BF16)

    def value_parts(slot, m_prev, key_blk, bias_row):
        m_cur = m_ref[slot]
        alpha = jnp.exp2(m_prev - m_cur)
        m_eff = m_cur - bias_ref[pl.ds(bias_row, 1), :]

        def head(hd):
            r0 = hd * HEAD_DIM
            c0 = hd * blk
            p_t = jnp.exp2(s_ref[slot, :, c0:c0 + blk] - m_eff[:, c0:c0 + blk])
            v_t = jnp.concatenate([vT_ref[0, key_blk, r0:r0 + HEAD_DIM, :], ones_rows], axis=0)
            a0 = hd * ACC_ROWS
            acc_ref[a0:a0 + ACC_ROWS, :] = (alpha[:, c0:c0 + blk] * acc_ref[a0:a0 + ACC_ROWS, :]
                                            + _dot(v_t, p_t.astype(BF16)))

        return [functools.partial(head, hd) for hd in range(n_heads)]

    def fused_step(t, cur):
        m_prev = m_ref[cur]
        rows = pl.ds(pl.multiple_of((t - 1) * blk, blk), blk)
        values = value_parts(1 - cur, m_prev, jnp.where(t == 1, i, t - 2), jnp.where(t == 1, nb, t - 2))
        for p in range(n_pairs):
            c0 = p * wide
            s_t = _dot(k_ref[0, rows, p * pair:(p + 1) * pair], qcat_ref[p])
            s_ref[cur, :, c0:c0 + wide] = s_t
            mx8 = jnp.maximum(mx_ref[:, c0:c0 + wide],
                              _sublane_fold(s_t, jnp.maximum) + bias_ref[pl.ds(t - 1, 1), c0:c0 + wide])
            mx_ref[:, c0:c0 + wide] = mx8
            m_ref[cur, :, c0:c0 + wide] = jnp.max(mx8, axis=0, keepdims=True)
            values[2 * p]()
            values[2 * p + 1]()

    def loop_body(t, carry):
        for parity in range(2):
            pl.when((t & 1) == parity)(functools.partial(fused_step, t, parity))
        return carry

    lax.fori_loop(1, i + 1, loop_body, 0)
    for parity in range(2):
        @pl.when((i & 1) == parity)
        def _(parity=parity):
            for value in value_parts(parity, m_ref[1 - parity],
                                     jnp.where(i == 0, i, i - 1), jnp.where(i == 0, nb, i - 1)):
                value()

    y_t = jnp.concatenate(
        [acc_ref[hd * ACC_ROWS:hd * ACC_ROWS + HEAD_DIM, :]
         / acc_ref[hd * ACC_ROWS + HEAD_DIM:hd * ACC_ROWS + HEAD_DIM + 1, :]
         for hd in range(n_heads)], axis=0)
    y_t = y_t * _rms(y_t, axis=0)
    o_ref[0] = (y_t.T * g_ref[...]).astype(BF16)


def _moba(qT, k, vT, kmean, mng_attn):
    batch, nb, d_attn, blk = qT.shape
    seq = k.shape[1]
    n_heads = d_attn // HEAD_DIM
    cols = n_heads * blk

    return pl.pallas_call(
        _moba_kernel,
        grid=(batch, nb),
        in_specs=[
            pl.BlockSpec((1, 1, d_attn, blk), lambda b, i: (b, i, 0, 0)),
            pl.BlockSpec((1, seq, d_attn), lambda b, i: (b, 0, 0)),
            pl.BlockSpec((1, nb, d_attn, blk), lambda b, i: (b, 0, 0, 0)),
            pl.BlockSpec((1, nb, 1, d_attn), lambda b, i: (b, 0, 0, 0)),
            pl.BlockSpec((1, d_attn), lambda b, i: (0, 0)),
        ],
        out_specs=pl.BlockSpec((1, blk, d_attn), lambda b, i: (b, i, 0)),
        out_shape=jax.ShapeDtypeStruct((batch, seq, d_attn), BF16),
        scratch_shapes=[
            pltpu.VMEM((n_heads // 2, 2 * HEAD_DIM, 2 * blk), BF16),
            pltpu.VMEM((nb + 8, cols), F32),
            pltpu.VMEM((2, blk, cols), F32),
            pltpu.VMEM((8, cols), F32),
            pltpu.VMEM((2, 1, cols), F32),
            pltpu.VMEM((n_heads * ACC_ROWS, blk), F32),
        ],
        compiler_params=pltpu.CompilerParams(
            dimension_semantics=("arbitrary", "arbitrary"), vmem_limit_bytes=VMEM_LIMIT_BYTES),
        name="moba_attention",
    )(qT, k, vT, kmean, mng_attn)


def _out_mlp_kernel(x_ref, ya_ref, ycl_ref, mod_ref, ln2_ref, wout_ref, wup_ref, wdown_ref, o_ref):
    d_attn = ya_ref.shape[2]
    d_ff = wup_ref.shape[2]
    x = x_ref[0]
    m = mod_ref[0, 0]
    gate1, shift2, scale2, gate2 = m[2:3], m[3:4], m[4:5], m[5:6]
    mix = _dot(ya_ref[0], wout_ref[0, 0:d_attn, :]) + _dot(ycl_ref[0], wout_ref[0, d_attn:, :])
    x1 = x + gate1 * mix
    h2 = ((x1 * _rms(x1) * ln2_ref[...]) * (1.0 + scale2) + shift2).astype(BF16)
    ff = jnp.zeros_like(x1)
    for c0 in range(0, d_ff, FF_CHUNK):
        up = _dot(h2, wup_ref[0, :, c0:c0 + FF_CHUNK])
        act = jnp.square(jnp.maximum(up, 0.0)).astype(BF16)
        ff = ff + _dot(act, wdown_ref[0, c0:c0 + FF_CHUNK, :])
    o_ref[0] = x1 + gate2 * ff


def _out_mlp(layer, x, ya, ycl, mod, ln2_g, w_out_b, w_up_b, w_down_b):
    batch, seq, d_model = x.shape
    tm = TOKEN_TILE
    d_attn = ya.shape[2]
    d_cl = ycl.shape[2]

    def layer_weight(w):
        return pl.BlockSpec((1,) + w.shape[1:], lambda b, s: (layer, 0, 0), pipeline_mode=pl.Buffered(1))

    return pl.pallas_call(
        _out_mlp_kernel,
        grid=(batch, seq // tm),
        in_specs=[
            pl.BlockSpec((1, tm, d_model), lambda b, s: (b, s, 0)),
            pl.BlockSpec((1, tm, d_attn), lambda b, s: (b, s, 0)),
            pl.BlockSpec((1, tm, d_cl), lambda b, s: (b, s, 0)),
            pl.BlockSpec((1, 1, N_MOD, d_model), lambda b, s: (layer, b, 0, 0)),
            pl.BlockSpec((1, d_model), lambda b, s: (0, 0)),
            layer_weight(w_out_b), layer_weight(w_up_b), layer_weight(w_down_b),
        ],
        out_specs=pl.BlockSpec((1, tm, d_model), lambda b, s: (b, s, 0)),
        out_shape=jax.ShapeDtypeStruct((batch, seq, d_model), F32),
        compiler_params=pltpu.CompilerParams(
            dimension_semantics=("arbitrary", "arbitrary"), vmem_limit_bytes=VMEM_LIMIT_BYTES),
        name="out_mlp",
    )(x, ya, ycl, mod, ln2_g, w_out_b, w_up_b, w_down_b)


def _block_diag(w):
    n, r, c = w.shape
    eye = jnp.eye(n, dtype=w.dtype)
    return (eye[:, None, :, None] * w[:, :, None, :]).reshape(n * r, n * c)


def kernel(x, c, ln1_g, ln2_g, w_ada, b_ada, w_in, q_norm_g, k_norm_g, sc_w, lru_conv_w, lru_conv_b,
           lru_wa, lru_ba, lru_wx, lru_bx, lru_lambda, mix_norm_g, w_out, w_up, w_down):
    batch, seq, d_model = x.shape
    depth = w_in.shape[0]
    d_conv = sc_w.shape[2]
    d_lru = lru_conv_w.shape[2]
    d_attn = mix_norm_g.shape[1] - d_conv - d_lru
    n_heads = d_attn // HEAD_DIM
    assert seq % TOKEN_TILE == 0 and TOKEN_TILE % MOBA_BLOCK == 0
    assert w_in.shape[2] == 3 * d_attn + 3 * d_conv + 2 * d_lru

    mod = _modulation(c, w_ada, b_ada).reshape(depth, batch, N_MOD, d_model)
    headsum = _block_diag(jnp.full((n_heads, HEAD_DIM, HEAD_DIM), 1.0 / HEAD_DIM, F32)).astype(BF16)

    w_in_b, w_out_b, w_up_b, w_down_b = (w.astype(BF16) for w in (w_in, w_out, w_up, w_down))
    for l in range(depth):
        row = lambda v: v.reshape(1, -1)
        qT, k, vT, kmean, ycl = _mixer_in(
            l, x, mod, row(ln1_g[l]), w_in_b,
            row(jnp.tile(q_norm_g[l], n_heads)), row(jnp.tile(k_norm_g[l], n_heads)), headsum,
            sc_w[l], lru_conv_w[l], row(lru_conv_b[l]),
            _block_diag(lru_wa[l]).astype(BF16), row(lru_ba[l]),
            _block_diag(lru_wx[l]).astype(BF16), row(lru_bx[l]),
            row(lru_lambda[l]), row(mix_norm_g[l, d_attn:]),
            d_attn=d_attn, d_conv=d_conv, d_lru=d_lru)
        ya = _moba(qT, k, vT, kmean, row(mix_norm_g[l, :d_attn]))
        x = _out_mlp(l, x, ya, ycl, mod, row(ln2_g[l]), w_out_b, w_up_b, w_down_b)
    return x
```

```python
import functools
import math

import jax
import jax.numpy as jnp
from jax import lax
from jax.experimental import pallas as pl
from jax.experimental.pallas import tpu as pltpu

F32 = jnp.float32
BF16 = jnp.bfloat16

HEAD_DIM = 64
MOBA_BLOCK = 256
MOBA_TOPK = 3
LRU_C = 8.0
N_MOD = 6
EPS = 1e-6

TOKEN_TILE = 512
MOD_COL_TILE = 1536
FF_CHUNK = 1024
HALO = 8
ACC_ROWS = HEAD_DIM + 16
LOG2_E = math.log2(math.e)
QUERY_BLOCKS = 2
M_FLOOR = -1e30
VMEM_LIMIT_BYTES = 56 * 1024 * 1024


def _rms(x, axis=-1):
    return lax.rsqrt(jnp.mean(x * x, axis=axis, keepdims=True) + EPS)


def _dot(a, b):
    return jnp.dot(a, b, preferred_element_type=F32)


def _mod_kernel(c_ref, w_ref, b_ref, o_ref):
    c = c_ref[...]
    c_act = (c * jax.nn.sigmoid(c)).astype(BF16)
    o_ref[0] = _dot(c_act, w_ref[0].astype(BF16)) + b_ref[0]


def _modulation(c, w_ada, b_ada):
    depth, d_model, n_out = w_ada.shape
    batch = c.shape[0]
    return pl.pallas_call(
        _mod_kernel,
        grid=(depth, n_out // MOD_COL_TILE),
        in_specs=[
            pl.BlockSpec((batch, d_model), lambda l, j: (0, 0)),
            pl.BlockSpec((1, d_model, MOD_COL_TILE), lambda l, j: (l, 0, j)),
            pl.BlockSpec((1, 1, MOD_COL_TILE), lambda l, j: (l, 0, j)),
        ],
        out_specs=pl.BlockSpec((1, batch, MOD_COL_TILE), lambda l, j: (l, 0, j)),
        out_shape=jax.ShapeDtypeStruct((depth, batch, n_out), F32),
        compiler_params=pltpu.CompilerParams(
            dimension_semantics=("arbitrary", "arbitrary"), vmem_limit_bytes=VMEM_LIMIT_BYTES),
        name="adaln_modulation",
    )(c, w_ada, b_ada.reshape(depth, 1, n_out))


def _scan_linear_recurrence(a, u):
    n = a.shape[0]
    row = lax.broadcasted_iota(jnp.int32, a.shape, 0)
    d = 1
    while d < n:
        keep = row >= d
        a_prev = jnp.where(keep, pltpu.roll(a, d, 0), 1.0)
        u_prev = jnp.where(keep, pltpu.roll(u, d, 0), 0.0)
        u = a * u_prev + u
        a = a * a_prev
        d *= 2
    return a, u


def _gelu_tanh(x):
    return 0.5 * x * (1.0 + jnp.tanh(math.sqrt(2.0 / math.pi) * (x + 0.044715 * (x * x * x))))


def _softplus(z):
    return jnp.maximum(z, 0.0) + jnp.log1p(jnp.exp(-jnp.abs(z)))


def _mixer_in_kernel(x_ref, mod_ref, ln1_ref, w_in_ref, qg_ref, kg_ref, headsum_ref, scw_ref,
                     lcw_ref, lcb_ref, wa_ref, ba_ref, wx_ref, bx_ref, lam_ref, mng_ref,
                     qT_ref, k_ref, vT_ref, kmean_ref, ycl_ref,
                     cu_buf, lx_buf, h_carry, *, d_attn, d_conv, d_lru):
    s = pl.program_id(1)
    tm = x_ref.shape[1]
    n_blk = tm // MOBA_BLOCK

    @pl.when(s == 0)
    def _():
        cu_buf[0:HALO, :] = jnp.zeros((HALO, d_conv), F32)
        lx_buf[0:HALO, :] = jnp.zeros((HALO, d_lru), F32)
        h_carry[...] = jnp.zeros_like(h_carry)

    x = x_ref[0]
    m = mod_ref[0, 0]
    shift1, scale1 = m[0:1], m[1:2]
    h = ((x * _rms(x) * ln1_ref[...]) * (1.0 + scale1) + shift1).astype(BF16)

    o_q, o_k, o_v = 0, d_attn, 2 * d_attn
    o_b = 3 * d_attn
    o_c, o_u = o_b + d_conv, o_b + 2 * d_conv
    o_lx = o_b + 3 * d_conv
    o_lg = o_lx + d_lru

    def head_norm(t, g_ref):
        ms = _dot((t * t).astype(BF16), headsum_ref[...])
        return t * lax.rsqrt(ms + EPS) * g_ref[...]

    q = head_norm(_dot(h, w_in_ref[0, :, o_q:o_q + d_attn]), qg_ref)
    qT = (q * (LOG2_E / math.sqrt(HEAD_DIM))).T.astype(BF16)
    for c in range(n_blk):
        qT_ref[0, c] = qT[:, c * MOBA_BLOCK:(c + 1) * MOBA_BLOCK]

    k = head_norm(_dot(h, w_in_ref[0, :, o_k:o_k + d_attn]), kg_ref)
    k_ref[0] = k.astype(BF16)
    for c in range(n_blk):
        kmean_ref[0, c] = jnp.mean(k[c * MOBA_BLOCK:(c + 1) * MOBA_BLOCK], axis=0, keepdims=True)

    vT = _dot(h, w_in_ref[0, :, o_v:o_v + d_attn]).T.astype(BF16)
    for c in range(n_blk):
        vT_ref[0, c] = vT[:, c * MOBA_BLOCK:(c + 1) * MOBA_BLOCK]

    sc_b = _dot(h, w_in_ref[0, :, o_b:o_b + d_conv])
    cu = _dot(h, w_in_ref[0, :, o_c:o_c + d_conv]) * _dot(h, w_in_ref[0, :, o_u:o_u + d_conv])
    cu_buf[HALO:HALO + tm, :] = cu
    scw = scw_ref[...]
    conv = (scw[0:1] * cu_buf[HALO - 2:HALO - 2 + tm, :]
            + scw[1:2] * cu_buf[HALO - 1:HALO - 1 + tm, :]
            + scw[2:3] * cu)
    cu_buf[0:HALO, :] = cu[tm - HALO:tm]
    y_conv = sc_b * conv

    lx = _dot(h, w_in_ref[0, :, o_lx:o_lx + d_lru])
    lx_buf[HALO:HALO + tm, :] = lx
    lcw = lcw_ref[...]
    xr = (lcw[0:1] * lx_buf[HALO - 3:HALO - 3 + tm, :]
          + lcw[1:2] * lx_buf[HALO - 2:HALO - 2 + tm, :]
          + lcw[2:3] * lx_buf[HALO - 1:HALO - 1 + tm, :]
          + lcw[3:4] * lx) + lcb_ref[...]
    lx_buf[0:HALO, :] = lx[tm - HALO:tm]
    xr_b = xr.astype(BF16)
    r = jax.nn.sigmoid(_dot(xr_b, wa_ref[...]) + ba_ref[...])
    i = jax.nn.sigmoid(_dot(xr_b, wx_ref[...]) + bx_ref[...])
    log_a = (-LRU_C) * r * _softplus(-lam_ref[...])
    a = jnp.exp(log_a)
    t = jnp.tanh(log_a)
    u = jnp.sqrt((-2.0 * t) / (1.0 - t)) * (i * xr)
    a_cum, h_loc = _scan_linear_recurrence(a, u)
    hs = h_loc + a_cum * h_carry[...]
    h_carry[...] = hs[tm - 1:tm]
    y_lru = hs * _gelu_tanh(_dot(h, w_in_ref[0, :, o_lg:o_lg + d_lru]))

    mng = mng_ref[...]
    ycl_ref[0, :, 0:d_conv] = (y_conv * _rms(y_conv) * mng[:, 0:d_conv]).astype(BF16)
    ycl_ref[0, :, d_conv:d_conv + d_lru] = (y_lru * _rms(y_lru) * mng[:, d_conv:]).astype(BF16)


def _mixer_in(layer, x, mod, ln1_g, w_in_b, qg, kg, headsum, sc_w, lcw, lcb, wa_bd, ba, wx_bd, bx, lam, mng_cl,
              *, d_attn, d_conv, d_lru):
    batch, seq, d_model = x.shape
    tm = TOKEN_TILE
    n_blk = tm // MOBA_BLOCK
    nb = seq // MOBA_BLOCK

    def const(shape):
        return pl.BlockSpec(shape, lambda b, s: (0,) * len(shape))

    kern = functools.partial(_mixer_in_kernel, d_attn=d_attn, d_conv=d_conv, d_lru=d_lru)
    return pl.pallas_call(
        kern,
        grid=(batch, seq // tm),
        in_specs=[
            pl.BlockSpec((1, tm, d_model), lambda b, s: (b, s, 0)),
            pl.BlockSpec((1, 1, N_MOD, d_model), lambda b, s: (layer, b, 0, 0)),
            const((1, d_model)),
            pl.BlockSpec((1,) + w_in_b.shape[1:], lambda b, s: (layer, 0, 0)),
            const((1, d_attn)), const((1, d_attn)), const((d_attn, d_attn)),
            const(sc_w.shape), const(lcw.shape), const((1, d_lru)),
            const((d_lru, d_lru)), const((1, d_lru)), const((d_lru, d_lru)), const((1, d_lru)),
            const((1, d_lru)), const((1, d_conv + d_lru)),
        ],
        out_specs=[
            pl.BlockSpec((1, n_blk, d_attn, MOBA_BLOCK), lambda b, s: (b, s, 0, 0)),
            pl.BlockSpec((1, tm, d_attn), lambda b, s: (b, s, 0)),
            pl.BlockSpec((1, n_blk, d_attn, MOBA_BLOCK), lambda b, s: (b, s, 0, 0)),
            pl.BlockSpec((1, n_blk, 1, d_attn), lambda b, s: (b, s, 0, 0)),
            pl.BlockSpec((1, tm, d_conv + d_lru), lambda b, s: (b, s, 0)),
        ],
        out_shape=[
            jax.ShapeDtypeStruct((batch, nb, d_attn, MOBA_BLOCK), BF16),
            jax.ShapeDtypeStruct((batch, seq, d_attn), BF16),
            jax.ShapeDtypeStruct((batch, nb, d_attn, MOBA_BLOCK), BF16),
            jax.ShapeDtypeStruct((batch, nb, 1, d_attn), F32),
            jax.ShapeDtypeStruct((batch, seq, d_conv + d_lru), BF16),
        ],
        scratch_shapes=[
            pltpu.VMEM((HALO + tm, d_conv), F32),
            pltpu.VMEM((HALO + tm, d_lru), F32),
            pltpu.VMEM((1, d_lru), F32),
        ],
        compiler_params=pltpu.CompilerParams(
            dimension_semantics=("arbitrary", "arbitrary"), vmem_limit_bytes=VMEM_LIMIT_BYTES),
        name="mixer_in",
    )(x, mod, ln1_g, w_in_b, qg, kg, headsum, sc_w, lcw, lcb, wa_bd, ba, wx_bd, bx, lam, mng_cl)


def _sublane_fold(x, op):
    tiles = [x[r:r + 8] for r in range(0, x.shape[0], 8)]
    while len(tiles) > 1:
        tiles = [op(a, b) for a, b in zip(tiles[0::2], tiles[1::2])] + tiles[len(tiles) & ~1:]
    return tiles[0]


def _moba_kernel(qT_ref, k_ref, vT_ref, kmean_ref, g_ref, o_ref,
                 qcat_ref, bias_ref, s_ref, mx_ref, m_ref, acc_ref):
    g = pl.program_id(1)
    blk = MOBA_BLOCK
    qw = QUERY_BLOCKS * blk
    d_attn = qT_ref.shape[2]
    n_heads = d_attn // HEAD_DIM
    n_pairs = n_heads // 2
    nb = kmean_ref.shape[1]
    pair = 2 * HEAD_DIM
    cols = n_heads * qw
    first_blk = QUERY_BLOCKS * g

    half = lax.broadcasted_iota(jnp.int32, (pair, qw), 0) < HEAD_DIM
    kmean = kmean_ref[0, :, 0, :]

    def query_block_of(shape, axis):
        return (lax.broadcasted_iota(jnp.int32, shape, axis) & (qw - 1)) // blk

    past_q = lax.broadcasted_iota(jnp.int32, (nb, qw), 0) < first_blk + query_block_of((nb, qw), 1)
    for p in range(n_pairs):
        p0 = p * pair
        q_pair = jnp.concatenate([qT_ref[0, c, p0:p0 + pair, :] for c in range(QUERY_BLOCKS)], axis=1)
        zero = jnp.zeros_like(q_pair)
        for hh in range(2):
            hd = 2 * p + hh
            q_m = jnp.where(half if hh == 0 else jnp.logical_not(half), q_pair, zero)
            qcat_ref[p, :, hh * qw:(hh + 1) * qw] = q_m
            gate = _dot(kmean[:, p0:p0 + pair].astype(BF16), q_m)
            bias_ref[0:nb, hd * qw:(hd + 1) * qw] = jnp.where(past_q, gate, -jnp.inf)

    ones_rows = jnp.where(lax.broadcasted_iota(jnp.int32, (ACC_ROWS - HEAD_DIM, blk), 0) == 0,
                          1.0, 0.0).astype(BF16)
    key_pos = lax.broadcasted_iota(jnp.int32, (blk, qw), 0)
    qry_pos = lax.broadcasted_iota(jnp.int32, (blk, qw), 1)

    def score_parts(slot, key_blk, bias_row, diag):
        rows = pl.ds(pl.multiple_of(key_blk * blk, blk), blk)
        if diag is not None:
            visible = key_pos + diag * blk <= qry_pos

        def head(hd):
            p, hh = divmod(hd, 2)
            c0 = hd * qw
            s_t = _dot(k_ref[0, rows, p * pair:(p + 1) * pair], qcat_ref[p, :, hh * qw:(hh + 1) * qw])
            if diag is not None:
                s_t = jnp.where(visible, s_t, -jnp.inf)
            s_ref[slot, :, c0:c0 + qw] = s_t
            if bias_row is None:
                mx_ref[:, c0:c0 + qw] = _sublane_fold(s_t, jnp.maximum)
                return
            mx8 = jnp.maximum(mx_ref[:, c0:c0 + qw],
                              _sublane_fold(s_t, jnp.maximum) + bias_ref[pl.ds(bias_row, 1), c0:c0 + qw])
            mx_ref[:, c0:c0 + qw] = mx8
            m_ref[slot, :, c0:c0 + qw] = jnp.maximum(jnp.max(mx8, axis=0, keepdims=True), M_FLOOR)

        return [functools.partial(head, hd) for hd in range(n_heads)]

    def value_parts(slot, m_prev, key_blk, bias_row, first=False):
        m_cur = m_ref[slot]
        alpha = jnp.exp2(m_prev - m_cur)
        m_eff = m_cur - bias_ref[pl.ds(bias_row, 1), :]

        def head(hd):
            r0 = hd * HEAD_DIM
            c0 = hd * qw
            p_t = jnp.exp2(s_ref[slot, :, c0:c0 + qw] - m_eff[:, c0:c0 + qw])
            v_t = jnp.concatenate([vT_ref[0, key_blk, r0:r0 + HEAD_DIM, :], ones_rows], axis=0)
            a0 = hd * ACC_ROWS
            update = _dot(v_t, p_t.astype(BF16))
            if not first:
                update = alpha[:, c0:c0 + qw] * acc_ref[a0:a0 + ACC_ROWS, :] + update
            acc_ref[a0:a0 + ACC_ROWS, :] = update

        return [functools.partial(head, hd) for hd in range(n_heads)]

    def step_blocks(t):
        diagonal = t < QUERY_BLOCKS
        return jnp.where(diagonal, first_blk + t, t - QUERY_BLOCKS), jnp.where(diagonal, nb + t, t - QUERY_BLOCKS)

    def interleave(scores, values):
        for score, value in zip(scores, values):
            score()
            value()

    for score in score_parts(0, first_blk, None, 0):
        score()

    gate = bias_ref[0:nb, :]
    blk_i = lax.broadcasted_iota(jnp.int32, (nb, cols), 0)
    blk_f = blk_i.astype(F32)
    keep = jnp.zeros((nb, cols), F32)
    for _ in range(MOBA_TOPK):
        top = jnp.max(gate, axis=0, keepdims=True)
        first = jnp.min(jnp.where(gate == top, blk_f, float(nb)), axis=0, keepdims=True)
        hit = blk_f == first
        keep = jnp.where(hit, 1.0, keep)
        gate = jnp.where(hit, -jnp.inf, gate)
    keep = jnp.where(blk_i < first_blk + query_block_of((nb, cols), 1), keep, 0.0)
    bias_ref[0:nb, :] = jnp.where(keep > 0.0, 0.0, -jnp.inf)
    col_c = query_block_of((1, cols), 1)
    for c in range(QUERY_BLOCKS):
        bias_ref[nb + c:nb + c + 1, :] = jnp.where(col_c > c, bias_ref[pl.ds(first_blk + c, 1), :], 0.0)

    mx8 = mx_ref[...] + bias_ref[nb:nb + 1, :]
    mx_ref[...] = mx8
    m_first = jnp.maximum(jnp.max(mx8, axis=0, keepdims=True), M_FLOOR)
    m_ref[0] = m_first
    m_ref[1] = m_first

    for t in range(1, QUERY_BLOCKS):
        cur = t & 1
        m_prev = m_ref[cur]
        interleave(score_parts(cur, first_blk + t, nb + t, t),
                   value_parts(1 - cur, m_prev, first_blk + t - 1, nb + t - 1, first=t == 1))

    def fused_step(t, cur):
        m_prev = m_ref[cur]
        interleave(score_parts(cur, t - QUERY_BLOCKS, t - QUERY_BLOCKS, None),
                   value_parts(1 - cur, m_prev, *step_blocks(t - 1)))

    def loop_body(t, carry):
        for parity in range(2):
            pl.when((t & 1) == parity)(functools.partial(fused_step, t, parity))
        return carry

    n_steps = first_blk + QUERY_BLOCKS
    lax.fori_loop(QUERY_BLOCKS, n_steps, loop_body, 0)
    for parity in range(2):
        @pl.when(((n_steps - 1) & 1) == parity)
        def _(parity=parity):
            for value in value_parts(parity, m_ref[1 - parity], *step_blocks(n_steps - 1)):
                value()

    y_t = jnp.concatenate(
        [acc_ref[hd * ACC_ROWS:hd * ACC_ROWS + HEAD_DIM, :]
         / acc_ref[hd * ACC_ROWS + HEAD_DIM:hd * ACC_ROWS + HEAD_DIM + 1, :]
         for hd in range(n_heads)], axis=0)
    y_t = y_t * _rms(y_t, axis=0)
    o_ref[0] = (y_t.T * g_ref[...]).astype(BF16)


def _moba(qT, k, vT, kmean, mng_attn):
    batch, nb, d_attn, blk = qT.shape
    seq = k.shape[1]
    n_heads = d_attn // HEAD_DIM
    qw = QUERY_BLOCKS * blk
    cols = n_heads * qw
    assert QUERY_BLOCKS >= 2 and nb % QUERY_BLOCKS == 0

    return pl.pallas_call(
        _moba_kernel,
        grid=(batch, nb // QUERY_BLOCKS),
        in_specs=[
            pl.BlockSpec((1, QUERY_BLOCKS, d_attn, blk), lambda b, g: (b, g, 0, 0)),
            pl.BlockSpec((1, seq, d_attn), lambda b, g: (b, 0, 0)),
            pl.BlockSpec((1, nb, d_attn, blk), lambda b, g: (b, 0, 0, 0)),
            pl.BlockSpec((1, nb, 1, d_attn), lambda b, g: (b, 0, 0, 0)),
            pl.BlockSpec((1, d_attn), lambda b, g: (0, 0)),
        ],
        out_specs=pl.BlockSpec((1, qw, d_attn), lambda b, g: (b, g, 0)),
        out_shape=jax.ShapeDtypeStruct((batch, seq, d_attn), BF16),
        scratch_shapes=[
            pltpu.VMEM((n_heads // 2, 2 * HEAD_DIM, 2 * qw), BF16),
            pltpu.VMEM((nb + 8, cols), F32),
            pltpu.VMEM((2, blk, cols), F32),
            pltpu.VMEM((8, cols), F32),
            pltpu.VMEM((2, 1, cols), F32),
            pltpu.VMEM((n_heads * ACC_ROWS, qw), F32),
        ],
        compiler_params=pltpu.CompilerParams(
            dimension_semantics=("arbitrary", "arbitrary"), vmem_limit_bytes=VMEM_LIMIT_BYTES),
        name="moba_attention",
    )(qT, k, vT, kmean, mng_attn)


def _out_mlp_kernel(x_ref, ya_ref, ycl_ref, mod_ref, ln2_ref, wout_ref, wup_ref, wdown_ref, o_ref):
    d_attn = ya_ref.shape[2]
    d_ff = wup_ref.shape[2]
    x = x_ref[0]
    m = mod_ref[0, 0]
    gate1, shift2, scale2, gate2 = m[2:3], m[3:4], m[4:5], m[5:6]
    mix = _dot(ya_ref[0], wout_ref[0, 0:d_attn, :]) + _dot(ycl_ref[0], wout_ref[0, d_attn:, :])
    x1 = x + gate1 * mix
    h2 = ((x1 * _rms(x1) * ln2_ref[...]) * (1.0 + scale2) + shift2).astype(BF16)
    ff = jnp.zeros_like(x1)
    for c0 in range(0, d_ff, FF_CHUNK):
        up = _dot(h2, wup_ref[0, :, c0:c0 + FF_CHUNK])
        act = jnp.square(jnp.maximum(up, 0.0)).astype(BF16)
        ff = ff + _dot(act, wdown_ref[0, c0:c0 + FF_CHUNK, :])
    o_ref[0] = x1 + gate2 * ff


def _out_mlp(layer, x, ya, ycl, mod, ln2_g, w_out_b, w_up_b, w_down_b):
    batch, seq, d_model = x.shape
    tm = TOKEN_TILE
    d_attn = ya.shape[2]
    d_cl = ycl.shape[2]

    def layer_weight(w):
        return pl.BlockSpec((1,) + w.shape[1:], lambda b, s: (layer, 0, 0), pipeline_mode=pl.Buffered(1))

    return pl.pallas_call(
        _out_mlp_kernel,
        grid=(batch, seq // tm),
        in_specs=[
            pl.BlockSpec((1, tm, d_model), lambda b, s: (b, s, 0)),
            pl.BlockSpec((1, tm, d_attn), lambda b, s: (b, s, 0)),
            pl.BlockSpec((1, tm, d_cl), lambda b, s: (b, s, 0)),
            pl.BlockSpec((1, 1, N_MOD, d_model), lambda b, s: (layer, b, 0, 0)),
            pl.BlockSpec((1, d_model), lambda b, s: (0, 0)),
            layer_weight(w_out_b), layer_weight(w_up_b), layer_weight(w_down_b),
        ],
        out_specs=pl.BlockSpec((1, tm, d_model), lambda b, s: (b, s, 0)),
        out_shape=jax.ShapeDtypeStruct((batch, seq, d_model), F32),
        compiler_params=pltpu.CompilerParams(
            dimension_semantics=("arbitrary", "arbitrary"), vmem_limit_bytes=VMEM_LIMIT_BYTES),
        name="out_mlp",
    )(x, ya, ycl, mod, ln2_g, w_out_b, w_up_b, w_down_b)


def _block_diag(w):
    n, r, c = w.shape
    eye = jnp.eye(n, dtype=w.dtype)
    return (eye[:, None, :, None] * w[:, :, None, :]).reshape(n * r, n * c)


def kernel(x, c, ln1_g, ln2_g, w_ada, b_ada, w_in, q_norm_g, k_norm_g, sc_w, lru_conv_w, lru_conv_b,
           lru_wa, lru_ba, lru_wx, lru_bx, lru_lambda, mix_norm_g, w_out, w_up, w_down):
    batch, seq, d_model = x.shape
    depth = w_in.shape[0]
    d_conv = sc_w.shape[2]
    d_lru = lru_conv_w.shape[2]
    d_attn = mix_norm_g.shape[1] - d_conv - d_lru
    n_heads = d_attn // HEAD_DIM
    assert seq % TOKEN_TILE == 0 and TOKEN_TILE % MOBA_BLOCK == 0
    assert w_in.shape[2] == 3 * d_attn + 3 * d_conv + 2 * d_lru

    mod = _modulation(c, w_ada, b_ada).reshape(depth, batch, N_MOD, d_model)
    headsum = _block_diag(jnp.full((n_heads, HEAD_DIM, HEAD_DIM), 1.0 / HEAD_DIM, F32)).astype(BF16)

    w_in_b, w_out_b, w_up_b, w_down_b = (w.astype(BF16) for w in (w_in, w_out, w_up, w_down))
    for l in range(depth):
        row = lambda v: v.reshape(1, -1)
        qT, k, vT, kmean, ycl = _mixer_in(
            l, x, mod, row(ln1_g[l]), w_in_b,
            row(jnp.tile(q_norm_g[l], n_heads)), row(jnp.tile(k_norm_g[l], n_heads)), headsum,
            sc_w[l], lru_conv_w[l], row(lru_conv_b[l]),
            _block_diag(lru_wa[l]).astype(BF16), row(lru_ba[l]),
            _block_diag(lru_wx[l]).astype(BF16), row(lru_bx[l]),
            row(lru_lambda[l]), row(mix_norm_g[l, d_attn:]),
            d_attn=d_attn, d_conv=d_conv, d_lru=d_lru)
        ya = _moba(qT, k, vT, kmean, row(mix_norm_g[l, :d_attn]))
        x = _out_mlp(l, x, ya, ycl, mod, row(ln2_g[l]), w_out_b, w_up_b, w_down_b)
    return x
```

```python
import functools
import math

import jax
import jax.numpy as jnp
from jax import lax
from jax.experimental import pallas as pl
from jax.experimental.pallas import tpu as pltpu

F32 = jnp.float32
BF16 = jnp.bfloat16

HEAD_DIM = 64
MOBA_BLOCK = 256
MOBA_TOPK = 3
LRU_C = 8.0
N_MOD = 6
EPS = 1e-6

TOKEN_TILE = 512
MLP_TOKEN_TILE = 1024
MOD_COL_TILE = 1536
FF_CHUNK = 1024
HALO = 8
ACC_ROWS = HEAD_DIM + 16
LOG2_E = math.log2(math.e)
QUERY_BLOCKS = 2
M_FLOOR = -1e30
VMEM_LIMIT_BYTES = 56 * 1024 * 1024


def _rms(x, axis=-1):
    return lax.rsqrt(jnp.mean(x * x, axis=axis, keepdims=True) + EPS)


def _dot(a, b):
    return jnp.dot(a, b, preferred_element_type=F32)


def _mod_kernel(c_ref, w_ref, b_ref, o_ref):
    c = c_ref[...]
    c_act = (c * jax.nn.sigmoid(c)).astype(BF16)
    o_ref[0] = _dot(c_act, w_ref[0].astype(BF16)) + b_ref[0]


def _modulation(c, w_ada, b_ada):
    depth, d_model, n_out = w_ada.shape
    batch = c.shape[0]
    return pl.pallas_call(
        _mod_kernel,
        grid=(depth, n_out // MOD_COL_TILE),
        in_specs=[
            pl.BlockSpec((batch, d_model), lambda l, j: (0, 0)),
            pl.BlockSpec((1, d_model, MOD_COL_TILE), lambda l, j: (l, 0, j)),
            pl.BlockSpec((1, 1, MOD_COL_TILE), lambda l, j: (l, 0, j)),
        ],
        out_specs=pl.BlockSpec((1, batch, MOD_COL_TILE), lambda l, j: (l, 0, j)),
        out_shape=jax.ShapeDtypeStruct((depth, batch, n_out), F32),
        compiler_params=pltpu.CompilerParams(
            dimension_semantics=("arbitrary", "arbitrary"), vmem_limit_bytes=VMEM_LIMIT_BYTES),
        name="adaln_modulation",
    )(c, w_ada, b_ada.reshape(depth, 1, n_out))


def _scan_linear_recurrence(a, u):
    n = a.shape[0]
    row = lax.broadcasted_iota(jnp.int32, a.shape, 0)
    d = 1
    while d < n:
        keep = row >= d
        a_prev = jnp.where(keep, pltpu.roll(a, d, 0), 1.0)
        u_prev = jnp.where(keep, pltpu.roll(u, d, 0), 0.0)
        u = a * u_prev + u
        a = a * a_prev
        d *= 2
    return a, u


def _gelu_tanh(x):
    return 0.5 * x * (1.0 + jnp.tanh(math.sqrt(2.0 / math.pi) * (x + 0.044715 * (x * x * x))))


def _softplus(z):
    return jnp.maximum(z, 0.0) + jnp.log1p(jnp.exp(-jnp.abs(z)))


def _mixer_in_kernel(x_ref, mod_ref, ln1_ref, w_in_ref, qg_ref, kg_ref, headsum_ref, scw_ref,
                     lcw_ref, lcb_ref, wa_ref, ba_ref, wx_ref, bx_ref, lam_ref, mng_ref,
                     qT_ref, k_ref, vT_ref, kmean_ref, ycl_ref,
                     cu_buf, lx_buf, h_carry, *, d_attn, d_conv, d_lru):
    s = pl.program_id(1)
    tm = x_ref.shape[1]
    n_blk = tm // MOBA_BLOCK

    @pl.when(s == 0)
    def _():
        cu_buf[0:HALO, :] = jnp.zeros((HALO, d_conv), F32)
        lx_buf[0:HALO, :] = jnp.zeros((HALO, d_lru), F32)
        h_carry[...] = jnp.zeros_like(h_carry)

    x = x_ref[0]
    m = mod_ref[0, 0]
    shift1, scale1 = m[0:1], m[1:2]
    h = ((x * _rms(x) * ln1_ref[...]) * (1.0 + scale1) + shift1).astype(BF16)

    o_q, o_k, o_v = 0, d_attn, 2 * d_attn
    o_b = 3 * d_attn
    o_c, o_u = o_b + d_conv, o_b + 2 * d_conv
    o_lx = o_b + 3 * d_conv
    o_lg = o_lx + d_lru

    def head_norm(t, g_ref):
        ms = _dot((t * t).astype(BF16), headsum_ref[...])
        return t * lax.rsqrt(ms + EPS) * g_ref[...]

    q = head_norm(_dot(h, w_in_ref[0, :, o_q:o_q + d_attn]), qg_ref)
    qT = (q * (LOG2_E / math.sqrt(HEAD_DIM))).T.astype(BF16)
    for c in range(n_blk):
        qT_ref[0, c] = qT[:, c * MOBA_BLOCK:(c + 1) * MOBA_BLOCK]

    k = head_norm(_dot(h, w_in_ref[0, :, o_k:o_k + d_attn]), kg_ref)
    k_ref[0] = k.astype(BF16)
    for c in range(n_blk):
        kmean_ref[0, c] = jnp.mean(k[c * MOBA_BLOCK:(c + 1) * MOBA_BLOCK], axis=0, keepdims=True)

    vT = _dot(h, w_in_ref[0, :, o_v:o_v + d_attn]).T.astype(BF16)
    for c in range(n_blk):
        vT_ref[0, c] = vT[:, c * MOBA_BLOCK:(c + 1) * MOBA_BLOCK]

    sc_b = _dot(h, w_in_ref[0, :, o_b:o_b + d_conv])
    cu = _dot(h, w_in_ref[0, :, o_c:o_c + d_conv]) * _dot(h, w_in_ref[0, :, o_u:o_u + d_conv])
    cu_buf[HALO:HALO + tm, :] = cu
    scw = scw_ref[...]
    conv = (scw[0:1] * cu_buf[HALO - 2:HALO - 2 + tm, :]
            + scw[1:2] * cu_buf[HALO - 1:HALO - 1 + tm, :]
            + scw[2:3] * cu)
    cu_buf[0:HALO, :] = cu[tm - HALO:tm]
    y_conv = sc_b * conv

    lx = _dot(h, w_in_ref[0, :, o_lx:o_lx + d_lru])
    lx_buf[HALO:HALO + tm, :] = lx
    lcw = lcw_ref[...]
    xr = (lcw[0:1] * lx_buf[HALO - 3:HALO - 3 + tm, :]
          + lcw[1:2] * lx_buf[HALO - 2:HALO - 2 + tm, :]
          + lcw[2:3] * lx_buf[HALO - 1:HALO - 1 + tm, :]
          + lcw[3:4] * lx) + lcb_ref[...]
    lx_buf[0:HALO, :] = lx[tm - HALO:tm]
    xr_b = xr.astype(BF16)
    r = jax.nn.sigmoid(_dot(xr_b, wa_ref[...]) + ba_ref[...])
    i = jax.nn.sigmoid(_dot(xr_b, wx_ref[...]) + bx_ref[...])
    log_a = (-LRU_C) * r * _softplus(-lam_ref[...])
    a = jnp.exp(log_a)
    t = jnp.tanh(log_a)
    u = jnp.sqrt((-2.0 * t) / (1.0 - t)) * (i * xr)
    a_cum, h_loc = _scan_linear_recurrence(a, u)
    hs = h_loc + a_cum * h_carry[...]
    h_carry[...] = hs[tm - 1:tm]
    y_lru = hs * _gelu_tanh(_dot(h, w_in_ref[0, :, o_lg:o_lg + d_lru]))

    mng = mng_ref[...]
    ycl_ref[0, :, 0:d_conv] = (y_conv * _rms(y_conv) * mng[:, 0:d_conv]).astype(BF16)
    ycl_ref[0, :, d_conv:d_conv + d_lru] = (y_lru * _rms(y_lru) * mng[:, d_conv:]).astype(BF16)


def _mixer_in(layer, x, mod, ln1_g, w_in_b, qg, kg, headsum, sc_w, lcw, lcb, wa_bd, ba, wx_bd, bx, lam, mng_cl,
              *, d_attn, d_conv, d_lru):
    batch, seq, d_model = x.shape
    tm = TOKEN_TILE
    n_blk = tm // MOBA_BLOCK
    nb = seq // MOBA_BLOCK

    def const(shape):
        return pl.BlockSpec(shape, lambda b, s: (0,) * len(shape))

    kern = functools.partial(_mixer_in_kernel, d_attn=d_attn, d_conv=d_conv, d_lru=d_lru)
    return pl.pallas_call(
        kern,
        grid=(batch, seq // tm),
        in_specs=[
            pl.BlockSpec((1, tm, d_model), lambda b, s: (b, s, 0)),
            pl.BlockSpec((1, 1, N_MOD, d_model), lambda b, s: (layer, b, 0, 0)),
            const((1, d_model)),
            pl.BlockSpec((1,) + w_in_b.shape[1:], lambda b, s: (layer, 0, 0)),
            const((1, d_attn)), const((1, d_attn)), const((d_attn, d_attn)),
            const(sc_w.shape), const(lcw.shape), const((1, d_lru)),
            const((d_lru, d_lru)), const((1, d_lru)), const((d_lru, d_lru)), const((1, d_lru)),
            const((1, d_lru)), const((1, d_conv + d_lru)),
        ],
        out_specs=[
            pl.BlockSpec((1, n_blk, d_attn, MOBA_BLOCK), lambda b, s: (b, s, 0, 0)),
            pl.BlockSpec((1, tm, d_attn), lambda b, s: (b, s, 0)),
            pl.BlockSpec((1, n_blk, d_attn, MOBA_BLOCK), lambda b, s: (b, s, 0, 0)),
            pl.BlockSpec((1, n_blk, 1, d_attn), lambda b, s: (b, s, 0, 0)),
            pl.BlockSpec((1, tm, d_conv + d_lru), lambda b, s: (b, s, 0)),
        ],
        out_shape=[
            jax.ShapeDtypeStruct((batch, nb, d_attn, MOBA_BLOCK), BF16),
            jax.ShapeDtypeStruct((batch, seq, d_attn), BF16),
            jax.ShapeDtypeStruct((batch, nb, d_attn, MOBA_BLOCK), BF16),
            jax.ShapeDtypeStruct((batch, nb, 1, d_attn), F32),
            jax.ShapeDtypeStruct((batch, seq, d_conv + d_lru), BF16),
        ],
        scratch_shapes=[
            pltpu.VMEM((HALO + tm, d_conv), F32),
            pltpu.VMEM((HALO + tm, d_lru), F32),
            pltpu.VMEM((1, d_lru), F32),
        ],
        compiler_params=pltpu.CompilerParams(
            dimension_semantics=("arbitrary", "arbitrary"), vmem_limit_bytes=VMEM_LIMIT_BYTES),
        name="mixer_in",
    )(x, mod, ln1_g, w_in_b, qg, kg, headsum, sc_w, lcw, lcb, wa_bd, ba, wx_bd, bx, lam, mng_cl)


def _sublane_fold(x, op):
    tiles = [x[r:r + 8] for r in range(0, x.shape[0], 8)]
    while len(tiles) > 1:
        tiles = [op(a, b) for a, b in zip(tiles[0::2], tiles[1::2])] + tiles[len(tiles) & ~1:]
    return tiles[0]


def _moba_kernel(qT_ref, k_ref, vT_ref, kmean_ref, g_ref, o_ref,
                 qcat_ref, bias_ref, s_ref, mx_ref, m_ref, acc_ref):
    g = pl.program_id(1)
    blk = MOBA_BLOCK
    qw = QUERY_BLOCKS * blk
    d_attn = qT_ref.shape[2]
    n_heads = d_attn // HEAD_DIM
    n_pairs = n_heads // 2
    nb = kmean_ref.shape[1]
    pair = 2 * HEAD_DIM
    cols = n_heads * qw
    first_blk = QUERY_BLOCKS * g

    half = lax.broadcasted_iota(jnp.int32, (pair, qw), 0) < HEAD_DIM
    kmean = kmean_ref[0, :, 0, :]

    def query_block_of(shape, axis):
        return (lax.broadcasted_iota(jnp.int32, shape, axis) & (qw - 1)) // blk

    past_q = lax.broadcasted_iota(jnp.int32, (nb, qw), 0) < first_blk + query_block_of((nb, qw), 1)
    for p in range(n_pairs):
        p0 = p * pair
        q_pair = jnp.concatenate([qT_ref[0, c, p0:p0 + pair, :] for c in range(QUERY_BLOCKS)], axis=1)
        zero = jnp.zeros_like(q_pair)
        for hh in range(2):
            hd = 2 * p + hh
            q_m = jnp.where(half if hh == 0 else jnp.logical_not(half), q_pair, zero)
            qcat_ref[p, :, hh * qw:(hh + 1) * qw] = q_m
            gate = _dot(kmean[:, p0:p0 + pair].astype(BF16), q_m)
            bias_ref[0:nb, hd * qw:(hd + 1) * qw] = jnp.where(past_q, gate, -jnp.inf)

    ones_rows = jnp.where(lax.broadcasted_iota(jnp.int32, (ACC_ROWS - HEAD_DIM, blk), 0) == 0,
                          1.0, 0.0).astype(BF16)
    key_pos = lax.broadcasted_iota(jnp.int32, (blk, qw), 0)
    qry_pos = lax.broadcasted_iota(jnp.int32, (blk, qw), 1)

    def score_parts(slot, key_blk, bias_row, diag):
        rows = pl.ds(pl.multiple_of(key_blk * blk, blk), blk)
        if diag is not None:
            visible = key_pos + diag * blk <= qry_pos

        def head(hd):
            p, hh = divmod(hd, 2)
            c0 = hd * qw
            s_t = _dot(k_ref[0, rows, p * pair:(p + 1) * pair], qcat_ref[p, :, hh * qw:(hh + 1) * qw])
            if diag is not None:
                s_t = jnp.where(visible, s_t, -jnp.inf)
            s_ref[slot, :, c0:c0 + qw] = s_t
            if bias_row is None:
                mx_ref[:, c0:c0 + qw] = _sublane_fold(s_t, jnp.maximum)
                return
            mx8 = jnp.maximum(mx_ref[:, c0:c0 + qw],
                              _sublane_fold(s_t, jnp.maximum) + bias_ref[pl.ds(bias_row, 1), c0:c0 + qw])
            mx_ref[:, c0:c0 + qw] = mx8
            m_ref[slot, :, c0:c0 + qw] = jnp.maximum(jnp.max(mx8, axis=0, keepdims=True), M_FLOOR)

        return [functools.partial(head, hd) for hd in range(n_heads)]

    def value_parts(slot, m_prev, key_blk, bias_row, first=False):
        m_cur = m_ref[slot]
        alpha = jnp.exp2(m_prev - m_cur)
        m_eff = m_cur - bias_ref[pl.ds(bias_row, 1), :]

        def head(hd):
            r0 = hd * HEAD_DIM
            c0 = hd * qw
            p_t = jnp.exp2(s_ref[slot, :, c0:c0 + qw] - m_eff[:, c0:c0 + qw])
            v_t = jnp.concatenate([vT_ref[0, key_blk, r0:r0 + HEAD_DIM, :], ones_rows], axis=0)
            a0 = hd * ACC_ROWS
            update = _dot(v_t, p_t.astype(BF16))
            if not first:
                update = alpha[:, c0:c0 + qw] * acc_ref[a0:a0 + ACC_ROWS, :] + update
            acc_ref[a0:a0 + ACC_ROWS, :] = update

        return [functools.partial(head, hd) for hd in range(n_heads)]

    def step_blocks(t):
        diagonal = t < QUERY_BLOCKS
        return jnp.where(diagonal, first_blk + t, t - QUERY_BLOCKS), jnp.where(diagonal, nb + t, t - QUERY_BLOCKS)

    def interleave(scores, values):
        for score, value in zip(scores, values):
            score()
            value()

    for score in score_parts(0, first_blk, None, 0):
        score()

    gate = bias_ref[0:nb, :]
    blk_i = lax.broadcasted_iota(jnp.int32, (nb, cols), 0)
    blk_f = blk_i.astype(F32)
    keep = jnp.zeros((nb, cols), F32)
    for _ in range(MOBA_TOPK):
        top = jnp.max(gate, axis=0, keepdims=True)
        first = jnp.min(jnp.where(gate == top, blk_f, float(nb)), axis=0, keepdims=True)
        hit = blk_f == first
        keep = jnp.where(hit, 1.0, keep)
        gate = jnp.where(hit, -jnp.inf, gate)
    keep = jnp.where(blk_i < first_blk + query_block_of((nb, cols), 1), keep, 0.0)
    bias_ref[0:nb, :] = jnp.where(keep > 0.0, 0.0, -jnp.inf)
    col_c = query_block_of((1, cols), 1)
    for c in range(QUERY_BLOCKS):
        bias_ref[nb + c:nb + c + 1, :] = jnp.where(col_c > c, bias_ref[pl.ds(first_blk + c, 1), :], 0.0)

    mx8 = mx_ref[...] + bias_ref[nb:nb + 1, :]
    mx_ref[...] = mx8
    m_first = jnp.maximum(jnp.max(mx8, axis=0, keepdims=True), M_FLOOR)
    m_ref[0] = m_first
    m_ref[1] = m_first

    for t in range(1, QUERY_BLOCKS):
        cur = t & 1
        m_prev = m_ref[cur]
        interleave(score_parts(cur, first_blk + t, nb + t, t),
                   value_parts(1 - cur, m_prev, first_blk + t - 1, nb + t - 1, first=t == 1))

    def fused_step(t, cur):
        m_prev = m_ref[cur]
        interleave(score_parts(cur, t - QUERY_BLOCKS, t - QUERY_BLOCKS, None),
                   value_parts(1 - cur, m_prev, *step_blocks(t - 1)))

    def loop_body(t, carry):
        for parity in range(2):
            pl.when((t & 1) == parity)(functools.partial(fused_step, t, parity))
        return carry

    n_steps = first_blk + QUERY_BLOCKS
    lax.fori_loop(QUERY_BLOCKS, n_steps, loop_body, 0)
    for parity in range(2):
        @pl.when(((n_steps - 1) & 1) == parity)
        def _(parity=parity):
            for value in value_parts(parity, m_ref[1 - parity], *step_blocks(n_steps - 1)):
                value()

    y_t = jnp.concatenate(
        [acc_ref[hd * ACC_ROWS:hd * ACC_ROWS + HEAD_DIM, :]
         / acc_ref[hd * ACC_ROWS + HEAD_DIM:hd * ACC_ROWS + HEAD_DIM + 1, :]
         for hd in range(n_heads)], axis=0)
    y_t = y_t * _rms(y_t, axis=0)
    o_ref[0] = (y_t.T * g_ref[...]).astype(BF16)


def _moba(qT, k, vT, kmean, mng_attn):
    batch, nb, d_attn, blk = qT.shape
    seq = k.shape[1]
    n_heads = d_attn // HEAD_DIM
    qw = QUERY_BLOCKS * blk
    cols = n_heads * qw
    assert QUERY_BLOCKS >= 2 and nb % QUERY_BLOCKS == 0

    return pl.pallas_call(
        _moba_kernel,
        grid=(batch, nb // QUERY_BLOCKS),
        in_specs=[
            pl.BlockSpec((1, QUERY_BLOCKS, d_attn, blk), lambda b, g: (b, g, 0, 0)),
            pl.BlockSpec((1, seq, d_attn), lambda b, g: (b, 0, 0)),
            pl.BlockSpec((1, nb, d_attn, blk), lambda b, g: (b, 0, 0, 0)),
            pl.BlockSpec((1, nb, 1, d_attn), lambda b, g: (b, 0, 0, 0)),
            pl.BlockSpec((1, d_attn), lambda b, g: (0, 0)),
        ],
        out_specs=pl.BlockSpec((1, qw, d_attn), lambda b, g: (b, g, 0)),
        out_shape=jax.ShapeDtypeStruct((batch, seq, d_attn), BF16),
        scratch_shapes=[
            pltpu.VMEM((n_heads // 2, 2 * HEAD_DIM, 2 * qw), BF16),
            pltpu.VMEM((nb + 8, cols), F32),
            pltpu.VMEM((2, blk, cols), F32),
            pltpu.VMEM((8, cols), F32),
            pltpu.VMEM((2, 1, cols), F32),
            pltpu.VMEM((n_heads * ACC_ROWS, qw), F32),
        ],
        compiler_params=pltpu.CompilerParams(
            dimension_semantics=("arbitrary", "arbitrary"), vmem_limit_bytes=VMEM_LIMIT_BYTES),
        name="moba_attention",
    )(qT, k, vT, kmean, mng_attn)


def _out_mlp_kernel(x_ref, ya_ref, ycl_ref, mod_ref, ln2_ref, wout_ref, wup_ref, wdown_ref, o_ref):
    d_attn = ya_ref.shape[2]
    d_ff = wup_ref.shape[2]
    x = x_ref[0]
    m = mod_ref[0, 0]
    gate1, shift2, scale2, gate2 = m[2:3], m[3:4], m[4:5], m[5:6]
    mix = _dot(ya_ref[0], wout_ref[0, 0:d_attn, :]) + _dot(ycl_ref[0], wout_ref[0, d_attn:, :])
    x1 = x + gate1 * mix
    h2 = ((x1 * _rms(x1) * ln2_ref[...]) * (1.0 + scale2) + shift2).astype(BF16)
    ff = jnp.zeros_like(x1)
    for c0 in range(0, d_ff, FF_CHUNK):
        up = _dot(h2, wup_ref[0, :, c0:c0 + FF_CHUNK])
        act = jnp.square(jnp.maximum(up, 0.0)).astype(BF16)
        ff = ff + _dot(act, wdown_ref[0, c0:c0 + FF_CHUNK, :])
    o_ref[0] = x1 + gate2 * ff


def _out_mlp(layer, x, ya, ycl, mod, ln2_g, w_out_b, w_up_b, w_down_b):
    batch, seq, d_model = x.shape
    tm = MLP_TOKEN_TILE
    d_attn = ya.shape[2]
    d_cl = ycl.shape[2]

    def layer_weight(w):
        return pl.BlockSpec((1,) + w.shape[1:], lambda b, s: (layer, 0, 0), pipeline_mode=pl.Buffered(1))

    return pl.pallas_call(
        _out_mlp_kernel,
        grid=(batch, seq // tm),
        in_specs=[
            pl.BlockSpec((1, tm, d_model), lambda b, s: (b, s, 0)),
            pl.BlockSpec((1, tm, d_attn), lambda b, s: (b, s, 0)),
            pl.BlockSpec((1, tm, d_cl), lambda b, s: (b, s, 0)),
            pl.BlockSpec((1, 1, N_MOD, d_model), lambda b, s: (layer, b, 0, 0)),
            pl.BlockSpec((1, d_model), lambda b, s: (0, 0)),
            layer_weight(w_out_b), layer_weight(w_up_b), layer_weight(w_down_b),
        ],
        out_specs=pl.BlockSpec((1, tm, d_model), lambda b, s: (b, s, 0)),
        out_shape=jax.ShapeDtypeStruct((batch, seq, d_model), F32),
        compiler_params=pltpu.CompilerParams(
            dimension_semantics=("arbitrary", "arbitrary"), vmem_limit_bytes=VMEM_LIMIT_BYTES),
        name="out_mlp",
    )(x, ya, ycl, mod, ln2_g, w_out_b, w_up_b, w_down_b)


def _block_diag(w):
    n, r, c = w.shape
    eye = jnp.eye(n, dtype=w.dtype)
    return (eye[:, None, :, None] * w[:, :, None, :]).reshape(n * r, n * c)


def kernel(x, c, ln1_g, ln2_g, w_ada, b_ada, w_in, q_norm_g, k_norm_g, sc_w, lru_conv_w, lru_conv_b,
           lru_wa, lru_ba, lru_wx, lru_bx, lru_lambda, mix_norm_g, w_out, w_up, w_down):
    batch, seq, d_model = x.shape
    depth = w_in.shape[0]
    d_conv = sc_w.shape[2]
    d_lru = lru_conv_w.shape[2]
    d_attn = mix_norm_g.shape[1] - d_conv - d_lru
    n_heads = d_attn // HEAD_DIM
    assert seq % TOKEN_TILE == 0 and TOKEN_TILE % MOBA_BLOCK == 0 and seq % MLP_TOKEN_TILE == 0
    assert w_in.shape[2] == 3 * d_attn + 3 * d_conv + 2 * d_lru

    mod = _modulation(c, w_ada, b_ada).reshape(depth, batch, N_MOD, d_model)
    headsum = _block_diag(jnp.full((n_heads, HEAD_DIM, HEAD_DIM), 1.0 / HEAD_DIM, F32)).astype(BF16)

    w_in_b, w_out_b, w_up_b, w_down_b = (w.astype(BF16) for w in (w_in, w_out, w_up, w_down))
    for l in range(depth):
        row = lambda v: v.reshape(1, -1)
        qT, k, vT, kmean, ycl = _mixer_in(
            l, x, mod, row(ln1_g[l]), w_in_b,
            row(jnp.tile(q_norm_g[l], n_heads)), row(jnp.tile(k_norm_g[l], n_heads)), headsum,
            sc_w[l], lru_conv_w[l], row(lru_conv_b[l]),
            _block_diag(lru_wa[l]).astype(BF16), row(lru_ba[l]),
            _block_diag(lru_wx[l]).astype(BF16), row(lru_bx[l]),
            row(lru_lambda[l]), row(mix_norm_g[l, d_attn:]),
            d_attn=d_attn, d_conv=d_conv, d_lru=d_lru)
        ya = _moba(qT, k, vT, kmean, row(mix_norm_g[l, :d_attn]))
        x = _out_mlp(l, x, ya, ycl, mod, row(ln2_g[l]), w_out_b, w_up_b, w_down_b)
    return x
```

```python
import functools
import math

import jax
import jax.numpy as jnp
from jax import lax
from jax.experimental import pallas as pl
from jax.experimental.pallas import tpu as pltpu

F32 = jnp.float32
BF16 = jnp.bfloat16

HEAD_DIM = 64
MOBA_BLOCK = 256
MOBA_TOPK = 3
LRU_C = 8.0
N_MOD = 6
EPS = 1e-6

TOKEN_TILE = 512
MLP_TOKEN_TILE = 1024
MOD_COL_TILE = 1536
FF_CHUNK = 1024
HALO = 8
ACC_ROWS = HEAD_DIM + 16
LOG2_E = math.log2(math.e)
QUERY_BLOCKS = 2
M_FLOOR = -1e30
VMEM_LIMIT_BYTES = 56 * 1024 * 1024


def _rms(x, axis=-1):
    return lax.rsqrt(jnp.mean(x * x, axis=axis, keepdims=True) + EPS)


def _dot(a, b):
    return jnp.dot(a, b, preferred_element_type=F32)


def _mod_kernel(c_ref, w_ref, b_ref, o_ref):
    c = c_ref[...]
    c_act = (c * jax.nn.sigmoid(c)).astype(BF16)
    o_ref[0] = _dot(c_act, w_ref[0].astype(BF16)) + b_ref[0]


def _modulation(c, w_ada, b_ada):
    depth, d_model, n_out = w_ada.shape
    batch = c.shape[0]
    return pl.pallas_call(
        _mod_kernel,
        grid=(depth, n_out // MOD_COL_TILE),
        in_specs=[
            pl.BlockSpec((batch, d_model), lambda l, j: (0, 0)),
            pl.BlockSpec((1, d_model, MOD_COL_TILE), lambda l, j: (l, 0, j)),
            pl.BlockSpec((1, 1, MOD_COL_TILE), lambda l, j: (l, 0, j)),
        ],
        out_specs=pl.BlockSpec((1, batch, MOD_COL_TILE), lambda l, j: (l, 0, j)),
        out_shape=jax.ShapeDtypeStruct((depth, batch, n_out), F32),
        compiler_params=pltpu.CompilerParams(
            dimension_semantics=("arbitrary", "arbitrary"), vmem_limit_bytes=VMEM_LIMIT_BYTES),
        name="adaln_modulation",
    )(c, w_ada, b_ada.reshape(depth, 1, n_out))


def _scan_linear_recurrence(a, u):
    n = a.shape[0]
    row = lax.broadcasted_iota(jnp.int32, a.shape, 0)
    d = 1
    while d < n:
        keep = row >= d
        a_prev = jnp.where(keep, pltpu.roll(a, d, 0), 1.0)
        u_prev = jnp.where(keep, pltpu.roll(u, d, 0), 0.0)
        u = a * u_prev + u
        a = a * a_prev
        d *= 2
    return a, u


def _gelu_tanh(x):
    return 0.5 * x * (1.0 + jnp.tanh(math.sqrt(2.0 / math.pi) * (x + 0.044715 * (x * x * x))))


def _softplus(z):
    return jnp.maximum(z, 0.0) + jnp.log1p(jnp.exp(-jnp.abs(z)))


def _mixer_in_kernel(x_ref, mod_ref, ln1_ref, w_in_ref, qg_ref, kg_ref, headsum_ref, scw_ref,
                     lcw_ref, lcb_ref, wa_ref, ba_ref, wx_ref, bx_ref, lam_ref, mng_ref,
                     qT_ref, k_ref, vT_ref, kmean_ref, ycl_ref,
                     cu_buf, lx_buf, h_carry, *, d_attn, d_conv, d_lru):
    s = pl.program_id(1)
    tm = x_ref.shape[1]
    n_blk = tm // MOBA_BLOCK

    @pl.when(s == 0)
    def _():
        cu_buf[0:HALO, :] = jnp.zeros((HALO, d_conv), F32)
        lx_buf[0:HALO, :] = jnp.zeros((HALO, d_lru), F32)
        h_carry[...] = jnp.zeros_like(h_carry)

    x = x_ref[0]
    m = mod_ref[0, 0]
    shift1, scale1 = m[0:1], m[1:2]
    h = ((x * _rms(x) * ln1_ref[...]) * (1.0 + scale1) + shift1).astype(BF16)

    o_q, o_k, o_v = 0, d_attn, 2 * d_attn
    o_b = 3 * d_attn
    o_c, o_u = o_b + d_conv, o_b + 2 * d_conv
    o_lx = o_b + 3 * d_conv
    o_lg = o_lx + d_lru

    def head_norm(t, g_ref):
        ms = _dot((t * t).astype(BF16), headsum_ref[...])
        return t * lax.rsqrt(ms + EPS) * g_ref[...]

    q = head_norm(_dot(h, w_in_ref[0, :, o_q:o_q + d_attn]), qg_ref)
    qT = (q * (LOG2_E / math.sqrt(HEAD_DIM))).T.astype(BF16)
    for c in range(n_blk):
        qT_ref[0, c] = qT[:, c * MOBA_BLOCK:(c + 1) * MOBA_BLOCK]

    k = head_norm(_dot(h, w_in_ref[0, :, o_k:o_k + d_attn]), kg_ref)
    k_ref[0] = k.astype(BF16)
    for c in range(n_blk):
        kmean_ref[0, c] = jnp.mean(k[c * MOBA_BLOCK:(c + 1) * MOBA_BLOCK], axis=0, keepdims=True)

    vT = _dot(h, w_in_ref[0, :, o_v:o_v + d_attn]).T.astype(BF16)
    for c in range(n_blk):
        vT_ref[0, c] = vT[:, c * MOBA_BLOCK:(c + 1) * MOBA_BLOCK]

    sc_b = _dot(h, w_in_ref[0, :, o_b:o_b + d_conv])
    cu = _dot(h, w_in_ref[0, :, o_c:o_c + d_conv]) * _dot(h, w_in_ref[0, :, o_u:o_u + d_conv])
    cu_buf[HALO:HALO + tm, :] = cu
    scw = scw_ref[...]
    conv = (scw[0:1] * cu_buf[HALO - 2:HALO - 2 + tm, :]
            + scw[1:2] * cu_buf[HALO - 1:HALO - 1 + tm, :]
            + scw[2:3] * cu)
    cu_buf[0:HALO, :] = cu[tm - HALO:tm]
    y_conv = sc_b * conv

    lx = _dot(h, w_in_ref[0, :, o_lx:o_lx + d_lru])
    lx_buf[HALO:HALO + tm, :] = lx
    lcw = lcw_ref[...]
    xr = (lcw[0:1] * lx_buf[HALO - 3:HALO - 3 + tm, :]
          + lcw[1:2] * lx_buf[HALO - 2:HALO - 2 + tm, :]
          + lcw[2:3] * lx_buf[HALO - 1:HALO - 1 + tm, :]
          + lcw[3:4] * lx) + lcb_ref[...]
    lx_buf[0:HALO, :] = lx[tm - HALO:tm]
    xr_b = xr.astype(BF16)
    r = jax.nn.sigmoid(_dot(xr_b, wa_ref[...]) + ba_ref[...])
    i = jax.nn.sigmoid(_dot(xr_b, wx_ref[...]) + bx_ref[...])
    log_a = (-LRU_C) * r * _softplus(-lam_ref[...])
    a = jnp.exp(log_a)
    t = jnp.tanh(log_a)
    u = jnp.sqrt((-2.0 * t) / (1.0 - t)) * (i * xr)
    a_cum, h_loc = _scan_linear_recurrence(a, u)
    hs = h_loc + a_cum * h_carry[...]
    h_carry[...] = hs[tm - 1:tm]
    y_lru = hs * _gelu_tanh(_dot(h, w_in_ref[0, :, o_lg:o_lg + d_lru]))

    mng = mng_ref[...]
    ycl_ref[0, :, 0:d_conv] = (y_conv * _rms(y_conv) * mng[:, 0:d_conv]).astype(BF16)
    ycl_ref[0, :, d_conv:d_conv + d_lru] = (y_lru * _rms(y_lru) * mng[:, d_conv:]).astype(BF16)


def _mixer_in(layer, x, mod, ln1_g, w_in_b, qg, kg, headsum, sc_w, lcw, lcb, wa_bd, ba, wx_bd, bx, lam, mng_cl,
              *, d_attn, d_conv, d_lru):
    batch, seq, d_model = x.shape
    tm = TOKEN_TILE
    n_blk = tm // MOBA_BLOCK
    nb = seq // MOBA_BLOCK

    def const(shape):
        return pl.BlockSpec(shape, lambda b, s: (0,) * len(shape))

    kern = functools.partial(_mixer_in_kernel, d_attn=d_attn, d_conv=d_conv, d_lru=d_lru)
    return pl.pallas_call(
        kern,
        grid=(batch, seq // tm),
        in_specs=[
            pl.BlockSpec((1, tm, d_model), lambda b, s: (b, s, 0)),
            pl.BlockSpec((1, 1, N_MOD, d_model), lambda b, s: (layer, b, 0, 0)),
            const((1, d_model)),
            pl.BlockSpec((1,) + w_in_b.shape[1:], lambda b, s: (layer, 0, 0)),
            const((1, d_attn)), const((1, d_attn)), const((d_attn, d_attn)),
            const(sc_w.shape), const(lcw.shape), const((1, d_lru)),
            const((d_lru, d_lru)), const((1, d_lru)), const((d_lru, d_lru)), const((1, d_lru)),
            const((1, d_lru)), const((1, d_conv + d_lru)),
        ],
        out_specs=[
            pl.BlockSpec((1, n_blk, d_attn, MOBA_BLOCK), lambda b, s: (b, s, 0, 0)),
            pl.BlockSpec((1, tm, d_attn), lambda b, s: (b, s, 0)),
            pl.BlockSpec((1, n_blk, d_attn, MOBA_BLOCK), lambda b, s: (b, s, 0, 0)),
            pl.BlockSpec((1, n_blk, 1, d_attn), lambda b, s: (b, s, 0, 0)),
            pl.BlockSpec((1, tm, d_conv + d_lru), lambda b, s: (b, s, 0)),
        ],
        out_shape=[
            jax.ShapeDtypeStruct((batch, nb, d_attn, MOBA_BLOCK), BF16),
            jax.ShapeDtypeStruct((batch, seq, d_attn), BF16),
            jax.ShapeDtypeStruct((batch, nb, d_attn, MOBA_BLOCK), BF16),
            jax.ShapeDtypeStruct((batch, nb, 1, d_attn), F32),
            jax.ShapeDtypeStruct((batch, seq, d_conv + d_lru), BF16),
        ],
        scratch_shapes=[
            pltpu.VMEM((HALO + tm, d_conv), F32),
            pltpu.VMEM((HALO + tm, d_lru), F32),
            pltpu.VMEM((1, d_lru), F32),
        ],
        compiler_params=pltpu.CompilerParams(
            dimension_semantics=("arbitrary", "arbitrary"), vmem_limit_bytes=VMEM_LIMIT_BYTES),
        name="mixer_in",
    )(x, mod, ln1_g, w_in_b, qg, kg, headsum, sc_w, lcw, lcb, wa_bd, ba, wx_bd, bx, lam, mng_cl)


def _sublane_fold(x, op):
    tiles = [x[r:r + 8] for r in range(0, x.shape[0], 8)]
    while len(tiles) > 1:
        tiles = [op(a, b) for a, b in zip(tiles[0::2], tiles[1::2])] + tiles[len(tiles) & ~1:]
    return tiles[0]


def _moba_kernel(qT_ref, k_ref, vT_ref, kmean_ref, g_ref, o_ref,
                 qcat_ref, bias_ref, s_ref, mx_ref, m_ref, acc_ref):
    g = pl.program_id(1)
    blk = MOBA_BLOCK
    qw = QUERY_BLOCKS * blk
    d_attn = qT_ref.shape[2]
    n_heads = d_attn // HEAD_DIM
    n_pairs = n_heads // 2
    nb = kmean_ref.shape[1]
    pair = 2 * HEAD_DIM
    cols = n_heads * qw
    first_blk = QUERY_BLOCKS * g

    half = lax.broadcasted_iota(jnp.int32, (pair, qw), 0) < HEAD_DIM
    kmean = kmean_ref[0, :, 0, :]

    def query_block_of(shape, axis):
        return (lax.broadcasted_iota(jnp.int32, shape, axis) & (qw - 1)) // blk

    past_q = lax.broadcasted_iota(jnp.int32, (nb, qw), 0) < first_blk + query_block_of((nb, qw), 1)
    for p in range(n_pairs):
        p0 = p * pair
        q_pair = jnp.concatenate([qT_ref[0, c, p0:p0 + pair, :] for c in range(QUERY_BLOCKS)], axis=1)
        zero = jnp.zeros_like(q_pair)
        for hh in range(2):
            hd = 2 * p + hh
            q_m = jnp.where(half if hh == 0 else jnp.logical_not(half), q_pair, zero)
            qcat_ref[p, :, hh * qw:(hh + 1) * qw] = q_m
            gate = _dot(kmean[:, p0:p0 + pair].astype(BF16), q_m)
            bias_ref[0:nb, hd * qw:(hd + 1) * qw] = jnp.where(past_q, gate, -jnp.inf)

    ones_rows = jnp.where(lax.broadcasted_iota(jnp.int32, (ACC_ROWS - HEAD_DIM, blk), 0) == 0,
                          1.0, 0.0).astype(BF16)
    key_pos = lax.broadcasted_iota(jnp.int32, (blk, qw), 0)
    qry_pos = lax.broadcasted_iota(jnp.int32, (blk, qw), 1)

    def score_parts(slot, key_blk, bias_row, diag):
        rows = pl.ds(pl.multiple_of(key_blk * blk, blk), blk)
        lo = 0 if diag is None else diag * blk
        if diag is not None:
            visible = (key_pos + diag * blk <= qry_pos)[:, lo:]

        def head(hd):
            p, hh = divmod(hd, 2)
            c0 = hd * qw + lo
            c1 = (hd + 1) * qw
            s_t = _dot(k_ref[0, rows, p * pair:(p + 1) * pair], qcat_ref[p, :, hh * qw + lo:(hh + 1) * qw])
            if diag is not None:
                s_t = jnp.where(visible, s_t, -jnp.inf)
            s_ref[slot, :, c0:c1] = s_t
            if bias_row is None:
                mx_ref[:, c0:c1] = _sublane_fold(s_t, jnp.maximum)
                return
            mx8 = jnp.maximum(mx_ref[:, c0:c1],
                              _sublane_fold(s_t, jnp.maximum) + bias_ref[pl.ds(bias_row, 1), c0:c1])
            mx_ref[:, c0:c1] = mx8
            m_ref[slot, :, c0:c1] = jnp.maximum(jnp.max(mx8, axis=0, keepdims=True), M_FLOOR)

        return [functools.partial(head, hd) for hd in range(n_heads)]

    def value_parts(slot, m_prev, key_blk, bias_row, lo=0):
        m_cur = m_ref[slot]
        alpha = jnp.exp2(m_prev - m_cur)
        m_eff = m_cur - bias_ref[pl.ds(bias_row, 1), :]

        def head(hd):
            r0 = hd * HEAD_DIM
            c0 = hd * qw + lo
            c1 = (hd + 1) * qw
            p_t = jnp.exp2(s_ref[slot, :, c0:c1] - m_eff[:, c0:c1])
            v_t = jnp.concatenate([vT_ref[0, key_blk, r0:r0 + HEAD_DIM, :], ones_rows], axis=0)
            a0 = hd * ACC_ROWS
            acc_ref[a0:a0 + ACC_ROWS, lo:] = (alpha[:, c0:c1] * acc_ref[a0:a0 + ACC_ROWS, lo:]
                                              + _dot(v_t, p_t.astype(BF16)))

        return [functools.partial(head, hd) for hd in range(n_heads)]

    def interleave(scores, values):
        for score, value in zip(scores, values):
            score()
            value()

    for score in score_parts(0, first_blk, None, 0):
        score()

    gate = bias_ref[0:nb, :]
    blk_i = lax.broadcasted_iota(jnp.int32, (nb, cols), 0)
    blk_f = blk_i.astype(F32)
    keep = jnp.zeros((nb, cols), F32)
    for _ in range(MOBA_TOPK):
        top = jnp.max(gate, axis=0, keepdims=True)
        first = jnp.min(jnp.where(gate == top, blk_f, float(nb)), axis=0, keepdims=True)
        hit = blk_f == first
        keep = jnp.where(hit, 1.0, keep)
        gate = jnp.where(hit, -jnp.inf, gate)
    keep = jnp.where(blk_i < first_blk + query_block_of((nb, cols), 1), keep, 0.0)
    bias_ref[0:nb, :] = jnp.where(keep > 0.0, 0.0, -jnp.inf)
    col_c = query_block_of((1, cols), 1)
    for c in range(QUERY_BLOCKS):
        bias_ref[nb + c:nb + c + 1, :] = jnp.where(col_c > c, bias_ref[pl.ds(first_blk + c, 1), :], 0.0)

    mx8 = mx_ref[...] + bias_ref[nb:nb + 1, :]
    mx_ref[...] = mx8
    m_first = jnp.maximum(jnp.max(mx8, axis=0, keepdims=True), M_FLOOR)
    m_ref[0] = m_first
    m_ref[1] = m_first

    acc_ref[...] = jnp.zeros_like(acc_ref)

    def step_blocks(t):
        return jnp.where(t == 0, first_blk, t - 1), jnp.where(t == 0, nb, t - 1)

    def fused_step(t, cur):
        m_prev = m_ref[cur]
        interleave(score_parts(cur, t - 1, t - 1, None), value_parts(1 - cur, m_prev, *step_blocks(t - 1)))

    def loop_body(t, carry):
        for parity in range(2):
            pl.when((t & 1) == parity)(functools.partial(fused_step, t, parity))
        return carry

    lax.fori_loop(1, first_blk + 1, loop_body, 0)

    for c in range(1, QUERY_BLOCKS):
        cur = c & 1
        m_prev = m_ref[cur]
        if c == 1:
            previous = value_parts(1 - cur, m_prev, *step_blocks(first_blk))
        else:
            previous = value_parts(1 - cur, m_prev, first_blk + c - 1, nb + c - 1, lo=(c - 1) * blk)
        interleave(score_parts(cur, first_blk + c, nb + c, c), previous)
    last = QUERY_BLOCKS - 1
    for value in value_parts(last & 1, m_ref[1 - (last & 1)], first_blk + last, nb + last, lo=last * blk):
        value()

    y_t = jnp.concatenate(
        [acc_ref[hd * ACC_ROWS:hd * ACC_ROWS + HEAD_DIM, :]
         / acc_ref[hd * ACC_ROWS + HEAD_DIM:hd * ACC_ROWS + HEAD_DIM + 1, :]
         for hd in range(n_heads)], axis=0)
    y_t = y_t * _rms(y_t, axis=0)
    o_ref[0] = (y_t.T * g_ref[...]).astype(BF16)


def _moba(qT, k, vT, kmean, mng_attn):
    batch, nb, d_attn, blk = qT.shape
    seq = k.shape[1]
    n_heads = d_attn // HEAD_DIM
    qw = QUERY_BLOCKS * blk
    cols = n_heads * qw
    assert QUERY_BLOCKS >= 2 and QUERY_BLOCKS % 2 == 0 and nb % QUERY_BLOCKS == 0

    return pl.pallas_call(
        _moba_kernel,
        grid=(batch, nb // QUERY_BLOCKS),
        in_specs=[
            pl.BlockSpec((1, QUERY_BLOCKS, d_attn, blk), lambda b, g: (b, g, 0, 0)),
            pl.BlockSpec((1, seq, d_attn), lambda b, g: (b, 0, 0)),
            pl.BlockSpec((1, nb, d_attn, blk), lambda b, g: (b, 0, 0, 0)),
            pl.BlockSpec((1, nb, 1, d_attn), lambda b, g: (b, 0, 0, 0)),
            pl.BlockSpec((1, d_attn), lambda b, g: (0, 0)),
        ],
        out_specs=pl.BlockSpec((1, qw, d_attn), lambda b, g: (b, g, 0)),
        out_shape=jax.ShapeDtypeStruct((batch, seq, d_attn), BF16),
        scratch_shapes=[
            pltpu.VMEM((n_heads // 2, 2 * HEAD_DIM, 2 * qw), BF16),
            pltpu.VMEM((nb + 8, cols), F32),
            pltpu.VMEM((2, blk, cols), F32),
            pltpu.VMEM((8, cols), F32),
            pltpu.VMEM((2, 1, cols), F32),
            pltpu.VMEM((n_heads * ACC_ROWS, qw), F32),
        ],
        compiler_params=pltpu.CompilerParams(
            dimension_semantics=("arbitrary", "arbitrary"), vmem_limit_bytes=VMEM_LIMIT_BYTES),
        name="moba_attention",
    )(qT, k, vT, kmean, mng_attn)


def _out_mlp_kernel(x_ref, ya_ref, ycl_ref, mod_ref, ln2_ref, wout_ref, wup_ref, wdown_ref, o_ref):
    d_attn = ya_ref.shape[2]
    d_ff = wup_ref.shape[2]
    x = x_ref[0]
    m = mod_ref[0, 0]
    gate1, shift2, scale2, gate2 = m[2:3], m[3:4], m[4:5], m[5:6]
    mix = _dot(ya_ref[0], wout_ref[0, 0:d_attn, :]) + _dot(ycl_ref[0], wout_ref[0, d_attn:, :])
    x1 = x + gate1 * mix
    h2 = ((x1 * _rms(x1) * ln2_ref[...]) * (1.0 + scale2) + shift2).astype(BF16)
    ff = jnp.zeros_like(x1)
    for c0 in range(0, d_ff, FF_CHUNK):
        up = _dot(h2, wup_ref[0, :, c0:c0 + FF_CHUNK])
        act = jnp.square(jnp.maximum(up, 0.0)).astype(BF16)
        ff = ff + _dot(act, wdown_ref[0, c0:c0 + FF_CHUNK, :])
    o_ref[0] = x1 + gate2 * ff


def _out_mlp(layer, x, ya, ycl, mod, ln2_g, w_out_b, w_up_b, w_down_b):
    batch, seq, d_model = x.shape
    tm = MLP_TOKEN_TILE
    d_attn = ya.shape[2]
    d_cl = ycl.shape[2]

    def layer_weight(w):
        return pl.BlockSpec((1,) + w.shape[1:], lambda b, s: (layer, 0, 0), pipeline_mode=pl.Buffered(1))

    return pl.pallas_call(
        _out_mlp_kernel,
        grid=(batch, seq // tm),
        in_specs=[
            pl.BlockSpec((1, tm, d_model), lambda b, s: (b, s, 0)),
            pl.BlockSpec((1, tm, d_attn), lambda b, s: (b, s, 0)),
            pl.BlockSpec((1, tm, d_cl), lambda b, s: (b, s, 0)),
            pl.BlockSpec((1, 1, N_MOD, d_model), lambda b, s: (layer, b, 0, 0)),
            pl.BlockSpec((1, d_model), lambda b, s: (0, 0)),
            layer_weight(w_out_b), layer_weight(w_up_b), layer_weight(w_down_b),
        ],
        out_specs=pl.BlockSpec((1, tm, d_model), lambda b, s: (b, s, 0)),
        out_shape=jax.ShapeDtypeStruct((batch, seq, d_model), F32),
        compiler_params=pltpu.CompilerParams(
            dimension_semantics=("arbitrary", "arbitrary"), vmem_limit_bytes=VMEM_LIMIT_BYTES),
        name="out_mlp",
    )(x, ya, ycl, mod, ln2_g, w_out_b, w_up_b, w_down_b)


def _block_diag(w):
    n, r, c = w.shape
    eye = jnp.eye(n, dtype=w.dtype)
    return (eye[:, None, :, None] * w[:, :, None, :]).reshape(n * r, n * c)


def kernel(x, c, ln1_g, ln2_g, w_ada, b_ada, w_in, q_norm_g, k_norm_g, sc_w, lru_conv_w, lru_conv_b,
           lru_wa, lru_ba, lru_wx, lru_bx, lru_lambda, mix_norm_g, w_out, w_up, w_down):
    batch, seq, d_model = x.shape
    depth = w_in.shape[0]
    d_conv = sc_w.shape[2]
    d_lru = lru_conv_w.shape[2]
    d_attn = mix_norm_g.shape[1] - d_conv - d_lru
    n_heads = d_attn // HEAD_DIM
    assert seq % TOKEN_TILE == 0 and TOKEN_TILE % MOBA_BLOCK == 0 and seq % MLP_TOKEN_TILE == 0
    assert w_in.shape[2] == 3 * d_attn + 3 * d_conv + 2 * d_lru

    mod = _modulation(c, w_ada, b_ada).reshape(depth, batch, N_MOD, d_model)
    headsum = _block_diag(jnp.full((n_heads, HEAD_DIM, HEAD_DIM), 1.0 / HEAD_DIM, F32)).astype(BF16)

    w_in_b, w_out_b, w_up_b, w_down_b = (w.astype(BF16) for w in (w_in, w_out, w_up, w_down))
    for l in range(depth):
        row = lambda v: v.reshape(1, -1)
        qT, k, vT, kmean, ycl = _mixer_in(
            l, x, mod, row(ln1_g[l]), w_in_b,
            row(jnp.tile(q_norm_g[l], n_heads)), row(jnp.tile(k_norm_g[l], n_heads)), headsum,
            sc_w[l], lru_conv_w[l], row(lru_conv_b[l]),
            _block_diag(lru_wa[l]).astype(BF16), row(lru_ba[l]),
            _block_diag(lru_wx[l]).astype(BF16), row(lru_bx[l]),
            row(lru_lambda[l]), row(mix_norm_g[l, d_attn:]),
            d_attn=d_attn, d_conv=d_conv, d_lru=d_lru)
        ya = _moba(qT, k, vT, kmean, row(mix_norm_g[l, :d_attn]))
        x = _out_mlp(l, x, ya, ycl, mod, row(ln2_g[l]), w_out_b, w_up_b, w_down_b)
    return x
```

```python
import functools
import math

import jax
import jax.numpy as jnp
from jax import lax
from jax.experimental import pallas as pl
from jax.experimental.pallas import tpu as pltpu

F32 = jnp.float32
BF16 = jnp.bfloat16

HEAD_DIM = 64
MOBA_BLOCK = 256
MOBA_TOPK = 3
LRU_C = 8.0
N_MOD = 6
EPS = 1e-6

TOKEN_TILE = 512
MLP_TOKEN_TILE = 1024
MOD_COL_TILE = 1536
FF_CHUNK = 1024
HALO = 8
ACC_ROWS = HEAD_DIM + 16
LOG2_E = math.log2(math.e)
QUERY_BLOCKS = 2
M_FLOOR = -1e30
VMEM_LIMIT_BYTES = 56 * 1024 * 1024


def _rms(x, axis=-1):
    return lax.rsqrt(jnp.mean(x * x, axis=axis, keepdims=True) + EPS)


def _dot(a, b):
    return jnp.dot(a, b, preferred_element_type=F32)


def _mod_kernel(c_ref, w_ref, b_ref, o_ref):
    c = c_ref[...]
    c_act = (c * jax.nn.sigmoid(c)).astype(BF16)
    o_ref[0] = _dot(c_act, w_ref[0].astype(BF16)) + b_ref[0]


def _modulation(c, w_ada, b_ada):
    depth, d_model, n_out = w_ada.shape
    batch = c.shape[0]
    return pl.pallas_call(
        _mod_kernel,
        grid=(depth, n_out // MOD_COL_TILE),
        in_specs=[
            pl.BlockSpec((batch, d_model), lambda l, j: (0, 0)),
            pl.BlockSpec((1, d_model, MOD_COL_TILE), lambda l, j: (l, 0, j)),
            pl.BlockSpec((1, 1, MOD_COL_TILE), lambda l, j: (l, 0, j)),
        ],
        out_specs=pl.BlockSpec((1, batch, MOD_COL_TILE), lambda l, j: (l, 0, j)),
        out_shape=jax.ShapeDtypeStruct((depth, batch, n_out), F32),
        compiler_params=pltpu.CompilerParams(
            dimension_semantics=("arbitrary", "arbitrary"), vmem_limit_bytes=VMEM_LIMIT_BYTES),
        name="adaln_modulation",
    )(c, w_ada, b_ada.reshape(depth, 1, n_out))


def _scan_linear_recurrence(a, u):
    n = a.shape[0]
    row = lax.broadcasted_iota(jnp.int32, a.shape, 0)
    d = 1
    while d < n:
        keep = row >= d
        a_prev = jnp.where(keep, pltpu.roll(a, d, 0), 1.0)
        u_prev = jnp.where(keep, pltpu.roll(u, d, 0), 0.0)
        u = a * u_prev + u
        a = a * a_prev
        d *= 2
    return a, u


def _gelu_tanh(x):
    return 0.5 * x * (1.0 + jnp.tanh(math.sqrt(2.0 / math.pi) * (x + 0.044715 * (x * x * x))))


def _softplus(z):
    return jnp.maximum(z, 0.0) + jnp.log1p(jnp.exp(-jnp.abs(z)))


def _mixer_in_kernel(x_ref, mod_ref, ln1_ref, w_in_ref, qg_ref, kg_ref, headsum_ref, scw_ref,
                     lcw_ref, lcb_ref, wa_ref, ba_ref, wx_ref, bx_ref, lam_ref, mng_ref,
                     qT_ref, k_ref, vT_ref, kmean_ref, ycl_ref,
                     cu_buf, lx_buf, h_carry, *, d_attn, d_conv, d_lru):
    s = pl.program_id(1)
    tm = x_ref.shape[1]
    n_blk = tm // MOBA_BLOCK

    @pl.when(s == 0)
    def _():
        cu_buf[0:HALO, :] = jnp.zeros((HALO, d_conv), F32)
        lx_buf[0:HALO, :] = jnp.zeros((HALO, d_lru), F32)
        h_carry[...] = jnp.zeros_like(h_carry)

    x = x_ref[0]
    m = mod_ref[0, 0]
    shift1, scale1 = m[0:1], m[1:2]
    h = ((x * _rms(x) * ln1_ref[...]) * (1.0 + scale1) + shift1).astype(BF16)

    o_q, o_k, o_v = 0, d_attn, 2 * d_attn
    o_b = 3 * d_attn
    o_c, o_u = o_b + d_conv, o_b + 2 * d_conv
    o_lx = o_b + 3 * d_conv
    o_lg = o_lx + d_lru

    def head_norm(t, g_ref):
        ms = _dot((t * t).astype(BF16), headsum_ref[...])
        return t * lax.rsqrt(ms + EPS) * g_ref[...]

    q = head_norm(_dot(h, w_in_ref[0, :, o_q:o_q + d_attn]), qg_ref)
    qT = (q * (LOG2_E / math.sqrt(HEAD_DIM))).T.astype(BF16)
    for c in range(n_blk):
        qT_ref[0, c] = qT[:, c * MOBA_BLOCK:(c + 1) * MOBA_BLOCK]

    k = head_norm(_dot(h, w_in_ref[0, :, o_k:o_k + d_attn]), kg_ref)
    k_ref[0] = k.astype(BF16)
    for c in range(n_blk):
        kmean_ref[0, c] = jnp.mean(k[c * MOBA_BLOCK:(c + 1) * MOBA_BLOCK], axis=0, keepdims=True)

    vT = _dot(h, w_in_ref[0, :, o_v:o_v + d_attn]).T.astype(BF16)
    for c in range(n_blk):
        vT_ref[0, c] = vT[:, c * MOBA_BLOCK:(c + 1) * MOBA_BLOCK]

    sc_b = _dot(h, w_in_ref[0, :, o_b:o_b + d_conv])
    cu = _dot(h, w_in_ref[0, :, o_c:o_c + d_conv]) * _dot(h, w_in_ref[0, :, o_u:o_u + d_conv])
    cu_buf[HALO:HALO + tm, :] = cu
    scw = scw_ref[...]
    conv = (scw[0:1] * cu_buf[HALO - 2:HALO - 2 + tm, :]
            + scw[1:2] * cu_buf[HALO - 1:HALO - 1 + tm, :]
            + scw[2:3] * cu)
    cu_buf[0:HALO, :] = cu[tm - HALO:tm]
    y_conv = sc_b * conv

    lx = _dot(h, w_in_ref[0, :, o_lx:o_lx + d_lru])
    lx_buf[HALO:HALO + tm, :] = lx
    lcw = lcw_ref[...]
    xr = (lcw[0:1] * lx_buf[HALO - 3:HALO - 3 + tm, :]
          + lcw[1:2] * lx_buf[HALO - 2:HALO - 2 + tm, :]
          + lcw[2:3] * lx_buf[HALO - 1:HALO - 1 + tm, :]
          + lcw[3:4] * lx) + lcb_ref[...]
    lx_buf[0:HALO, :] = lx[tm - HALO:tm]
    xr_b = xr.astype(BF16)
    r = jax.nn.sigmoid(_dot(xr_b, wa_ref[...]) + ba_ref[...])
    i = jax.nn.sigmoid(_dot(xr_b, wx_ref[...]) + bx_ref[...])
    log_a = (-LRU_C) * r * _softplus(-lam_ref[...])
    a = jnp.exp(log_a)
    t = jnp.tanh(log_a)
    u = jnp.sqrt((-2.0 * t) / (1.0 - t)) * (i * xr)
    a_cum, h_loc = _scan_linear_recurrence(a, u)
    hs = h_loc + a_cum * h_carry[...]
    h_carry[...] = hs[tm - 1:tm]
    y_lru = hs * _gelu_tanh(_dot(h, w_in_ref[0, :, o_lg:o_lg + d_lru]))

    mng = mng_ref[...]
    ycl_ref[0, :, 0:d_conv] = (y_conv * _rms(y_conv) * mng[:, 0:d_conv]).astype(BF16)
    ycl_ref[0, :, d_conv:d_conv + d_lru] = (y_lru * _rms(y_lru) * mng[:, d_conv:]).astype(BF16)


def _mixer_in(layer, x, mod, ln1_g, w_in_b, qg, kg, headsum, sc_w, lcw, lcb, wa_bd, ba, wx_bd, bx, lam, mng_cl,
              *, d_attn, d_conv, d_lru):
    batch, seq, d_model = x.shape
    tm = TOKEN_TILE
    n_blk = tm // MOBA_BLOCK
    nb = seq // MOBA_BLOCK

    def const(shape):
        return pl.BlockSpec(shape, lambda b, s: (0,) * len(shape))

    kern = functools.partial(_mixer_in_kernel, d_attn=d_attn, d_conv=d_conv, d_lru=d_lru)
    return pl.pallas_call(
        kern,
        grid=(batch, seq // tm),
        in_specs=[
            pl.BlockSpec((1, tm, d_model), lambda b, s: (b, s, 0)),
            pl.BlockSpec((1, 1, N_MOD, d_model), lambda b, s: (layer, b, 0, 0)),
            const((1, d_model)),
            pl.BlockSpec((1,) + w_in_b.shape[1:], lambda b, s: (layer, 0, 0)),
            const((1, d_attn)), const((1, d_attn)), const((d_attn, d_attn)),
            const(sc_w.shape), const(lcw.shape), const((1, d_lru)),
            const((d_lru, d_lru)), const((1, d_lru)), const((d_lru, d_lru)), const((1, d_lru)),
            const((1, d_lru)), const((1, d_conv + d_lru)),
        ],
        out_specs=[
            pl.BlockSpec((1, n_blk, d_attn, MOBA_BLOCK), lambda b, s: (b, s, 0, 0)),
            pl.BlockSpec((1, tm, d_attn), lambda b, s: (b, s, 0)),
            pl.BlockSpec((1, n_blk, d_attn, MOBA_BLOCK), lambda b, s: (b, s, 0, 0)),
            pl.BlockSpec((1, n_blk, 1, d_attn), lambda b, s: (b, s, 0, 0)),
            pl.BlockSpec((1, tm, d_conv + d_lru), lambda b, s: (b, s, 0)),
        ],
        out_shape=[
            jax.ShapeDtypeStruct((batch, nb, d_attn, MOBA_BLOCK), BF16),
            jax.ShapeDtypeStruct((batch, seq, d_attn), BF16),
            jax.ShapeDtypeStruct((batch, nb, d_attn, MOBA_BLOCK), BF16),
            jax.ShapeDtypeStruct((batch, nb, 1, d_attn), F32),
            jax.ShapeDtypeStruct((batch, seq, d_conv + d_lru), BF16),
        ],
        scratch_shapes=[
            pltpu.VMEM((HALO + tm, d_conv), F32),
            pltpu.VMEM((HALO + tm, d_lru), F32),
            pltpu.VMEM((1, d_lru), F32),
        ],
        compiler_params=pltpu.CompilerParams(
            dimension_semantics=("arbitrary", "arbitrary"), vmem_limit_bytes=VMEM_LIMIT_BYTES),
        name="mixer_in",
    )(x, mod, ln1_g, w_in_b, qg, kg, headsum, sc_w, lcw, lcb, wa_bd, ba, wx_bd, bx, lam, mng_cl)


def _sublane_fold(x, op):
    tiles = [x[r:r + 8] for r in range(0, x.shape[0], 8)]
    while len(tiles) > 1:
        tiles = [op(a, b) for a, b in zip(tiles[0::2], tiles[1::2])] + tiles[len(tiles) & ~1:]
    return tiles[0]


def _moba_kernel(qT_ref, k_ref, vT_ref, kmean_ref, g_ref, o_ref,
                 qcat_ref, bias_ref, s_ref, mx_ref, m_ref, acc_ref):
    g = pl.program_id(1)
    blk = MOBA_BLOCK
    qw = QUERY_BLOCKS * blk
    d_attn = qT_ref.shape[2]
    n_heads = d_attn // HEAD_DIM
    n_pairs = n_heads // 2
    nb = kmean_ref.shape[1]
    pair = 2 * HEAD_DIM
    cols = n_heads * qw
    first_blk = QUERY_BLOCKS * g

    half = lax.broadcasted_iota(jnp.int32, (pair, qw), 0) < HEAD_DIM
    kmean = kmean_ref[0, :, 0, :]

    def query_block_of(shape, axis):
        return (lax.broadcasted_iota(jnp.int32, shape, axis) & (qw - 1)) // blk

    past_q = lax.broadcasted_iota(jnp.int32, (nb, qw), 0) < first_blk + query_block_of((nb, qw), 1)
    for p in range(n_pairs):
        p0 = p * pair
        q_pair = jnp.concatenate([qT_ref[0, c, p0:p0 + pair, :] for c in range(QUERY_BLOCKS)], axis=1)
        zero = jnp.zeros_like(q_pair)
        for hh in range(2):
            hd = 2 * p + hh
            q_m = jnp.where(half if hh == 0 else jnp.logical_not(half), q_pair, zero)
            qcat_ref[p, :, hh * qw:(hh + 1) * qw] = q_m
            gate = _dot(kmean[:, p0:p0 + pair].astype(BF16), q_m)
            bias_ref[0:nb, hd * qw:(hd + 1) * qw] = jnp.where(past_q, gate, -jnp.inf)

    ones_rows = jnp.where(lax.broadcasted_iota(jnp.int32, (ACC_ROWS - HEAD_DIM, blk), 0) == 0,
                          1.0, 0.0).astype(BF16)
    key_pos = lax.broadcasted_iota(jnp.int32, (blk, qw), 0)
    qry_pos = lax.broadcasted_iota(jnp.int32, (blk, qw), 1)

    def score_parts(slot, key_blk, bias_row, diag):
        rows = pl.ds(pl.multiple_of(key_blk * blk, blk), blk)
        lo = 0 if diag is None else diag * blk
        if diag is not None:
            visible = (key_pos + diag * blk <= qry_pos)[:, lo:]

        def head(hd):
            p, hh = divmod(hd, 2)
            c0 = hd * qw + lo
            c1 = (hd + 1) * qw
            s_t = _dot(k_ref[0, rows, p * pair:(p + 1) * pair], qcat_ref[p, :, hh * qw + lo:(hh + 1) * qw])
            if diag is not None:
                s_t = jnp.where(visible, s_t, -jnp.inf)
            s_ref[slot, :, c0:c1] = s_t
            if bias_row is None:
                mx_ref[:, c0:c1] = _sublane_fold(s_t, jnp.maximum)
                return
            mx8 = jnp.maximum(mx_ref[:, c0:c1],
                              _sublane_fold(s_t, jnp.maximum) + bias_ref[pl.ds(bias_row, 1), c0:c1])
            mx_ref[:, c0:c1] = mx8
            m_ref[slot, :, c0:c1] = jnp.maximum(jnp.max(mx8, axis=0, keepdims=True), M_FLOOR)

        return [functools.partial(head, hd) for hd in range(n_heads)]

    def value_parts(slot, m_prev, key_blk, bias_row, lo=0):
        m_cur = m_ref[slot]
        alpha = jnp.exp2(m_prev - m_cur)
        m_eff = m_cur - bias_ref[pl.ds(bias_row, 1), :]

        def head(hd):
            r0 = hd * HEAD_DIM
            c0 = hd * qw + lo
            c1 = (hd + 1) * qw
            p_t = jnp.exp2(s_ref[slot, :, c0:c1] - m_eff[:, c0:c1])
            v_t = jnp.concatenate([vT_ref[0, key_blk, r0:r0 + HEAD_DIM, :], ones_rows], axis=0)
            a0 = hd * ACC_ROWS
            acc_ref[a0:a0 + ACC_ROWS, lo:] = (alpha[:, c0:c1] * acc_ref[a0:a0 + ACC_ROWS, lo:]
                                              + _dot(v_t, p_t.astype(BF16)))

        return [functools.partial(head, hd) for hd in range(n_heads)]

    def interleave(scores, values):
        for score, value in zip(scores, values):
            score()
            value()

    for score in score_parts(0, first_blk, None, 0):
        score()

    gate = bias_ref[0:nb, :]
    blk_i = lax.broadcasted_iota(jnp.int32, (nb, cols), 0)
    blk_f = blk_i.astype(F32)
    keep = jnp.zeros((nb, cols), F32)
    for _ in range(MOBA_TOPK):
        top = jnp.max(gate, axis=0, keepdims=True)
        first = jnp.min(jnp.where(gate == top, blk_f, float(nb)), axis=0, keepdims=True)
        hit = blk_f == first
        keep = jnp.where(hit, 1.0, keep)
        gate = jnp.where(hit, -jnp.inf, gate)
    keep = jnp.where(blk_i < first_blk + query_block_of((nb, cols), 1), keep, 0.0)
    bias_ref[0:nb, :] = jnp.where(keep > 0.0, 0.0, -jnp.inf)
    col_c = query_block_of((1, cols), 1)
    for c in range(QUERY_BLOCKS):
        bias_ref[nb + c:nb + c + 1, :] = jnp.where(col_c > c, bias_ref[pl.ds(first_blk + c, 1), :], 0.0)

    mx8 = mx_ref[...] + bias_ref[nb:nb + 1, :]
    mx_ref[...] = mx8
    m_first = jnp.maximum(jnp.max(mx8, axis=0, keepdims=True), M_FLOOR)
    m_ref[0] = m_first
    m_ref[1] = m_first

    acc_ref[...] = jnp.zeros_like(acc_ref)

    def step_blocks(t):
        return jnp.where(t == 0, first_blk, t - 1), jnp.where(t == 0, nb, t - 1)

    def fused_step(t, cur):
        m_prev = m_ref[cur]
        interleave(score_parts(cur, t - 1, t - 1, None), value_parts(1 - cur, m_prev, *step_blocks(t - 1)))

    def two_steps(u):
        for cur in (1, 0):
            fused_step(2 * u + 2 - cur, cur)

    def loop_body(v, carry):
        two_steps(2 * v)
        two_steps(2 * v + 1)
        return carry

    step_pairs = first_blk // 2
    lax.fori_loop(0, step_pairs // 2, loop_body, 0)

    @pl.when((step_pairs & 1) == 1)
    def _():
        two_steps(step_pairs - 1)

    for c in range(1, QUERY_BLOCKS):
        cur = c & 1
        m_prev = m_ref[cur]
        if c == 1:
            previous = value_parts(1 - cur, m_prev, *step_blocks(first_blk))
        else:
            previous = value_parts(1 - cur, m_prev, first_blk + c - 1, nb + c - 1, lo=(c - 1) * blk)
        interleave(score_parts(cur, first_blk + c, nb + c, c), previous)
    last = QUERY_BLOCKS - 1
    for value in value_parts(last & 1, m_ref[1 - (last & 1)], first_blk + last, nb + last, lo=last * blk):
        value()

    y_t = jnp.concatenate(
        [acc_ref[hd * ACC_ROWS:hd * ACC_ROWS + HEAD_DIM, :]
         / acc_ref[hd * ACC_ROWS + HEAD_DIM:hd * ACC_ROWS + HEAD_DIM + 1, :]
         for hd in range(n_heads)], axis=0)
    y_t = y_t * _rms(y_t, axis=0)
    o_ref[0] = (y_t.T * g_ref[...]).astype(BF16)


def _moba(qT, k, vT, kmean, mng_attn):
    batch, nb, d_attn, blk = qT.shape
    seq = k.shape[1]
    n_heads = d_attn // HEAD_DIM
    qw = QUERY_BLOCKS * blk
    cols = n_heads * qw
    assert QUERY_BLOCKS >= 2 and QUERY_BLOCKS % 2 == 0 and nb % QUERY_BLOCKS == 0

    return pl.pallas_call(
        _moba_kernel,
        grid=(batch, nb // QUERY_BLOCKS),
        in_specs=[
            pl.BlockSpec((1, QUERY_BLOCKS, d_attn, blk), lambda b, g: (b, g, 0, 0)),
            pl.BlockSpec((1, seq, d_attn), lambda b, g: (b, 0, 0)),
            pl.BlockSpec((1, nb, d_attn, blk), lambda b, g: (b, 0, 0, 0)),
            pl.BlockSpec((1, nb, 1, d_attn), lambda b, g: (b, 0, 0, 0)),
            pl.BlockSpec((1, d_attn), lambda b, g: (0, 0)),
        ],
        out_specs=pl.BlockSpec((1, qw, d_attn), lambda b, g: (b, g, 0)),
        out_shape=jax.ShapeDtypeStruct((batch, seq, d_attn), BF16),
        scratch_shapes=[
            pltpu.VMEM((n_heads // 2, 2 * HEAD_DIM, 2 * qw), BF16),
            pltpu.VMEM((nb + 8, cols), F32),
            pltpu.VMEM((2, blk, cols), F32),
            pltpu.VMEM((8, cols), F32),
            pltpu.VMEM((2, 1, cols), F32),
            pltpu.VMEM((n_heads * ACC_ROWS, qw), F32),
        ],
        compiler_params=pltpu.CompilerParams(
            dimension_semantics=("arbitrary", "arbitrary"), vmem_limit_bytes=VMEM_LIMIT_BYTES),
        name="moba_attention",
    )(qT, k, vT, kmean, mng_attn)


def _out_mlp_kernel(x_ref, ya_ref, ycl_ref, mod_ref, ln2_ref, wout_ref, wup_ref, wdown_ref, o_ref):
    d_attn = ya_ref.shape[2]
    d_ff = wup_ref.shape[2]
    x = x_ref[0]
    m = mod_ref[0, 0]
    gate1, shift2, scale2, gate2 = m[2:3], m[3:4], m[4:5], m[5:6]
    mix = _dot(ya_ref[0], wout_ref[0, 0:d_attn, :]) + _dot(ycl_ref[0], wout_ref[0, d_attn:, :])
    x1 = x + gate1 * mix
    h2 = ((x1 * _rms(x1) * ln2_ref[...]) * (1.0 + scale2) + shift2).astype(BF16)
    ff = jnp.zeros_like(x1)
    for c0 in range(0, d_ff, FF_CHUNK):
        up = _dot(h2, wup_ref[0, :, c0:c0 + FF_CHUNK])
        act = jnp.square(jnp.maximum(up, 0.0)).astype(BF16)
        ff = ff + _dot(act, wdown_ref[0, c0:c0 + FF_CHUNK, :])
    o_ref[0] = x1 + gate2 * ff


def _out_mlp(layer, x, ya, ycl, mod, ln2_g, w_out_b, w_up_b, w_down_b):
    batch, seq, d_model = x.shape
    tm = MLP_TOKEN_TILE
    d_attn = ya.shape[2]
    d_cl = ycl.shape[2]

    def layer_weight(w):
        return pl.BlockSpec((1,) + w.shape[1:], lambda b, s: (layer, 0, 0), pipeline_mode=pl.Buffered(1))

    return pl.pallas_call(
        _out_mlp_kernel,
        grid=(batch, seq // tm),
        in_specs=[
            pl.BlockSpec((1, tm, d_model), lambda b, s: (b, s, 0)),
            pl.BlockSpec((1, tm, d_attn), lambda b, s: (b, s, 0)),
            pl.BlockSpec((1, tm, d_cl), lambda b, s: (b, s, 0)),
            pl.BlockSpec((1, 1, N_MOD, d_model), lambda b, s: (layer, b, 0, 0)),
            pl.BlockSpec((1, d_model), lambda b, s: (0, 0)),
            layer_weight(w_out_b), layer_weight(w_up_b), layer_weight(w_down_b),
        ],
        out_specs=pl.BlockSpec((1, tm, d_model), lambda b, s: (b, s, 0)),
        out_shape=jax.ShapeDtypeStruct((batch, seq, d_model), F32),
        compiler_params=pltpu.CompilerParams(
            dimension_semantics=("arbitrary", "arbitrary"), vmem_limit_bytes=VMEM_LIMIT_BYTES),
        name="out_mlp",
    )(x, ya, ycl, mod, ln2_g, w_out_b, w_up_b, w_down_b)


def _block_diag(w):
    n, r, c = w.shape
    eye = jnp.eye(n, dtype=w.dtype)
    return (eye[:, None, :, None] * w[:, :, None, :]).reshape(n * r, n * c)


def kernel(x, c, ln1_g, ln2_g, w_ada, b_ada, w_in, q_norm_g, k_norm_g, sc_w, lru_conv_w, lru_conv_b,
           lru_wa, lru_ba, lru_wx, lru_bx, lru_lambda, mix_norm_g, w_out, w_up, w_down):
    batch, seq, d_model = x.shape
    depth = w_in.shape[0]
    d_conv = sc_w.shape[2]
    d_lru = lru_conv_w.shape[2]
    d_attn = mix_norm_g.shape[1] - d_conv - d_lru
    n_heads = d_attn // HEAD_DIM
    assert seq % TOKEN_TILE == 0 and TOKEN_TILE % MOBA_BLOCK == 0 and seq % MLP_TOKEN_TILE == 0
    assert w_in.shape[2] == 3 * d_attn + 3 * d_conv + 2 * d_lru

    mod = _modulation(c, w_ada, b_ada).reshape(depth, batch, N_MOD, d_model)
    headsum = _block_diag(jnp.full((n_heads, HEAD_DIM, HEAD_DIM), 1.0 / HEAD_DIM, F32)).astype(BF16)

    w_in_b, w_out_b, w_up_b, w_down_b = (w.astype(BF16) for w in (w_in, w_out, w_up, w_down))
    for l in range(depth):
        row = lambda v: v.reshape(1, -1)
        qT, k, vT, kmean, ycl = _mixer_in(
            l, x, mod, row(ln1_g[l]), w_in_b,
            row(jnp.tile(q_norm_g[l], n_heads)), row(jnp.tile(k_norm_g[l], n_heads)), headsum,
            sc_w[l], lru_conv_w[l], row(lru_conv_b[l]),
            _block_diag(lru_wa[l]).astype(BF16), row(lru_ba[l]),
            _block_diag(lru_wx[l]).astype(BF16), row(lru_bx[l]),
            row(lru_lambda[l]), row(mix_norm_g[l, d_attn:]),
            d_attn=d_attn, d_conv=d_conv, d_lru=d_lru)
        ya = _moba(qT, k, vT, kmean, row(mix_norm_g[l, :d_attn]))
        x = _out_mlp(l, x, ya, ycl, mod, row(ln2_g[l]), w_out_b, w_up_b, w_down_b)
    return x
```

```python
import functools
import math

import jax
import jax.numpy as jnp
from jax import lax
from jax.experimental import pallas as pl
from jax.experimental.pallas import tpu as pltpu

F32 = jnp.float32
BF16 = jnp.bfloat16

HEAD_DIM = 64
MOBA_BLOCK = 256
MOBA_TOPK = 3
LRU_C = 8.0
N_MOD = 6
EPS = 1e-6

TOKEN_TILE = 512
MLP_TOKEN_TILE = 1024
MOD_COL_TILE = 1536
FF_CHUNK = 1024
SUBLANES = 8
ROW_SPLIT = 2
HALO = SUBLANES
ACC_ROWS = HEAD_DIM + 2 * SUBLANES
LOG2_E = math.log2(math.e)
QUERY_BLOCKS = 2
M_FLOOR = -1e30
VMEM_LIMIT_BYTES = 56 * 1024 * 1024


def _rms(x, axis=-1):
    return lax.rsqrt(jnp.mean(x * x, axis=axis, keepdims=True) + EPS)


def _dot(a, b):
    return jnp.dot(a, b, preferred_element_type=F32)


def _mod_kernel(c_ref, w_ref, b_ref, o_ref):
    c = c_ref[...]
    c_act = (c * jax.nn.sigmoid(c)).astype(BF16)
    o_ref[0] = _dot(c_act, w_ref[0].astype(BF16)) + b_ref[0]


def _modulation(c, w_ada, b_ada):
    depth, d_model, n_out = w_ada.shape
    batch = c.shape[0]
    return pl.pallas_call(
        _mod_kernel,
        grid=(depth, n_out // MOD_COL_TILE),
        in_specs=[
            pl.BlockSpec((batch, d_model), lambda l, j: (0, 0)),
            pl.BlockSpec((1, d_model, MOD_COL_TILE), lambda l, j: (l, 0, j)),
            pl.BlockSpec((1, 1, MOD_COL_TILE), lambda l, j: (l, 0, j)),
        ],
        out_specs=pl.BlockSpec((1, batch, MOD_COL_TILE), lambda l, j: (l, 0, j)),
        out_shape=jax.ShapeDtypeStruct((depth, batch, n_out), F32),
        compiler_params=pltpu.CompilerParams(
            dimension_semantics=("arbitrary", "arbitrary"), vmem_limit_bytes=VMEM_LIMIT_BYTES),
        name="adaln_modulation",
    )(c, w_ada, b_ada.reshape(depth, 1, n_out))


def _scan_linear_recurrence(a, u):
    n = a.shape[0]
    row = lax.broadcasted_iota(jnp.int32, a.shape, 0)
    d = 1
    while d < n:
        keep = row >= d
        a_prev = jnp.where(keep, pltpu.roll(a, d, 0), 1.0)
        u_prev = jnp.where(keep, pltpu.roll(u, d, 0), 0.0)
        u = a * u_prev + u
        a = a * a_prev
        d *= 2
    return a, u


def _gelu_tanh(x):
    return 0.5 * x * (1.0 + jnp.tanh(math.sqrt(2.0 / math.pi) * (x + 0.044715 * (x * x * x))))


def _softplus(z):
    return jnp.maximum(z, 0.0) + jnp.log1p(jnp.exp(-jnp.abs(z)))


def _mixer_in_kernel(x_ref, mod_ref, ln1_ref, w_in_ref, qg_ref, kg_ref, headsum_ref, scw_ref,
                     lcw_ref, lcb_ref, wa_ref, ba_ref, wx_ref, bx_ref, lam_ref, mng_ref,
                     qT_ref, k_ref, vT_ref, kmean_ref, ycl_ref,
                     cu_buf, lx_buf, h_carry, *, d_attn, d_conv, d_lru):
    s = pl.program_id(1)
    tm = x_ref.shape[1]
    n_blk = tm // MOBA_BLOCK

    @pl.when(s == 0)
    def _():
        cu_buf[0:HALO, :] = jnp.zeros((HALO, d_conv), F32)
        lx_buf[0:HALO, :] = jnp.zeros((HALO, d_lru), F32)
        h_carry[...] = jnp.zeros_like(h_carry)

    x = x_ref[0]
    m = mod_ref[0, 0]
    shift1, scale1 = m[0:1], m[1:2]
    rows = tm // ROW_SPLIT
    h_parts = [((xr * _rms(xr) * ln1_ref[...]) * (1.0 + scale1) + shift1).astype(BF16)
               for xr in (x[r:r + rows] for r in range(0, tm, rows))]

    def proj(off, width):
        w = w_in_ref[0, :, off:off + width]
        return jnp.concatenate([_dot(h_part, w) for h_part in h_parts], axis=0)

    o_q, o_k, o_v = 0, d_attn, 2 * d_attn
    o_b = 3 * d_attn
    o_c, o_u = o_b + d_conv, o_b + 2 * d_conv
    o_lx = o_b + 3 * d_conv
    o_lg = o_lx + d_lru

    def head_norm(t, g_ref):
        ms = _dot((t * t).astype(BF16), headsum_ref[...])
        return t * lax.rsqrt(ms + EPS) * g_ref[...]

    q = head_norm(proj(o_q, d_attn), qg_ref)
    qT = (q * (LOG2_E / math.sqrt(HEAD_DIM))).T.astype(BF16)
    for c in range(n_blk):
        qT_ref[0, c] = qT[:, c * MOBA_BLOCK:(c + 1) * MOBA_BLOCK]

    k = head_norm(proj(o_k, d_attn), kg_ref)
    k_ref[0] = k.astype(BF16)
    for c in range(n_blk):
        kmean_ref[0, c] = jnp.mean(k[c * MOBA_BLOCK:(c + 1) * MOBA_BLOCK], axis=0, keepdims=True)

    vT = proj(o_v, d_attn).T.astype(BF16)
    for c in range(n_blk):
        vT_ref[0, c] = vT[:, c * MOBA_BLOCK:(c + 1) * MOBA_BLOCK]

    sc_b = proj(o_b, d_conv)
    cu = proj(o_c, d_conv) * proj(o_u, d_conv)
    cu_buf[HALO:HALO + tm, :] = cu
    scw = scw_ref[...]
    conv = (scw[0:1] * cu_buf[HALO - 2:HALO - 2 + tm, :]
            + scw[1:2] * cu_buf[HALO - 1:HALO - 1 + tm, :]
            + scw[2:3] * cu)
    cu_buf[0:HALO, :] = cu[tm - HALO:tm]
    y_conv = sc_b * conv

    lx = proj(o_lx, d_lru)
    lx_buf[HALO:HALO + tm, :] = lx
    lcw = lcw_ref[...]
    xr = (lcw[0:1] * lx_buf[HALO - 3:HALO - 3 + tm, :]
          + lcw[1:2] * lx_buf[HALO - 2:HALO - 2 + tm, :]
          + lcw[2:3] * lx_buf[HALO - 1:HALO - 1 + tm, :]
          + lcw[3:4] * lx) + lcb_ref[...]
    lx_buf[0:HALO, :] = lx[tm - HALO:tm]
    xr_b = xr.astype(BF16)
    r = jax.nn.sigmoid(_dot(xr_b, wa_ref[...]) + ba_ref[...])
    i = jax.nn.sigmoid(_dot(xr_b, wx_ref[...]) + bx_ref[...])
    log_a = (-LRU_C) * r * _softplus(-lam_ref[...])
    a = jnp.exp(log_a)
    t = jnp.tanh(log_a)
    u = jnp.sqrt((-2.0 * t) / (1.0 - t)) * (i * xr)
    a_cum, h_loc = _scan_linear_recurrence(a, u)
    hs = h_loc + a_cum * h_carry[...]
    h_carry[...] = hs[tm - 1:tm]
    y_lru = hs * _gelu_tanh(proj(o_lg, d_lru))

    mng = mng_ref[...]
    ycl_ref[0, :, 0:d_conv] = (y_conv * _rms(y_conv) * mng[:, 0:d_conv]).astype(BF16)
    ycl_ref[0, :, d_conv:d_conv + d_lru] = (y_lru * _rms(y_lru) * mng[:, d_conv:]).astype(BF16)


def _mixer_in(layer, x, mod, ln1_g, w_in_b, qg, kg, headsum, sc_w, lcw, lcb, wa_bd, ba, wx_bd, bx, lam, mng_cl,
              *, d_attn, d_conv, d_lru):
    batch, seq, d_model = x.shape
    tm = TOKEN_TILE
    n_blk = tm // MOBA_BLOCK
    nb = seq // MOBA_BLOCK

    def const(shape):
        return pl.BlockSpec(shape, lambda b, s: (0,) * len(shape))

    kern = functools.partial(_mixer_in_kernel, d_attn=d_attn, d_conv=d_conv, d_lru=d_lru)
    return pl.pallas_call(
        kern,
        grid=(batch, seq // tm),
        in_specs=[
            pl.BlockSpec((1, tm, d_model), lambda b, s: (b, s, 0)),
            pl.BlockSpec((1, 1, N_MOD, d_model), lambda b, s: (layer, b, 0, 0)),
            const((1, d_model)),
            pl.BlockSpec((1,) + w_in_b.shape[1:], lambda b, s: (layer, 0, 0)),
            const((1, d_attn)), const((1, d_attn)), const((d_attn, d_attn)),
            const(sc_w.shape), const(lcw.shape), const((1, d_lru)),
            const((d_lru, d_lru)), const((1, d_lru)), const((d_lru, d_lru)), const((1, d_lru)),
            const((1, d_lru)), const((1, d_conv + d_lru)),
        ],
        out_specs=[
            pl.BlockSpec((1, n_blk, d_attn, MOBA_BLOCK), lambda b, s: (b, s, 0, 0)),
            pl.BlockSpec((1, tm, d_attn), lambda b, s: (b, s, 0)),
            pl.BlockSpec((1, n_blk, d_attn, MOBA_BLOCK), lambda b, s: (b, s, 0, 0)),
            pl.BlockSpec((1, n_blk, 1, d_attn), lambda b, s: (b, s, 0, 0)),
            pl.BlockSpec((1, tm, d_conv + d_lru), lambda b, s: (b, s, 0)),
        ],
        out_shape=[
            jax.ShapeDtypeStruct((batch, nb, d_attn, MOBA_BLOCK), BF16),
            jax.ShapeDtypeStruct((batch, seq, d_attn), BF16),
            jax.ShapeDtypeStruct((batch, nb, d_attn, MOBA_BLOCK), BF16),
            jax.ShapeDtypeStruct((batch, nb, 1, d_attn), F32),
            jax.ShapeDtypeStruct((batch, seq, d_conv + d_lru), BF16),
        ],
        scratch_shapes=[
            pltpu.VMEM((HALO + tm, d_conv), F32),
            pltpu.VMEM((HALO + tm, d_lru), F32),
            pltpu.VMEM((1, d_lru), F32),
        ],
        compiler_params=pltpu.CompilerParams(
            dimension_semantics=("arbitrary", "arbitrary"), vmem_limit_bytes=VMEM_LIMIT_BYTES),
        name="mixer_in",
    )(x, mod, ln1_g, w_in_b, qg, kg, headsum, sc_w, lcw, lcb, wa_bd, ba, wx_bd, bx, lam, mng_cl)


def _sublane_fold(x, op):
    tiles = [x[r:r + SUBLANES] for r in range(0, x.shape[0], SUBLANES)]
    while len(tiles) > 1:
        tiles = [op(a, b) for a, b in zip(tiles[0::2], tiles[1::2])] + tiles[len(tiles) & ~1:]
    return tiles[0]


def _moba_kernel(qT_ref, k_ref, vT_ref, kmean_ref, g_ref, o_ref,
                 qcat_ref, bias_ref, s_ref, mx_ref, m_ref, acc_ref):
    g = pl.program_id(1)
    blk = MOBA_BLOCK
    qw = QUERY_BLOCKS * blk
    d_attn = qT_ref.shape[2]
    n_heads = d_attn // HEAD_DIM
    n_pairs = n_heads // 2
    nb = kmean_ref.shape[1]
    pair = 2 * HEAD_DIM
    cols = n_heads * qw
    first_blk = QUERY_BLOCKS * g

    half = lax.broadcasted_iota(jnp.int32, (pair, qw), 0) < HEAD_DIM
    kmean = kmean_ref[0, :, 0, :]

    def query_block_of(shape, axis):
        return (lax.broadcasted_iota(jnp.int32, shape, axis) & (qw - 1)) // blk

    past_q = lax.broadcasted_iota(jnp.int32, (nb, qw), 0) < first_blk + query_block_of((nb, qw), 1)
    for p in range(n_pairs):
        p0 = p * pair
        q_pair = jnp.concatenate([qT_ref[0, c, p0:p0 + pair, :] for c in range(QUERY_BLOCKS)], axis=1)
        zero = jnp.zeros_like(q_pair)
        for hh in range(2):
            hd = 2 * p + hh
            q_m = jnp.where(half if hh == 0 else jnp.logical_not(half), q_pair, zero)
            qcat_ref[p, :, hh * qw:(hh + 1) * qw] = q_m
            gate = _dot(kmean[:, p0:p0 + pair].astype(BF16), q_m)
            bias_ref[0:nb, hd * qw:(hd + 1) * qw] = jnp.where(past_q, gate, -jnp.inf)

    ones_rows = jnp.where(lax.broadcasted_iota(jnp.int32, (ACC_ROWS - HEAD_DIM, blk), 0) == 0,
                          1.0, 0.0).astype(BF16)
    key_pos = lax.broadcasted_iota(jnp.int32, (blk, qw), 0)
    qry_pos = lax.broadcasted_iota(jnp.int32, (blk, qw), 1)

    def score_parts(slot, key_blk, bias_row, diag):
        rows = pl.ds(pl.multiple_of(key_blk * blk, blk), blk)
        lo = 0 if diag is None else diag * blk
        if diag is not None:
            visible = (key_pos + diag * blk <= qry_pos)[:, lo:]

        def head(hd):
            p, hh = divmod(hd, 2)
            c0 = hd * qw + lo
            c1 = (hd + 1) * qw
            s_t = _dot(k_ref[0, rows, p * pair:(p + 1) * pair], qcat_ref[p, :, hh * qw + lo:(hh + 1) * qw])
            if diag is not None:
                s_t = jnp.where(visible, s_t, -jnp.inf)
            s_ref[slot, :, c0:c1] = s_t
            if bias_row is None:
                mx_ref[:, c0:c1] = _sublane_fold(s_t, jnp.maximum)
                return
            mx8 = jnp.maximum(mx_ref[:, c0:c1],
                              _sublane_fold(s_t, jnp.maximum) + bias_ref[pl.ds(bias_row, 1), c0:c1])
            mx_ref[:, c0:c1] = mx8
            m_ref[slot, :, c0:c1] = jnp.maximum(jnp.max(mx8, axis=0, keepdims=True), M_FLOOR)

        return [functools.partial(head, hd) for hd in range(n_heads)]

    def value_parts(slot, m_prev, key_blk, bias_row, lo=0):
        m_cur = m_ref[slot]
        alpha = jnp.exp2(m_prev - m_cur)
        m_eff = m_cur - bias_ref[pl.ds(bias_row, 1), :]

        def head(hd):
            r0 = hd * HEAD_DIM
            c0 = hd * qw + lo
            c1 = (hd + 1) * qw
            p_t = jnp.exp2(s_ref[slot, :, c0:c1] - m_eff[:, c0:c1])
            v_t = jnp.concatenate([vT_ref[0, key_blk, r0:r0 + HEAD_DIM, :], ones_rows], axis=0)
            a0 = hd * ACC_ROWS
            acc_ref[a0:a0 + ACC_ROWS, lo:] = (alpha[:, c0:c1] * acc_ref[a0:a0 + ACC_ROWS, lo:]
                                              + _dot(v_t, p_t.astype(BF16)))

        return [functools.partial(head, hd) for hd in range(n_heads)]

    def interleave(scores, values):
        for score, value in zip(scores, values):
            score()
            value()

    for score in score_parts(0, first_blk, None, 0):
        score()

    gate = bias_ref[0:nb, :]
    blk_i = lax.broadcasted_iota(jnp.int32, (nb, cols), 0)
    blk_f = blk_i.astype(F32)
    keep = jnp.zeros((nb, cols), F32)
    for _ in range(MOBA_TOPK):
        top = jnp.max(gate, axis=0, keepdims=True)
        first = jnp.min(jnp.where(gate == top, blk_f, float(nb)), axis=0, keepdims=True)
        hit = blk_f == first
        keep = jnp.where(hit, 1.0, keep)
        gate = jnp.where(hit, -jnp.inf, gate)
    keep = jnp.where(blk_i < first_blk + query_block_of((nb, cols), 1), keep, 0.0)
    bias_ref[0:nb, :] = jnp.where(keep > 0.0, 0.0, -jnp.inf)
    col_c = query_block_of((1, cols), 1)
    for c in range(QUERY_BLOCKS):
        bias_ref[nb + c:nb + c + 1, :] = jnp.where(col_c > c, bias_ref[pl.ds(first_blk + c, 1), :], 0.0)

    mx8 = mx_ref[...] + bias_ref[nb:nb + 1, :]
    mx_ref[...] = mx8
    m_first = jnp.maximum(jnp.max(mx8, axis=0, keepdims=True), M_FLOOR)
    m_ref[0] = m_first
    m_ref[1] = m_first

    acc_ref[...] = jnp.zeros_like(acc_ref)

    def step_blocks(t):
        return jnp.where(t == 0, first_blk, t - 1), jnp.where(t == 0, nb, t - 1)

    def fused_step(t, cur):
        m_prev = m_ref[cur]
        interleave(score_parts(cur, t - 1, t - 1, None), value_parts(1 - cur, m_prev, *step_blocks(t - 1)))

    def two_steps(u):
        for cur in (1, 0):
            fused_step(2 * u + 2 - cur, cur)

    def loop_body(v, carry):
        two_steps(2 * v)
        two_steps(2 * v + 1)
        return carry

    step_pairs = first_blk // 2
    lax.fori_loop(0, step_pairs // 2, loop_body, 0)

    @pl.when((step_pairs & 1) == 1)
    def _():
        two_steps(step_pairs - 1)

    for c in range(1, QUERY_BLOCKS):
        cur = c & 1
        m_prev = m_ref[cur]
        if c == 1:
            previous = value_parts(1 - cur, m_prev, *step_blocks(first_blk))
        else:
            previous = value_parts(1 - cur, m_prev, first_blk + c - 1, nb + c - 1, lo=(c - 1) * blk)
        interleave(score_parts(cur, first_blk + c, nb + c, c), previous)
    last = QUERY_BLOCKS - 1
    for value in value_parts(last & 1, m_ref[1 - (last & 1)], first_blk + last, nb + last, lo=last * blk):
        value()

    y_t = jnp.concatenate(
        [acc_ref[hd * ACC_ROWS:hd * ACC_ROWS + HEAD_DIM, :]
         / acc_ref[hd * ACC_ROWS + HEAD_DIM:hd * ACC_ROWS + HEAD_DIM + 1, :]
         for hd in range(n_heads)], axis=0)
    y_t = y_t * _rms(y_t, axis=0)
    o_ref[0] = (y_t.T * g_ref[...]).astype(BF16)


def _moba(qT, k, vT, kmean, mng_attn):
    batch, nb, d_attn, blk = qT.shape
    seq = k.shape[1]
    n_heads = d_attn // HEAD_DIM
    qw = QUERY_BLOCKS * blk
    cols = n_heads * qw
    assert QUERY_BLOCKS >= 2 and QUERY_BLOCKS % 2 == 0 and nb % QUERY_BLOCKS == 0

    return pl.pallas_call(
        _moba_kernel,
        grid=(batch, nb // QUERY_BLOCKS),
        in_specs=[
            pl.BlockSpec((1, QUERY_BLOCKS, d_attn, blk), lambda b, g: (b, g, 0, 0)),
            pl.BlockSpec((1, seq, d_attn), lambda b, g: (b, 0, 0)),
            pl.BlockSpec((1, nb, d_attn, blk), lambda b, g: (b, 0, 0, 0)),
            pl.BlockSpec((1, nb, 1, d_attn), lambda b, g: (b, 0, 0, 0)),
            pl.BlockSpec((1, d_attn), lambda b, g: (0, 0)),
        ],
        out_specs=pl.BlockSpec((1, qw, d_attn), lambda b, g: (b, g, 0)),
        out_shape=jax.ShapeDtypeStruct((batch, seq, d_attn), BF16),
        scratch_shapes=[
            pltpu.VMEM((n_heads // 2, 2 * HEAD_DIM, 2 * qw), BF16),
            pltpu.VMEM((nb + SUBLANES, cols), F32),
            pltpu.VMEM((2, blk, cols), F32),
            pltpu.VMEM((SUBLANES, cols), F32),
            pltpu.VMEM((2, 1, cols), F32),
            pltpu.VMEM((n_heads * ACC_ROWS, qw), F32),
        ],
        compiler_params=pltpu.CompilerParams(
            dimension_semantics=("arbitrary", "arbitrary"), vmem_limit_bytes=VMEM_LIMIT_BYTES),
        name="moba_attention",
    )(qT, k, vT, kmean, mng_attn)


def _out_mlp_kernel(x_ref, ya_ref, ycl_ref, mod_ref, ln2_ref, wout_ref, wup_ref, wdown_ref, o_ref):
    d_attn = ya_ref.shape[2]
    d_ff = wup_ref.shape[2]
    x = x_ref[0]
    m = mod_ref[0, 0]
    gate1, shift2, scale2, gate2 = m[2:3], m[3:4], m[4:5], m[5:6]
    mix = _dot(ya_ref[0], wout_ref[0, 0:d_attn, :]) + _dot(ycl_ref[0], wout_ref[0, d_attn:, :])
    x1 = x + gate1 * mix
    h2 = ((x1 * _rms(x1) * ln2_ref[...]) * (1.0 + scale2) + shift2).astype(BF16)
    ff = jnp.zeros_like(x1)
    for c0 in range(0, d_ff, FF_CHUNK):
        up = _dot(h2, wup_ref[0, :, c0:c0 + FF_CHUNK])
        act = jnp.square(jnp.maximum(up, 0.0)).astype(BF16)
        ff = ff + _dot(act, wdown_ref[0, c0:c0 + FF_CHUNK, :])
    o_ref[0] = x1 + gate2 * ff


def _out_mlp(layer, x, ya, ycl, mod, ln2_g, w_out_b, w_up_b, w_down_b):
    batch, seq, d_model = x.shape
    tm = MLP_TOKEN_TILE
    d_attn = ya.shape[2]
    d_cl = ycl.shape[2]

    def layer_weight(w):
        return pl.BlockSpec((1,) + w.shape[1:], lambda b, s: (layer, 0, 0), pipeline_mode=pl.Buffered(1))

    return pl.pallas_call(
        _out_mlp_kernel,
        grid=(batch, seq // tm),
        in_specs=[
            pl.BlockSpec((1, tm, d_model), lambda b, s: (b, s, 0)),
            pl.BlockSpec((1, tm, d_attn), lambda b, s: (b, s, 0)),
            pl.BlockSpec((1, tm, d_cl), lambda b, s: (b, s, 0)),
            pl.BlockSpec((1, 1, N_MOD, d_model), lambda b, s: (layer, b, 0, 0)),
            pl.BlockSpec((1, d_model), lambda b, s: (0, 0)),
            layer_weight(w_out_b), layer_weight(w_up_b), layer_weight(w_down_b),
        ],
        out_specs=pl.BlockSpec((1, tm, d_model), lambda b, s: (b, s, 0)),
        out_shape=jax.ShapeDtypeStruct((batch, seq, d_model), F32),
        compiler_params=pltpu.CompilerParams(
            dimension_semantics=("arbitrary", "arbitrary"), vmem_limit_bytes=VMEM_LIMIT_BYTES),
        name="out_mlp",
    )(x, ya, ycl, mod, ln2_g, w_out_b, w_up_b, w_down_b)


def _block_diag(w):
    n, r, c = w.shape
    eye = jnp.eye(n, dtype=w.dtype)
    return (eye[:, None, :, None] * w[:, :, None, :]).reshape(n * r, n * c)


def kernel(x, c, ln1_g, ln2_g, w_ada, b_ada, w_in, q_norm_g, k_norm_g, sc_w, lru_conv_w, lru_conv_b,
           lru_wa, lru_ba, lru_wx, lru_bx, lru_lambda, mix_norm_g, w_out, w_up, w_down):
    batch, seq, d_model = x.shape
    depth = w_in.shape[0]
    d_conv = sc_w.shape[2]
    d_lru = lru_conv_w.shape[2]
    d_attn = mix_norm_g.shape[1] - d_conv - d_lru
    n_heads = d_attn // HEAD_DIM
    assert seq % TOKEN_TILE == 0 and TOKEN_TILE % MOBA_BLOCK == 0 and seq % MLP_TOKEN_TILE == 0
    assert w_in.shape[2] == 3 * d_attn + 3 * d_conv + 2 * d_lru

    mod = _modulation(c, w_ada, b_ada).reshape(depth, batch, N_MOD, d_model)
    headsum = _block_diag(jnp.full((n_heads, HEAD_DIM, HEAD_DIM), 1.0 / HEAD_DIM, F32)).astype(BF16)

    w_in_b, w_out_b, w_up_b, w_down_b = (w.astype(BF16) for w in (w_in, w_out, w_up, w_down))
    for l in range(depth):
        row = lambda v: v.reshape(1, -1)
        qT, k, vT, kmean, ycl = _mixer_in(
            l, x, mod, row(ln1_g[l]), w_in_b,
            row(jnp.tile(q_norm_g[l], n_heads)), row(jnp.tile(k_norm_g[l], n_heads)), headsum,
            sc_w[l], lru_conv_w[l], row(lru_conv_b[l]),
            _block_diag(lru_wa[l]).astype(BF16), row(lru_ba[l]),
            _block_diag(lru_wx[l]).astype(BF16), row(lru_bx[l]),
            row(lru_lambda[l]), row(mix_norm_g[l, d_attn:]),
            d_attn=d_attn, d_conv=d_conv, d_lru=d_lru)
        ya = _moba(qT, k, vT, kmean, row(mix_norm_g[l, :d_attn]))
        x = _out_mlp(l, x, ya, ycl, mod, row(ln2_g[l]), w_out_b, w_up_b, w_down_b)
    return x
```

```python
import functools
import math

import jax
import jax.numpy as jnp
from jax import lax
from jax.experimental import pallas as pl
from jax.experimental.pallas import tpu as pltpu

F32 = jnp.float32
BF16 = jnp.bfloat16

HEAD_DIM = 64
MOBA_BLOCK = 256
MOBA_TOPK = 3
LRU_C = 8.0
N_MOD = 6
EPS = 1e-6

TOKEN_TILE = 1024
MLP_TOKEN_TILE = 1024
MOD_COL_TILE = 3072
FF_CHUNK = 1024
SUBLANES = 8
HALO = SUBLANES
ACC_ROWS = HEAD_DIM + 2 * SUBLANES
LOG2_E = math.log2(math.e)
QUERY_BLOCKS = 2
M_FLOOR = -1e30
VMEM_LIMIT_BYTES = 56 * 1024 * 1024


def _rms(x, axis=-1):
    return lax.rsqrt(jnp.mean(x * x, axis=axis, keepdims=True) + EPS)


def _dot(a, b):
    return jnp.dot(a, b, preferred_element_type=F32)


def _mod_kernel(c_ref, w_ref, b_ref, o_ref):
    c = c_ref[...]
    c_act = (c * jax.nn.sigmoid(c)).astype(BF16)
    o_ref[0] = _dot(c_act, w_ref[0].astype(BF16)) + b_ref[0]


def _modulation(c, w_ada, b_ada):
    depth, d_model, n_out = w_ada.shape
    batch = c.shape[0]
    return pl.pallas_call(
        _mod_kernel,
        grid=(depth, n_out // MOD_COL_TILE),
        in_specs=[
            pl.BlockSpec((batch, d_model), lambda l, j: (0, 0)),
            pl.BlockSpec((1, d_model, MOD_COL_TILE), lambda l, j: (l, 0, j)),
            pl.BlockSpec((1, 1, MOD_COL_TILE), lambda l, j: (l, 0, j)),
        ],
        out_specs=pl.BlockSpec((1, batch, MOD_COL_TILE), lambda l, j: (l, 0, j)),
        out_shape=jax.ShapeDtypeStruct((depth, batch, n_out), F32),
        compiler_params=pltpu.CompilerParams(
            dimension_semantics=("arbitrary", "arbitrary"), vmem_limit_bytes=VMEM_LIMIT_BYTES),
        name="adaln_modulation",
    )(c, w_ada, b_ada.reshape(depth, 1, n_out))


def _scan_linear_recurrence(a, u):
    n = a.shape[0]
    row = lax.broadcasted_iota(jnp.int32, a.shape, 0)
    d = 1
    while d < n:
        keep = row >= d
        a_prev = jnp.where(keep, pltpu.roll(a, d, 0), 1.0)
        u_prev = jnp.where(keep, pltpu.roll(u, d, 0), 0.0)
        u = a * u_prev + u
        a = a * a_prev
        d *= 2
    return a, u


def _gelu_tanh(x):
    return 0.5 * x * (1.0 + jnp.tanh(math.sqrt(2.0 / math.pi) * (x + 0.044715 * (x * x * x))))


def _softplus(z):
    return jnp.maximum(z, 0.0) + jnp.log1p(jnp.exp(-jnp.abs(z)))


def _mixer_in_kernel(x_ref, mod_ref, ln1_ref, w_in_ref, qg_ref, kg_ref, headsum_ref, scw_ref,
                     lcw_ref, lcb_ref, wa_ref, ba_ref, wx_ref, bx_ref, lam_ref, mng_ref,
                     qT_ref, k_ref, vT_ref, kmean_ref, ycl_ref,
                     cu_buf, lx_buf, h_carry, *, d_attn, d_conv, d_lru):
    s = pl.program_id(1)
    tm = x_ref.shape[1]
    n_blk = tm // MOBA_BLOCK

    @pl.when(s == 0)
    def _():
        cu_buf[0:HALO, :] = jnp.zeros((HALO, d_conv), F32)
        lx_buf[0:HALO, :] = jnp.zeros((HALO, d_lru), F32)
        h_carry[...] = jnp.zeros_like(h_carry)

    x = x_ref[0]
    m = mod_ref[0, 0]
    shift1, scale1 = m[0:1], m[1:2]
    h = ((x * _rms(x) * ln1_ref[...]) * (1.0 + scale1) + shift1).astype(BF16)

    def proj(off, width):
        return _dot(h, w_in_ref[0, :, off:off + width])

    o_q, o_k, o_v = 0, d_attn, 2 * d_attn
    o_b = 3 * d_attn
    o_c, o_u = o_b + d_conv, o_b + 2 * d_conv
    o_lx = o_b + 3 * d_conv
    o_lg = o_lx + d_lru

    def head_norm(t, g_ref):
        ms = _dot((t * t).astype(BF16), headsum_ref[...])
        return t * lax.rsqrt(ms + EPS) * g_ref[...]

    q = head_norm(proj(o_q, d_attn), qg_ref)
    qT = (q * (LOG2_E / math.sqrt(HEAD_DIM))).T.astype(BF16)
    for c in range(n_blk):
        qT_ref[0, c] = qT[:, c * MOBA_BLOCK:(c + 1) * MOBA_BLOCK]

    k = head_norm(proj(o_k, d_attn), kg_ref)
    k_ref[0] = k.astype(BF16)
    for c in range(n_blk):
        kmean_ref[0, c] = jnp.mean(k[c * MOBA_BLOCK:(c + 1) * MOBA_BLOCK], axis=0, keepdims=True)

    vT = proj(o_v, d_attn).T.astype(BF16)
    for c in range(n_blk):
        vT_ref[0, c] = vT[:, c * MOBA_BLOCK:(c + 1) * MOBA_BLOCK]

    sc_b = proj(o_b, d_conv)
    cu = proj(o_c, d_conv) * proj(o_u, d_conv)
    cu_buf[HALO:HALO + tm, :] = cu
    scw = scw_ref[...]
    conv = (scw[0:1] * cu_buf[HALO - 2:HALO - 2 + tm, :]
            + scw[1:2] * cu_buf[HALO - 1:HALO - 1 + tm, :]
            + scw[2:3] * cu)
    cu_buf[0:HALO, :] = cu[tm - HALO:tm]
    y_conv = sc_b * conv

    lx = proj(o_lx, d_lru)
    lx_buf[HALO:HALO + tm, :] = lx
    lcw = lcw_ref[...]
    xr = (lcw[0:1] * lx_buf[HALO - 3:HALO - 3 + tm, :]
          + lcw[1:2] * lx_buf[HALO - 2:HALO - 2 + tm, :]
          + lcw[2:3] * lx_buf[HALO - 1:HALO - 1 + tm, :]
          + lcw[3:4] * lx) + lcb_ref[...]
    lx_buf[0:HALO, :] = lx[tm - HALO:tm]
    xr_b = xr.astype(BF16)
    r = jax.nn.sigmoid(_dot(xr_b, wa_ref[...]) + ba_ref[...])
    i = jax.nn.sigmoid(_dot(xr_b, wx_ref[...]) + bx_ref[...])
    log_a = (-LRU_C) * r * _softplus(-lam_ref[...])
    a = jnp.exp(log_a)
    t = jnp.tanh(log_a)
    u = jnp.sqrt((-2.0 * t) / (1.0 - t)) * (i * xr)
    a_cum, h_loc = _scan_linear_recurrence(a, u)
    hs = h_loc + a_cum * h_carry[...]
    h_carry[...] = hs[tm - 1:tm]
    y_lru = hs * _gelu_tanh(proj(o_lg, d_lru))

    mng = mng_ref[...]
    ycl_ref[0, :, 0:d_conv] = (y_conv * _rms(y_conv) * mng[:, 0:d_conv]).astype(BF16)
    ycl_ref[0, :, d_conv:d_conv + d_lru] = (y_lru * _rms(y_lru) * mng[:, d_conv:]).astype(BF16)


def _mixer_in(layer, x, mod, ln1_g, w_in_b, qg, kg, headsum, sc_w, lcw, lcb, wa_bd, ba, wx_bd, bx, lam, mng_cl,
              *, d_attn, d_conv, d_lru):
    batch, seq, d_model = x.shape
    tm = TOKEN_TILE
    n_blk = tm // MOBA_BLOCK
    nb = seq // MOBA_BLOCK

    def const(shape):
        return pl.BlockSpec(shape, lambda b, s: (0,) * len(shape))

    kern = functools.partial(_mixer_in_kernel, d_attn=d_attn, d_conv=d_conv, d_lru=d_lru)
    return pl.pallas_call(
        kern,
        grid=(batch, seq // tm),
        in_specs=[
            pl.BlockSpec((1, tm, d_model), lambda b, s: (b, s, 0)),
            pl.BlockSpec((1, 1, N_MOD, d_model), lambda b, s: (layer, b, 0, 0)),
            const((1, d_model)),
            pl.BlockSpec((1,) + w_in_b.shape[1:], lambda b, s: (layer, 0, 0)),
            const((1, d_attn)), const((1, d_attn)), const((d_attn, d_attn)),
            const(sc_w.shape), const(lcw.shape), const((1, d_lru)),
            const((d_lru, d_lru)), const((1, d_lru)), const((d_lru, d_lru)), const((1, d_lru)),
            const((1, d_lru)), const((1, d_conv + d_lru)),
        ],
        out_specs=[
            pl.BlockSpec((1, n_blk, d_attn, MOBA_BLOCK), lambda b, s: (b, s, 0, 0)),
            pl.BlockSpec((1, tm, d_attn), lambda b, s: (b, s, 0)),
            pl.BlockSpec((1, n_blk, d_attn, MOBA_BLOCK), lambda b, s: (b, s, 0, 0)),
            pl.BlockSpec((1, n_blk, 1, d_attn), lambda b, s: (b, s, 0, 0)),
            pl.BlockSpec((1, tm, d_conv + d_lru), lambda b, s: (b, s, 0)),
        ],
        out_shape=[
            jax.ShapeDtypeStruct((batch, nb, d_attn, MOBA_BLOCK), BF16),
            jax.ShapeDtypeStruct((batch, seq, d_attn), BF16),
            jax.ShapeDtypeStruct((batch, nb, d_attn, MOBA_BLOCK), BF16),
            jax.ShapeDtypeStruct((batch, nb, 1, d_attn), F32),
            jax.ShapeDtypeStruct((batch, seq, d_conv + d_lru), BF16),
        ],
        scratch_shapes=[
            pltpu.VMEM((HALO + tm, d_conv), F32),
            pltpu.VMEM((HALO + tm, d_lru), F32),
            pltpu.VMEM((1, d_lru), F32),
        ],
        compiler_params=pltpu.CompilerParams(
            dimension_semantics=("arbitrary", "arbitrary"), vmem_limit_bytes=VMEM_LIMIT_BYTES),
        name="mixer_in",
    )(x, mod, ln1_g, w_in_b, qg, kg, headsum, sc_w, lcw, lcb, wa_bd, ba, wx_bd, bx, lam, mng_cl)


def _sublane_fold(x, op):
    tiles = [x[r:r + SUBLANES] for r in range(0, x.shape[0], SUBLANES)]
    while len(tiles) > 1:
        tiles = [op(a, b) for a, b in zip(tiles[0::2], tiles[1::2])] + tiles[len(tiles) & ~1:]
    return tiles[0]


def _moba_kernel(qT_ref, k_ref, vT_ref, kmean_ref, g_ref, o_ref,
                 qcat_ref, bias_ref, s_ref, mx_ref, m_ref, acc_ref):
    g = pl.program_id(1)
    blk = MOBA_BLOCK
    qw = QUERY_BLOCKS * blk
    d_attn = qT_ref.shape[2]
    n_heads = d_attn // HEAD_DIM
    n_pairs = n_heads // 2
    nb = kmean_ref.shape[1]
    pair = 2 * HEAD_DIM
    cols = n_heads * qw
    first_blk = QUERY_BLOCKS * g

    half = lax.broadcasted_iota(jnp.int32, (pair, qw), 0) < HEAD_DIM
    kmean = kmean_ref[0, :, 0, :]

    def query_block_of(shape, axis):
        return (lax.broadcasted_iota(jnp.int32, shape, axis) & (qw - 1)) // blk

    past_q = lax.broadcasted_iota(jnp.int32, (nb, qw), 0) < first_blk + query_block_of((nb, qw), 1)
    for p in range(n_pairs):
        p0 = p * pair
        q_pair = jnp.concatenate([qT_ref[0, c, p0:p0 + pair, :] for c in range(QUERY_BLOCKS)], axis=1)
        zero = jnp.zeros_like(q_pair)
        for hh in range(2):
            hd = 2 * p + hh
            q_m = jnp.where(half if hh == 0 else jnp.logical_not(half), q_pair, zero)
            qcat_ref[p, :, hh * qw:(hh + 1) * qw] = q_m
            gate = _dot(kmean[:, p0:p0 + pair].astype(BF16), q_m)
            bias_ref[0:nb, hd * qw:(hd + 1) * qw] = jnp.where(past_q, gate, -jnp.inf)

    ones_rows = jnp.where(lax.broadcasted_iota(jnp.int32, (ACC_ROWS - HEAD_DIM, blk), 0) == 0,
                          1.0, 0.0).astype(BF16)
    key_pos = lax.broadcasted_iota(jnp.int32, (blk, qw), 0)
    qry_pos = lax.broadcasted_iota(jnp.int32, (blk, qw), 1)

    def score_parts(slot, key_blk, bias_row, diag):
        rows = pl.ds(pl.multiple_of(key_blk * blk, blk), blk)
        lo = 0 if diag is None else diag * blk
        if diag is not None:
            visible = (key_pos + diag * blk <= qry_pos)[:, lo:]

        def head(hd):
            p, hh = divmod(hd, 2)
            c0 = hd * qw + lo
            c1 = (hd + 1) * qw
            s_t = _dot(k_ref[0, rows, p * pair:(p + 1) * pair], qcat_ref[p, :, hh * qw + lo:(hh + 1) * qw])
            if diag is not None:
                s_t = jnp.where(visible, s_t, -jnp.inf)
            s_ref[slot, :, c0:c1] = s_t
            if bias_row is None:
                mx_ref[:, c0:c1] = _sublane_fold(s_t, jnp.maximum)
                return
            mx8 = jnp.maximum(mx_ref[:, c0:c1],
                              _sublane_fold(s_t, jnp.maximum) + bias_ref[pl.ds(bias_row, 1), c0:c1])
            mx_ref[:, c0:c1] = mx8
            m_ref[slot, :, c0:c1] = jnp.maximum(jnp.max(mx8, axis=0, keepdims=True), M_FLOOR)

        return [functools.partial(head, hd) for hd in range(n_heads)]

    def value_parts(slot, m_prev, key_blk, bias_row, lo=0):
        m_cur = m_ref[slot]
        alpha = jnp.exp2(m_prev - m_cur)
        m_eff = m_cur - bias_ref[pl.ds(bias_row, 1), :]

        def head(hd):
            r0 = hd * HEAD_DIM
            c0 = hd * qw + lo
            c1 = (hd + 1) * qw
            p_t = jnp.exp2(s_ref[slot, :, c0:c1] - m_eff[:, c0:c1])
            v_t = jnp.concatenate([vT_ref[0, key_blk, r0:r0 + HEAD_DIM, :], ones_rows], axis=0)
            a0 = hd * ACC_ROWS
            acc_ref[a0:a0 + ACC_ROWS, lo:] = (alpha[:, c0:c1] * acc_ref[a0:a0 + ACC_ROWS, lo:]
                                              + _dot(v_t, p_t.astype(BF16)))

        return [functools.partial(head, hd) for hd in range(n_heads)]

    def interleave(scores, values):
        for score, value in zip(scores, values):
            score()
            value()

    for score in score_parts(0, first_blk, None, 0):
        score()

    gate = bias_ref[0:nb, :]
    blk_i = lax.broadcasted_iota(jnp.int32, (nb, cols), 0)
    blk_f = blk_i.astype(F32)
    keep = jnp.zeros((nb, cols), F32)
    for _ in range(MOBA_TOPK):
        top = jnp.max(gate, axis=0, keepdims=True)
        first = jnp.min(jnp.where(gate == top, blk_f, float(nb)), axis=0, keepdims=True)
        hit = blk_f == first
        keep = jnp.where(hit, 1.0, keep)
        gate = jnp.where(hit, -jnp.inf, gate)
    keep = jnp.where(blk_i < first_blk + query_block_of((nb, cols), 1), keep, 0.0)
    bias_ref[0:nb, :] = jnp.where(keep > 0.0, 0.0, -jnp.inf)
    col_c = query_block_of((1, cols), 1)
    for c in range(QUERY_BLOCKS):
        bias_ref[nb + c:nb + c + 1, :] = jnp.where(col_c > c, bias_ref[pl.ds(first_blk + c, 1), :], 0.0)

    mx8 = mx_ref[...] + bias_ref[nb:nb + 1, :]
    mx_ref[...] = mx8
    m_first = jnp.maximum(jnp.max(mx8, axis=0, keepdims=True), M_FLOOR)
    m_ref[0] = m_first
    m_ref[1] = m_first

    acc_ref[...] = jnp.zeros_like(acc_ref)

    def step_blocks(t):
        return jnp.where(t == 0, first_blk, t - 1), jnp.where(t == 0, nb, t - 1)

    def fused_step(t, cur):
        m_prev = m_ref[cur]
        interleave(score_parts(cur, t - 1, t - 1, None), value_parts(1 - cur, m_prev, *step_blocks(t - 1)))

    def two_steps(u):
        for cur in (1, 0):
            fused_step(2 * u + 2 - cur, cur)

    def loop_body(v, carry):
        two_steps(2 * v)
        two_steps(2 * v + 1)
        return carry

    step_pairs = first_blk // 2
    lax.fori_loop(0, step_pairs // 2, loop_body, 0)

    @pl.when((step_pairs & 1) == 1)
    def _():
        two_steps(step_pairs - 1)

    for c in range(1, QUERY_BLOCKS):
        cur = c & 1
        m_prev = m_ref[cur]
        if c == 1:
            previous = value_parts(1 - cur, m_prev, *step_blocks(first_blk))
        else:
            previous = value_parts(1 - cur, m_prev, first_blk + c - 1, nb + c - 1, lo=(c - 1) * blk)
        interleave(score_parts(cur, first_blk + c, nb + c, c), previous)
    last = QUERY_BLOCKS - 1
    for value in value_parts(last & 1, m_ref[1 - (last & 1)], first_blk + last, nb + last, lo=last * blk):
        value()

    y_t = jnp.concatenate(
        [acc_ref[hd * ACC_ROWS:hd * ACC_ROWS + HEAD_DIM, :]
         / acc_ref[hd * ACC_ROWS + HEAD_DIM:hd * ACC_ROWS + HEAD_DIM + 1, :]
         for hd in range(n_heads)], axis=0)
    y_t = y_t * _rms(y_t, axis=0)
    o_ref[0] = (y_t.T * g_ref[...]).astype(BF16)


def _moba(qT, k, vT, kmean, mng_attn):
    batch, nb, d_attn, blk = qT.shape
    seq = k.shape[1]
    n_heads = d_attn // HEAD_DIM
    qw = QUERY_BLOCKS * blk
    cols = n_heads * qw
    assert QUERY_BLOCKS >= 2 and QUERY_BLOCKS % 2 == 0 and nb % QUERY_BLOCKS == 0

    return pl.pallas_call(
        _moba_kernel,
        grid=(batch, nb // QUERY_BLOCKS),
        in_specs=[
            pl.BlockSpec((1, QUERY_BLOCKS, d_attn, blk), lambda b, g: (b, g, 0, 0)),
            pl.BlockSpec((1, seq, d_attn), lambda b, g: (b, 0, 0)),
            pl.BlockSpec((1, nb, d_attn, blk), lambda b, g: (b, 0, 0, 0)),
            pl.BlockSpec((1, nb, 1, d_attn), lambda b, g: (b, 0, 0, 0)),
            pl.BlockSpec((1, d_attn), lambda b, g: (0, 0)),
        ],
        out_specs=pl.BlockSpec((1, qw, d_attn), lambda b, g: (b, g, 0)),
        out_shape=jax.ShapeDtypeStruct((batch, seq, d_attn), BF16),
        scratch_shapes=[
            pltpu.VMEM((n_heads // 2, 2 * HEAD_DIM, 2 * qw), BF16),
            pltpu.VMEM((nb + SUBLANES, cols), F32),
            pltpu.VMEM((2, blk, cols), F32),
            pltpu.VMEM((SUBLANES, cols), F32),
            pltpu.VMEM((2, 1, cols), F32),
            pltpu.VMEM((n_heads * ACC_ROWS, qw), F32),
        ],
        compiler_params=pltpu.CompilerParams(
            dimension_semantics=("arbitrary", "arbitrary"), vmem_limit_bytes=VMEM_LIMIT_BYTES),
        name="moba_attention",
    )(qT, k, vT, kmean, mng_attn)


def _out_mlp_kernel(x_ref, ya_ref, ycl_ref, mod_ref, ln2_ref, wout_ref, wup_ref, wdown_ref, o_ref):
    d_attn = ya_ref.shape[2]
    d_ff = wup_ref.shape[2]
    x = x_ref[0]
    m = mod_ref[0, 0]
    gate1, shift2, scale2, gate2 = m[2:3], m[3:4], m[4:5], m[5:6]
    mix = _dot(ya_ref[0], wout_ref[0, 0:d_attn, :]) + _dot(ycl_ref[0], wout_ref[0, d_attn:, :])
    x1 = x + gate1 * mix
    h2 = ((x1 * _rms(x1) * ln2_ref[...]) * (1.0 + scale2) + shift2).astype(BF16)
    ff = jnp.zeros_like(x1)
    for c0 in range(0, d_ff, FF_CHUNK):
        up = _dot(h2, wup_ref[0, :, c0:c0 + FF_CHUNK])
        act = jnp.square(jnp.maximum(up, 0.0)).astype(BF16)
        ff = ff + _dot(act, wdown_ref[0, c0:c0 + FF_CHUNK, :])
    o_ref[0] = x1 + gate2 * ff


def _out_mlp(layer, x, ya, ycl, mod, ln2_g, w_out_b, w_up_b, w_down_b):
    batch, seq, d_model = x.shape
    tm = MLP_TOKEN_TILE
    d_attn = ya.shape[2]
    d_cl = ycl.shape[2]

    def layer_weight(w):
        return pl.BlockSpec((1,) + w.shape[1:], lambda b, s: (layer, 0, 0), pipeline_mode=pl.Buffered(1))

    return pl.pallas_call(
        _out_mlp_kernel,
        grid=(batch, seq // tm),
        in_specs=[
            pl.BlockSpec((1, tm, d_model), lambda b, s: (b, s, 0)),
            pl.BlockSpec((1, tm, d_attn), lambda b, s: (b, s, 0)),
            pl.BlockSpec((1, tm, d_cl), lambda b, s: (b, s, 0)),
            pl.BlockSpec((1, 1, N_MOD, d_model), lambda b, s: (layer, b, 0, 0)),
            pl.BlockSpec((1, d_model), lambda b, s: (0, 0)),
            layer_weight(w_out_b), layer_weight(w_up_b), layer_weight(w_down_b),
        ],
        out_specs=pl.BlockSpec((1, tm, d_model), lambda b, s: (b, s, 0)),
        out_shape=jax.ShapeDtypeStruct((batch, seq, d_model), F32),
        compiler_params=pltpu.CompilerParams(
            dimension_semantics=("arbitrary", "arbitrary"), vmem_limit_bytes=VMEM_LIMIT_BYTES),
        name="out_mlp",
    )(x, ya, ycl, mod, ln2_g, w_out_b, w_up_b, w_down_b)


def _block_diag(w):
    n, r, c = w.shape
    eye = jnp.eye(n, dtype=w.dtype)
    return (eye[:, None, :, None] * w[:, :, None, :]).reshape(n * r, n * c)


def kernel(x, c, ln1_g, ln2_g, w_ada, b_ada, w_in, q_norm_g, k_norm_g, sc_w, lru_conv_w, lru_conv_b,
           lru_wa, lru_ba, lru_wx, lru_bx, lru_lambda, mix_norm_g, w_out, w_up, w_down):
    batch, seq, d_model = x.shape
    depth = w_in.shape[0]
    d_conv = sc_w.shape[2]
    d_lru = lru_conv_w.shape[2]
    d_attn = mix_norm_g.shape[1] - d_conv - d_lru
    n_heads = d_attn // HEAD_DIM
    assert seq % TOKEN_TILE == 0 and TOKEN_TILE % MOBA_BLOCK == 0 and seq % MLP_TOKEN_TILE == 0
    assert w_in.shape[2] == 3 * d_attn + 3 * d_conv + 2 * d_lru

    mod = _modulation(c, w_ada, b_ada).reshape(depth, batch, N_MOD, d_model)
    headsum = _block_diag(jnp.full((n_heads, HEAD_DIM, HEAD_DIM), 1.0 / HEAD_DIM, F32)).astype(BF16)

    w_in_b, w_out_b, w_up_b, w_down_b = (w.astype(BF16) for w in (w_in, w_out, w_up, w_down))
    for l in range(depth):
        row = lambda v: v.reshape(1, -1)
        qT, k, vT, kmean, ycl = _mixer_in(
            l, x, mod, row(ln1_g[l]), w_in_b,
            row(jnp.tile(q_norm_g[l], n_heads)), row(jnp.tile(k_norm_g[l], n_heads)), headsum,
            sc_w[l], lru_conv_w[l], row(lru_conv_b[l]),
            _block_diag(lru_wa[l]).astype(BF16), row(lru_ba[l]),
            _block_diag(lru_wx[l]).astype(BF16), row(lru_bx[l]),
            row(lru_lambda[l]), row(mix_norm_g[l, d_attn:]),
            d_attn=d_attn, d_conv=d_conv, d_lru=d_lru)
        ya = _moba(qT, k, vT, kmean, row(mix_norm_g[l, :d_attn]))
        x = _out_mlp(l, x, ya, ycl, mod, row(ln2_g[l]), w_out_b, w_up_b, w_down_b)
    return x
```

```python
import functools
import math

import jax
import jax.numpy as jnp
from jax import lax
from jax.experimental import pallas as pl
from jax.experimental.pallas import tpu as pltpu

F32 = jnp.float32
BF16 = jnp.bfloat16

HEAD_DIM = 64
MOBA_BLOCK = 256
MOBA_TOPK = 3
LRU_C = 8.0
N_MOD = 6
EPS = 1e-6

TOKEN_TILE = 512
MLP_TOKEN_TILE = 1024
MOD_COL_TILE = 1536
FF_CHUNK = 1024
SUBLANES = 8
HALO = SUBLANES
ACC_ROWS = HEAD_DIM + 2 * SUBLANES
LOG2_E = math.log2(math.e)
QUERY_BLOCKS = 2
M_FLOOR = -1e30
VMEM_LIMIT_BYTES = 56 * 1024 * 1024


def _rms(x, axis=-1):
    return lax.rsqrt(jnp.mean(x * x, axis=axis, keepdims=True) + EPS)


def _dot(a, b):
    return jnp.dot(a, b, preferred_element_type=F32)


def _mod_kernel(c_ref, w_ref, b_ref, o_ref):
    c = c_ref[...]
    c_act = (c * jax.nn.sigmoid(c)).astype(BF16)
    o_ref[0] = _dot(c_act, w_ref[0].astype(BF16)) + b_ref[0]


def _modulation(c, w_ada, b_ada):
    depth, d_model, n_out = w_ada.shape
    batch = c.shape[0]
    return pl.pallas_call(
        _mod_kernel,
        grid=(depth, n_out // MOD_COL_TILE),
        in_specs=[
            pl.BlockSpec((batch, d_model), lambda l, j: (0, 0)),
            pl.BlockSpec((1, d_model, MOD_COL_TILE), lambda l, j: (l, 0, j)),
            pl.BlockSpec((1, 1, MOD_COL_TILE), lambda l, j: (l, 0, j)),
        ],
        out_specs=pl.BlockSpec((1, batch, MOD_COL_TILE), lambda l, j: (l, 0, j)),
        out_shape=jax.ShapeDtypeStruct((depth, batch, n_out), F32),
        compiler_params=pltpu.CompilerParams(
            dimension_semantics=("arbitrary", "arbitrary"), vmem_limit_bytes=VMEM_LIMIT_BYTES),
        name="adaln_modulation",
    )(c, w_ada, b_ada.reshape(depth, 1, n_out))


def _scan_linear_recurrence(a, u):
    n = a.shape[0]
    row = lax.broadcasted_iota(jnp.int32, a.shape, 0)
    d = 1
    while d < n:
        keep = row >= d
        a_prev = jnp.where(keep, pltpu.roll(a, d, 0), 1.0)
        u_prev = jnp.where(keep, pltpu.roll(u, d, 0), 0.0)
        u = a * u_prev + u
        a = a * a_prev
        d *= 2
    return a, u


def _gelu_tanh(x):
    return 0.5 * x * (1.0 + jnp.tanh(math.sqrt(2.0 / math.pi) * (x + 0.044715 * (x * x * x))))


def _softplus(z):
    return jnp.maximum(z, 0.0) + jnp.log1p(jnp.exp(-jnp.abs(z)))


def _mixer_in_kernel(x_ref, mod_ref, ln1_ref, w_in_ref, qg_ref, kg_ref, headsum_ref, scw_ref,
                     lcw_ref, lcb_ref, wa_ref, ba_ref, wx_ref, bx_ref, lam_ref, mng_ref,
                     qT_ref, k_ref, vT_ref, kmean_ref, ycl_ref,
                     cu_buf, lx_buf, h_carry, *, d_attn, d_conv, d_lru):
    s = pl.program_id(1)
    tm = x_ref.shape[1]
    n_blk = tm // MOBA_BLOCK

    @pl.when(s == 0)
    def _():
        cu_buf[0:HALO, :] = jnp.zeros((HALO, d_conv), F32)
        lx_buf[0:HALO, :] = jnp.zeros((HALO, d_lru), F32)
        h_carry[...] = jnp.zeros_like(h_carry)

    x = x_ref[0]
    m = mod_ref[0, 0]
    shift1, scale1 = m[0:1], m[1:2]
    h = ((x * _rms(x) * ln1_ref[...]) * (1.0 + scale1) + shift1).astype(BF16)

    def proj(off, width):
        return _dot(h, w_in_ref[0, :, off:off + width])

    o_q, o_k, o_v = 0, d_attn, 2 * d_attn
    o_b = 3 * d_attn
    o_c, o_u = o_b + d_conv, o_b + 2 * d_conv
    o_lx = o_b + 3 * d_conv
    o_lg = o_lx + d_lru

    def head_norm(t, g_ref):
        ms = _dot((t * t).astype(BF16), headsum_ref[...])
        return t * lax.rsqrt(ms + EPS) * g_ref[...]

    q = head_norm(proj(o_q, d_attn), qg_ref)
    qT = (q * (LOG2_E / math.sqrt(HEAD_DIM))).T.astype(BF16)
    for c in range(n_blk):
        qT_ref[0, c] = qT[:, c * MOBA_BLOCK:(c + 1) * MOBA_BLOCK]

    k = head_norm(proj(o_k, d_attn), kg_ref)
    k_ref[0] = k.astype(BF16)
    for c in range(n_blk):
        kmean_ref[0, c] = jnp.mean(k[c * MOBA_BLOCK:(c + 1) * MOBA_BLOCK], axis=0, keepdims=True)

    vT = proj(o_v, d_attn).T.astype(BF16)
    for c in range(n_blk):
        vT_ref[0, c] = vT[:, c * MOBA_BLOCK:(c + 1) * MOBA_BLOCK]

    sc_b = proj(o_b, d_conv)
    cu = proj(o_c, d_conv) * proj(o_u, d_conv)
    cu_buf[HALO:HALO + tm, :] = cu
    scw = scw_ref[...]
    conv = (scw[0:1] * cu_buf[HALO - 2:HALO - 2 + tm, :]
            + scw[1:2] * cu_buf[HALO - 1:HALO - 1 + tm, :]
            + scw[2:3] * cu)
    cu_buf[0:HALO, :] = cu[tm - HALO:tm]
    y_conv = sc_b * conv

    lx = proj(o_lx, d_lru)
    lx_buf[HALO:HALO + tm, :] = lx
    lcw = lcw_ref[...]
    xr = (lcw[0:1] * lx_buf[HALO - 3:HALO - 3 + tm, :]
          + lcw[1:2] * lx_buf[HALO - 2:HALO - 2 + tm, :]
          + lcw[2:3] * lx_buf[HALO - 1:HALO - 1 + tm, :]
          + lcw[3:4] * lx) + lcb_ref[...]
    lx_buf[0:HALO, :] = lx[tm - HALO:tm]
    xr_b = xr.astype(BF16)
    r = jax.nn.sigmoid(_dot(xr_b, wa_ref[...]) + ba_ref[...])
    i = jax.nn.sigmoid(_dot(xr_b, wx_ref[...]) + bx_ref[...])
    log_a = (-LRU_C) * r * _softplus(-lam_ref[...])
    a = jnp.exp(log_a)
    t = jnp.tanh(log_a)
    u = jnp.sqrt((-2.0 * t) / (1.0 - t)) * (i * xr)
    a_cum, h_loc = _scan_linear_recurrence(a, u)
    hs = h_loc + a_cum * h_carry[...]
    h_carry[...] = hs[tm - 1:tm]
    y_lru = hs * _gelu_tanh(proj(o_lg, d_lru))

    mng = mng_ref[...]
    ycl_ref[0, :, 0:d_conv] = (y_conv * _rms(y_conv) * mng[:, 0:d_conv]).astype(BF16)
    ycl_ref[0, :, d_conv:d_conv + d_lru] = (y_lru * _rms(y_lru) * mng[:, d_conv:]).astype(BF16)


def _mixer_in(layer, x, mod, ln1_g, w_in_b, qg, kg, headsum, sc_w, lcw, lcb, wa_bd, ba, wx_bd, bx, lam, mng_cl,
              *, d_attn, d_conv, d_lru):
    batch, seq, d_model = x.shape
    tm = TOKEN_TILE
    n_blk = tm // MOBA_BLOCK
    nb = seq // MOBA_BLOCK

    def const(shape):
        return pl.BlockSpec(shape, lambda b, s: (0,) * len(shape))

    kern = functools.partial(_mixer_in_kernel, d_attn=d_attn, d_conv=d_conv, d_lru=d_lru)
    return pl.pallas_call(
        kern,
        grid=(batch, seq // tm),
        in_specs=[
            pl.BlockSpec((1, tm, d_model), lambda b, s: (b, s, 0)),
            pl.BlockSpec((1, 1, N_MOD, d_model), lambda b, s: (layer, b, 0, 0)),
            const((1, d_model)),
            pl.BlockSpec((1,) + w_in_b.shape[1:], lambda b, s: (layer, 0, 0)),
            const((1, d_attn)), const((1, d_attn)), const((d_attn, d_attn)),
            const(sc_w.shape), const(lcw.shape), const((1, d_lru)),
            const((d_lru, d_lru)), const((1, d_lru)), const((d_lru, d_lru)), const((1, d_lru)),
            const((1, d_lru)), const((1, d_conv + d_lru)),
        ],
        out_specs=[
            pl.BlockSpec((1, n_blk, d_attn, MOBA_BLOCK), lambda b, s: (b, s, 0, 0)),
            pl.BlockSpec((1, tm, d_attn), lambda b, s: (b, s, 0)),
            pl.BlockSpec((1, n_blk, d_attn, MOBA_BLOCK), lambda b, s: (b, s, 0, 0)),
            pl.BlockSpec((1, n_blk, 1, d_attn), lambda b, s: (b, s, 0, 0)),
            pl.BlockSpec((1, tm, d_conv + d_lru), lambda b, s: (b, s, 0)),
        ],
        out_shape=[
            jax.ShapeDtypeStruct((batch, nb, d_attn, MOBA_BLOCK), BF16),
            jax.ShapeDtypeStruct((batch, seq, d_attn), BF16),
            jax.ShapeDtypeStruct((batch, nb, d_attn, MOBA_BLOCK), BF16),
            jax.ShapeDtypeStruct((batch, nb, 1, d_attn), F32),
            jax.ShapeDtypeStruct((batch, seq, d_conv + d_lru), BF16),
        ],
        scratch_shapes=[
            pltpu.VMEM((HALO + tm, d_conv), F32),
            pltpu.VMEM((HALO + tm, d_lru), F32),
            pltpu.VMEM((1, d_lru), F32),
        ],
        compiler_params=pltpu.CompilerParams(
            dimension_semantics=("arbitrary", "arbitrary"), vmem_limit_bytes=VMEM_LIMIT_BYTES),
        name="mixer_in",
    )(x, mod, ln1_g, w_in_b, qg, kg, headsum, sc_w, lcw, lcb, wa_bd, ba, wx_bd, bx, lam, mng_cl)


def _sublane_fold(x, op):
    tiles = [x[r:r + SUBLANES] for r in range(0, x.shape[0], SUBLANES)]
    while len(tiles) > 1:
        tiles = [op(a, b) for a, b in zip(tiles[0::2], tiles[1::2])] + tiles[len(tiles) & ~1:]
    return tiles[0]


def _moba_kernel(qT_ref, k_ref, vT_ref, kmean_ref, g_ref, o_ref,
                 qcat_ref, bias_ref, s_ref, mx_ref, m_ref, acc_ref):
    g = pl.program_id(1)
    blk = MOBA_BLOCK
    qw = QUERY_BLOCKS * blk
    d_attn = qT_ref.shape[2]
    n_heads = d_attn // HEAD_DIM
    n_pairs = n_heads // 2
    nb = kmean_ref.shape[1]
    pair = 2 * HEAD_DIM
    cols = n_heads * qw
    first_blk = QUERY_BLOCKS * g

    half = lax.broadcasted_iota(jnp.int32, (pair, qw), 0) < HEAD_DIM
    kmean = kmean_ref[0, :, 0, :]

    def query_block_of(shape, axis):
        return (lax.broadcasted_iota(jnp.int32, shape, axis) & (qw - 1)) // blk

    past_q = lax.broadcasted_iota(jnp.int32, (nb, qw), 0) < first_blk + query_block_of((nb, qw), 1)
    for p in range(n_pairs):
        p0 = p * pair
        q_pair = jnp.concatenate([qT_ref[0, c, p0:p0 + pair, :] for c in range(QUERY_BLOCKS)], axis=1)
        zero = jnp.zeros_like(q_pair)
        for hh in range(2):
            hd = 2 * p + hh
            q_m = jnp.where(half if hh == 0 else jnp.logical_not(half), q_pair, zero)
            qcat_ref[p, :, hh * qw:(hh + 1) * qw] = q_m
            gate = _dot(kmean[:, p0:p0 + pair].astype(BF16), q_m)
            bias_ref[0:nb, hd * qw:(hd + 1) * qw] = jnp.where(past_q, gate, -jnp.inf)

    ones_rows = jnp.where(lax.broadcasted_iota(jnp.int32, (ACC_ROWS - HEAD_DIM, blk), 0) == 0,
                          1.0, 0.0).astype(BF16)
    key_pos = lax.broadcasted_iota(jnp.int32, (blk, qw), 0)
    qry_pos = lax.broadcasted_iota(jnp.int32, (blk, qw), 1)

    def score_parts(slot, key_blk, bias_row, diag):
        rows = pl.ds(pl.multiple_of(key_blk * blk, blk), blk)
        lo = 0 if diag is None else diag * blk
        if diag is not None:
            visible = (key_pos + diag * blk <= qry_pos)[:, lo:]

        def head(hd):
            p, hh = divmod(hd, 2)
            c0 = hd * qw + lo
            c1 = (hd + 1) * qw
            s_t = _dot(k_ref[0, rows, p * pair:(p + 1) * pair], qcat_ref[p, :, hh * qw + lo:(hh + 1) * qw])
            if diag is not None:
                s_t = jnp.where(visible, s_t, -jnp.inf)
            s_ref[slot, :, c0:c1] = s_t
            if bias_row is None:
                mx_ref[:, c0:c1] = _sublane_fold(s_t, jnp.maximum)
                return
            mx8 = jnp.maximum(mx_ref[:, c0:c1],
                              _sublane_fold(s_t, jnp.maximum) + bias_ref[pl.ds(bias_row, 1), c0:c1])
            mx_ref[:, c0:c1] = mx8
            m_ref[slot, :, c0:c1] = jnp.maximum(jnp.max(mx8, axis=0, keepdims=True), M_FLOOR)

        return [functools.partial(head, hd) for hd in range(n_heads)]

    def value_parts(slot, m_prev, key_blk, bias_row, lo=0):
        m_cur = m_ref[slot]
        alpha = jnp.exp2(m_prev - m_cur)
        m_eff = m_cur - bias_ref[pl.ds(bias_row, 1), :]

        def head(hd):
            r0 = hd * HEAD_DIM
            c0 = hd * qw + lo
            c1 = (hd + 1) * qw
            p_t = jnp.exp2(s_ref[slot, :, c0:c1] - m_eff[:, c0:c1])
            v_t = jnp.concatenate([vT_ref[0, key_blk, r0:r0 + HEAD_DIM, :], ones_rows], axis=0)
            a0 = hd * ACC_ROWS
            acc_ref[a0:a0 + ACC_ROWS, lo:] = (alpha[:, c0:c1] * acc_ref[a0:a0 + ACC_ROWS, lo:]
                                              + _dot(v_t, p_t.astype(BF16)))

        return [functools.partial(head, hd) for hd in range(n_heads)]

    def interleave(scores, values):
        for score, value in zip(scores, values):
            score()
            value()

    for score in score_parts(0, first_blk, None, 0):
        score()

    gate = bias_ref[0:nb, :]
    blk_i = lax.broadcasted_iota(jnp.int32, (nb, cols), 0)
    blk_f = blk_i.astype(F32)
    keep = jnp.zeros((nb, cols), F32)
    for _ in range(MOBA_TOPK):
        top = jnp.max(gate, axis=0, keepdims=True)
        first = jnp.min(jnp.where(gate == top, blk_f, float(nb)), axis=0, keepdims=True)
        hit = blk_f == first
        keep = jnp.where(hit, 1.0, keep)
        gate = jnp.where(hit, -jnp.inf, gate)
    keep = jnp.where(blk_i < first_blk + query_block_of((nb, cols), 1), keep, 0.0)
    bias_ref[0:nb, :] = jnp.where(keep > 0.0, 0.0, -jnp.inf)
    col_c = query_block_of((1, cols), 1)
    for c in range(QUERY_BLOCKS):
        bias_ref[nb + c:nb + c + 1, :] = jnp.where(col_c > c, bias_ref[pl.ds(first_blk + c, 1), :], 0.0)

    mx8 = mx_ref[...] + bias_ref[nb:nb + 1, :]
    mx_ref[...] = mx8
    m_first = jnp.maximum(jnp.max(mx8, axis=0, keepdims=True), M_FLOOR)
    m_ref[0] = m_first
    m_ref[1] = m_first

    acc_ref[...] = jnp.zeros_like(acc_ref)

    def step_blocks(t):
        return jnp.where(t == 0, first_blk, t - 1), jnp.where(t == 0, nb, t - 1)

    def fused_step(t, cur):
        m_prev = m_ref[cur]
        interleave(score_parts(cur, t - 1, t - 1, None), value_parts(1 - cur, m_prev, *step_blocks(t - 1)))

    def two_steps(u):
        for cur in (1, 0):
            fused_step(2 * u + 2 - cur, cur)

    def loop_body(v, carry):
        two_steps(2 * v)
        two_steps(2 * v + 1)
        return carry

    step_pairs = first_blk // 2
    lax.fori_loop(0, step_pairs // 2, loop_body, 0)

    @pl.when((step_pairs & 1) == 1)
    def _():
        two_steps(step_pairs - 1)

    for c in range(1, QUERY_BLOCKS):
        cur = c & 1
        m_prev = m_ref[cur]
        if c == 1:
            previous = value_parts(1 - cur, m_prev, *step_blocks(first_blk))
        else:
            previous = value_parts(1 - cur, m_prev, first_blk + c - 1, nb + c - 1, lo=(c - 1) * blk)
        interleave(score_parts(cur, first_blk + c, nb + c, c), previous)
    last = QUERY_BLOCKS - 1
    for value in value_parts(last & 1, m_ref[1 - (last & 1)], first_blk + last, nb + last, lo=last * blk):
        value()

    y_t = jnp.concatenate(
        [acc_ref[hd * ACC_ROWS:hd * ACC_ROWS + HEAD_DIM, :]
         / acc_ref[hd * ACC_ROWS + HEAD_DIM:hd * ACC_ROWS + HEAD_DIM + 1, :]
         for hd in range(n_heads)], axis=0)
    y_t = y_t * _rms(y_t, axis=0)
    o_ref[0] = (y_t.T * g_ref[...]).astype(BF16)


def _moba(qT, k, vT, kmean, mng_attn):
    batch, nb, d_attn, blk = qT.shape
    seq = k.shape[1]
    n_heads = d_attn // HEAD_DIM
    qw = QUERY_BLOCKS * blk
    cols = n_heads * qw
    assert QUERY_BLOCKS >= 2 and QUERY_BLOCKS % 2 == 0 and nb % QUERY_BLOCKS == 0

    return pl.pallas_call(
        _moba_kernel,
        grid=(batch, nb // QUERY_BLOCKS),
        in_specs=[
            pl.BlockSpec((1, QUERY_BLOCKS, d_attn, blk), lambda b, g: (b, g, 0, 0)),
            pl.BlockSpec((1, seq, d_attn), lambda b, g: (b, 0, 0)),
            pl.BlockSpec((1, nb, d_attn, blk), lambda b, g: (b, 0, 0, 0)),
            pl.BlockSpec((1, nb, 1, d_attn), lambda b, g: (b, 0, 0, 0)),
            pl.BlockSpec((1, d_attn), lambda b, g: (0, 0)),
        ],
        out_specs=pl.BlockSpec((1, qw, d_attn), lambda b, g: (b, g, 0)),
        out_shape=jax.ShapeDtypeStruct((batch, seq, d_attn), BF16),
        scratch_shapes=[
            pltpu.VMEM((n_heads // 2, 2 * HEAD_DIM, 2 * qw), BF16),
            pltpu.VMEM((nb + SUBLANES, cols), F32),
            pltpu.VMEM((2, blk, cols), F32),
            pltpu.VMEM((SUBLANES, cols), F32),
            pltpu.VMEM((2, 1, cols), F32),
            pltpu.VMEM((n_heads * ACC_ROWS, qw), F32),
        ],
        compiler_params=pltpu.CompilerParams(
            dimension_semantics=("arbitrary", "arbitrary"), vmem_limit_bytes=VMEM_LIMIT_BYTES),
        name="moba_attention",
    )(qT, k, vT, kmean, mng_attn)


def _out_mlp_kernel(x_ref, ya_ref, ycl_ref, mod_ref, ln2_ref, wout_ref, wup_ref, wdown_ref, o_ref):
    d_attn = ya_ref.shape[2]
    d_ff = wup_ref.shape[2]
    x = x_ref[0]
    m = mod_ref[0, 0]
    gate1, shift2, scale2, gate2 = m[2:3], m[3:4], m[4:5], m[5:6]
    mix = _dot(ya_ref[0], wout_ref[0, 0:d_attn, :]) + _dot(ycl_ref[0], wout_ref[0, d_attn:, :])
    x1 = x + gate1 * mix
    h2 = ((x1 * _rms(x1) * ln2_ref[...]) * (1.0 + scale2) + shift2).astype(BF16)
    ff = jnp.zeros_like(x1)
    for c0 in range(0, d_ff, FF_CHUNK):
        up = _dot(h2, wup_ref[0, :, c0:c0 + FF_CHUNK])
        act = jnp.square(jnp.maximum(up, 0.0)).astype(BF16)
        ff = ff + _dot(act, wdown_ref[0, c0:c0 + FF_CHUNK, :])
    o_ref[0] = x1 + gate2 * ff


def _out_mlp(layer, x, ya, ycl, mod, ln2_g, w_out_b, w_up_b, w_down_b):
    batch, seq, d_model = x.shape
    tm = MLP_TOKEN_TILE
    d_attn = ya.shape[2]
    d_cl = ycl.shape[2]

    def layer_weight(w):
        return pl.BlockSpec((1,) + w.shape[1:], lambda b, s: (layer, 0, 0), pipeline_mode=pl.Buffered(1))

    return pl.pallas_call(
        _out_mlp_kernel,
        grid=(batch, seq // tm),
        in_specs=[
            pl.BlockSpec((1, tm, d_model), lambda b, s: (b, s, 0)),
            pl.BlockSpec((1, tm, d_attn), lambda b, s: (b, s, 0)),
            pl.BlockSpec((1, tm, d_cl), lambda b, s: (b, s, 0)),
            pl.BlockSpec((1, 1, N_MOD, d_model), lambda b, s: (layer, b, 0, 0)),
            pl.BlockSpec((1, d_model), lambda b, s: (0, 0)),
            layer_weight(w_out_b), layer_weight(w_up_b), layer_weight(w_down_b),
        ],
        out_specs=pl.BlockSpec((1, tm, d_model), lambda b, s: (b, s, 0)),
        out_shape=jax.ShapeDtypeStruct((batch, seq, d_model), F32),
        compiler_params=pltpu.CompilerParams(
            dimension_semantics=("arbitrary", "arbitrary"), vmem_limit_bytes=VMEM_LIMIT_BYTES),
        name="out_mlp",
    )(x, ya, ycl, mod, ln2_g, w_out_b, w_up_b, w_down_b)


def _block_diag(w):
    n, r, c = w.shape
    eye = jnp.eye(n, dtype=w.dtype)
    return (eye[:, None, :, None] * w[:, :, None, :]).reshape(n * r, n * c)


def kernel(x, c, ln1_g, ln2_g, w_ada, b_ada, w_in, q_norm_g, k_norm_g, sc_w, lru_conv_w, lru_conv_b,
           lru_wa, lru_ba, lru_wx, lru_bx, lru_lambda, mix_norm_g, w_out, w_up, w_down):
    batch, seq, d_model = x.shape
    depth = w_in.shape[0]
    d_conv = sc_w.shape[2]
    d_lru = lru_conv_w.shape[2]
    d_attn = mix_norm_g.shape[1] - d_conv - d_lru
    n_heads = d_attn // HEAD_DIM
    assert seq % TOKEN_TILE == 0 and TOKEN_TILE % MOBA_BLOCK == 0 and seq % MLP_TOKEN_TILE == 0
    assert w_in.shape[2] == 3 * d_attn + 3 * d_conv + 2 * d_lru

    mod = _modulation(c, w_ada, b_ada).reshape(depth, batch, N_MOD, d_model)
    headsum = _block_diag(jnp.full((n_heads, HEAD_DIM, HEAD_DIM), 1.0 / HEAD_DIM, F32)).astype(BF16)

    w_in_b, w_out_b, w_up_b, w_down_b = (w.astype(BF16) for w in (w_in, w_out, w_up, w_down))
    for l in range(depth):
        row = lambda v: v.reshape(1, -1)
        qT, k, vT, kmean, ycl = _mixer_in(
            l, x, mod, row(ln1_g[l]), w_in_b,
            row(jnp.tile(q_norm_g[l], n_heads)), row(jnp.tile(k_norm_g[l], n_heads)), headsum,
            sc_w[l], lru_conv_w[l], row(lru_conv_b[l]),
            _block_diag(lru_wa[l]).astype(BF16), row(lru_ba[l]),
            _block_diag(lru_wx[l]).astype(BF16), row(lru_bx[l]),
            row(lru_lambda[l]), row(mix_norm_g[l, d_attn:]),
            d_attn=d_attn, d_conv=d_conv, d_lru=d_lru)
        ya = _moba(qT, k, vT, kmean, row(mix_norm_g[l, :d_attn]))
        x = _out_mlp(l, x, ya, ycl, mod, row(ln2_g[l]), w_out_b, w_up_b, w_down_b)
    return x
```

```python
import functools
import math

import jax
import jax.numpy as jnp
from jax import lax
from jax.experimental import pallas as pl
from jax.experimental.pallas import tpu as pltpu

F32 = jnp.float32
BF16 = jnp.bfloat16

HEAD_DIM = 64
MOBA_BLOCK = 256
MOBA_TOPK = 3
LRU_C = 8.0
N_MOD = 6
EPS = 1e-6

TOKEN_TILE = 512
MLP_TOKEN_TILE = 1024
MOD_COL_TILE = 1536
FF_CHUNK = 1024
SUBLANES = 8
HALO = SUBLANES
ACC_ROWS = HEAD_DIM + 2 * SUBLANES
LOG2_E = math.log2(math.e)
QUERY_BLOCKS = 2
M_FLOOR = -1e30
SCORE_PAD_LANES = 128
VMEM_LIMIT_BYTES = 56 * 1024 * 1024


def _rms(x, axis=-1):
    return lax.rsqrt(jnp.mean(x * x, axis=axis, keepdims=True) + EPS)


def _dot(a, b):
    return jnp.dot(a, b, preferred_element_type=F32)


def _mod_kernel(c_ref, w_ref, b_ref, o_ref):
    c = c_ref[...]
    c_act = (c * jax.nn.sigmoid(c)).astype(BF16)
    o_ref[0] = _dot(c_act, w_ref[0].astype(BF16)) + b_ref[0]


def _modulation(c, w_ada, b_ada):
    depth, d_model, n_out = w_ada.shape
    batch = c.shape[0]
    return pl.pallas_call(
        _mod_kernel,
        grid=(depth, n_out // MOD_COL_TILE),
        in_specs=[
            pl.BlockSpec((batch, d_model), lambda l, j: (0, 0)),
            pl.BlockSpec((1, d_model, MOD_COL_TILE), lambda l, j: (l, 0, j)),
            pl.BlockSpec((1, 1, MOD_COL_TILE), lambda l, j: (l, 0, j)),
        ],
        out_specs=pl.BlockSpec((1, batch, MOD_COL_TILE), lambda l, j: (l, 0, j)),
        out_shape=jax.ShapeDtypeStruct((depth, batch, n_out), F32),
        compiler_params=pltpu.CompilerParams(
            dimension_semantics=("arbitrary", "arbitrary"), vmem_limit_bytes=VMEM_LIMIT_BYTES),
        name="adaln_modulation",
    )(c, w_ada, b_ada.reshape(depth, 1, n_out))


def _scan_linear_recurrence(a, u):
    n = a.shape[0]
    row = lax.broadcasted_iota(jnp.int32, a.shape, 0)
    d = 1
    while d < n:
        keep = row >= d
        a_prev = jnp.where(keep, pltpu.roll(a, d, 0), 1.0)
        u_prev = jnp.where(keep, pltpu.roll(u, d, 0), 0.0)
        u = a * u_prev + u
        a = a * a_prev
        d *= 2
    return a, u


def _gelu_tanh(x):
    return 0.5 * x * (1.0 + jnp.tanh(math.sqrt(2.0 / math.pi) * (x + 0.044715 * (x * x * x))))


def _softplus(z):
    return jnp.maximum(z, 0.0) + jnp.log1p(jnp.exp(-jnp.abs(z)))


def _mixer_in_kernel(x_ref, mod_ref, ln1_ref, w_in_ref, qg_ref, kg_ref, headsum_ref, scw_ref,
                     lcw_ref, lcb_ref, wa_ref, ba_ref, wx_ref, bx_ref, lam_ref, mng_ref,
                     qT_ref, k_ref, vT_ref, kmean_ref, ycl_ref,
                     cu_buf, lx_buf, h_carry, *, d_attn, d_conv, d_lru):
    s = pl.program_id(1)
    tm = x_ref.shape[1]
    n_blk = tm // MOBA_BLOCK

    @pl.when(s == 0)
    def _():
        cu_buf[0:HALO, :] = jnp.zeros((HALO, d_conv), F32)
        lx_buf[0:HALO, :] = jnp.zeros((HALO, d_lru), F32)
        h_carry[...] = jnp.zeros_like(h_carry)

    x = x_ref[0]
    m = mod_ref[0, 0]
    shift1, scale1 = m[0:1], m[1:2]
    h = ((x * _rms(x) * ln1_ref[...]) * (1.0 + scale1) + shift1).astype(BF16)

    def proj(off, width):
        return _dot(h, w_in_ref[0, :, off:off + width])

    o_q, o_k, o_v = 0, d_attn, 2 * d_attn
    o_b = 3 * d_attn
    o_c, o_u = o_b + d_conv, o_b + 2 * d_conv
    o_lx = o_b + 3 * d_conv
    o_lg = o_lx + d_lru

    def head_norm(t, g_ref):
        ms = _dot((t * t).astype(BF16), headsum_ref[...])
        return t * lax.rsqrt(ms + EPS) * g_ref[...]

    q = head_norm(proj(o_q, d_attn), qg_ref)
    qT = (q * (LOG2_E / math.sqrt(HEAD_DIM))).T.astype(BF16)
    for c in range(n_blk):
        qT_ref[0, c] = qT[:, c * MOBA_BLOCK:(c + 1) * MOBA_BLOCK]

    k = head_norm(proj(o_k, d_attn), kg_ref)
    k_ref[0] = k.astype(BF16)
    for c in range(n_blk):
        kmean_ref[0, c] = jnp.mean(k[c * MOBA_BLOCK:(c + 1) * MOBA_BLOCK], axis=0, keepdims=True)

    vT = proj(o_v, d_attn).T.astype(BF16)
    for c in range(n_blk):
        vT_ref[0, c] = vT[:, c * MOBA_BLOCK:(c + 1) * MOBA_BLOCK]

    sc_b = proj(o_b, d_conv)
    cu = proj(o_c, d_conv) * proj(o_u, d_conv)
    cu_buf[HALO:HALO + tm, :] = cu
    scw = scw_ref[...]
    conv = (scw[0:1] * cu_buf[HALO - 2:HALO - 2 + tm, :]
            + scw[1:2] * cu_buf[HALO - 1:HALO - 1 + tm, :]
            + scw[2:3] * cu)
    cu_buf[0:HALO, :] = cu[tm - HALO:tm]
    y_conv = sc_b * conv

    lx = proj(o_lx, d_lru)
    lx_buf[HALO:HALO + tm, :] = lx
    lcw = lcw_ref[...]
    xr = (lcw[0:1] * lx_buf[HALO - 3:HALO - 3 + tm, :]
          + lcw[1:2] * lx_buf[HALO - 2:HALO - 2 + tm, :]
          + lcw[2:3] * lx_buf[HALO - 1:HALO - 1 + tm, :]
          + lcw[3:4] * lx) + lcb_ref[...]
    lx_buf[0:HALO, :] = lx[tm - HALO:tm]
    xr_b = xr.astype(BF16)
    r = jax.nn.sigmoid(_dot(xr_b, wa_ref[...]) + ba_ref[...])
    i = jax.nn.sigmoid(_dot(xr_b, wx_ref[...]) + bx_ref[...])
    log_a = (-LRU_C) * r * _softplus(-lam_ref[...])
    a = jnp.exp(log_a)
    t = jnp.tanh(log_a)
    u = jnp.sqrt((-2.0 * t) / (1.0 - t)) * (i * xr)
    a_cum, h_loc = _scan_linear_recurrence(a, u)
    hs = h_loc + a_cum * h_carry[...]
    h_carry[...] = hs[tm - 1:tm]
    y_lru = hs * _gelu_tanh(proj(o_lg, d_lru))

    mng = mng_ref[...]
    ycl_ref[0, :, 0:d_conv] = (y_conv * _rms(y_conv) * mng[:, 0:d_conv]).astype(BF16)
    ycl_ref[0, :, d_conv:d_conv + d_lru] = (y_lru * _rms(y_lru) * mng[:, d_conv:]).astype(BF16)


def _mixer_in(layer, x, mod, ln1_g, w_in_b, qg, kg, headsum, sc_w, lcw, lcb, wa_bd, ba, wx_bd, bx, lam, mng_cl,
              *, d_attn, d_conv, d_lru):
    batch, seq, d_model = x.shape
    tm = TOKEN_TILE
    n_blk = tm // MOBA_BLOCK
    nb = seq // MOBA_BLOCK

    def const(shape):
        return pl.BlockSpec(shape, lambda b, s: (0,) * len(shape))

    kern = functools.partial(_mixer_in_kernel, d_attn=d_attn, d_conv=d_conv, d_lru=d_lru)
    return pl.pallas_call(
        kern,
        grid=(batch, seq // tm),
        in_specs=[
            pl.BlockSpec((1, tm, d_model), lambda b, s: (b, s, 0)),
            pl.BlockSpec((1, 1, N_MOD, d_model), lambda b, s: (layer, b, 0, 0)),
            const((1, d_model)),
            pl.BlockSpec((1,) + w_in_b.shape[1:], lambda b, s: (layer, 0, 0)),
            const((1, d_attn)), const((1, d_attn)), const((d_attn, d_attn)),
            const(sc_w.shape), const(lcw.shape), const((1, d_lru)),
            const((d_lru, d_lru)), const((1, d_lru)), const((d_lru, d_lru)), const((1, d_lru)),
            const((1, d_lru)), const((1, d_conv + d_lru)),
        ],
        out_specs=[
            pl.BlockSpec((1, n_blk, d_attn, MOBA_BLOCK), lambda b, s: (b, s, 0, 0)),
            pl.BlockSpec((1, tm, d_attn), lambda b, s: (b, s, 0)),
            pl.BlockSpec((1, n_blk, d_attn, MOBA_BLOCK), lambda b, s: (b, s, 0, 0)),
            pl.BlockSpec((1, n_blk, 1, d_attn), lambda b, s: (b, s, 0, 0)),
            pl.BlockSpec((1, tm, d_conv + d_lru), lambda b, s: (b, s, 0)),
        ],
        out_shape=[
            jax.ShapeDtypeStruct((batch, nb, d_attn, MOBA_BLOCK), BF16),
            jax.ShapeDtypeStruct((batch, seq, d_attn), BF16),
            jax.ShapeDtypeStruct((batch, nb, d_attn, MOBA_BLOCK), BF16),
            jax.ShapeDtypeStruct((batch, nb, 1, d_attn), F32),
            jax.ShapeDtypeStruct((batch, seq, d_conv + d_lru), BF16),
        ],
        scratch_shapes=[
            pltpu.VMEM((HALO + tm, d_conv), F32),
            pltpu.VMEM((HALO + tm, d_lru), F32),
            pltpu.VMEM((1, d_lru), F32),
        ],
        compiler_params=pltpu.CompilerParams(
            dimension_semantics=("arbitrary", "arbitrary"), vmem_limit_bytes=VMEM_LIMIT_BYTES),
        name="mixer_in",
    )(x, mod, ln1_g, w_in_b, qg, kg, headsum, sc_w, lcw, lcb, wa_bd, ba, wx_bd, bx, lam, mng_cl)


def _sublane_fold(x, op):
    tiles = [x[r:r + SUBLANES] for r in range(0, x.shape[0], SUBLANES)]
    while len(tiles) > 1:
        tiles = [op(a, b) for a, b in zip(tiles[0::2], tiles[1::2])] + tiles[len(tiles) & ~1:]
    return tiles[0]


def _moba_kernel(qT_ref, k_ref, vT_ref, kmean_ref, g_ref, o_ref,
                 qcat_ref, bias_ref, s_ref, mx_ref, m_ref, acc_ref):
    g = pl.program_id(1)
    blk = MOBA_BLOCK
    qw = QUERY_BLOCKS * blk
    d_attn = qT_ref.shape[2]
    n_heads = d_attn // HEAD_DIM
    n_pairs = n_heads // 2
    nb = kmean_ref.shape[1]
    pair = 2 * HEAD_DIM
    cols = n_heads * qw
    first_blk = QUERY_BLOCKS * g

    half = lax.broadcasted_iota(jnp.int32, (pair, qw), 0) < HEAD_DIM
    kmean = kmean_ref[0, :, 0, :]

    def query_block_of(shape, axis):
        return (lax.broadcasted_iota(jnp.int32, shape, axis) & (qw - 1)) // blk

    past_q = lax.broadcasted_iota(jnp.int32, (nb, qw), 0) < first_blk + query_block_of((nb, qw), 1)
    for p in range(n_pairs):
        p0 = p * pair
        q_pair = jnp.concatenate([qT_ref[0, c, p0:p0 + pair, :] for c in range(QUERY_BLOCKS)], axis=1)
        zero = jnp.zeros_like(q_pair)
        for hh in range(2):
            hd = 2 * p + hh
            q_m = jnp.where(half if hh == 0 else jnp.logical_not(half), q_pair, zero)
            qcat_ref[p, :, hh * qw:(hh + 1) * qw] = q_m
            gate = _dot(kmean[:, p0:p0 + pair].astype(BF16), q_m)
            bias_ref[0:nb, hd * qw:(hd + 1) * qw] = jnp.where(past_q, gate, -jnp.inf)

    ones_rows = jnp.where(lax.broadcasted_iota(jnp.int32, (ACC_ROWS - HEAD_DIM, blk), 0) == 0,
                          1.0, 0.0).astype(BF16)
    key_pos = lax.broadcasted_iota(jnp.int32, (blk, qw), 0)
    qry_pos = lax.broadcasted_iota(jnp.int32, (blk, qw), 1)

    def score_parts(slot, key_blk, bias_row, diag):
        rows = pl.ds(pl.multiple_of(key_blk * blk, blk), blk)
        lo = 0 if diag is None else diag * blk
        if diag is not None:
            visible = (key_pos + diag * blk <= qry_pos)[:, lo:]

        def head(hd):
            p, hh = divmod(hd, 2)
            c0 = hd * qw + lo
            c1 = (hd + 1) * qw
            s_t = _dot(k_ref[0, rows, p * pair:(p + 1) * pair], qcat_ref[p, :, hh * qw + lo:(hh + 1) * qw])
            if diag is not None:
                s_t = jnp.where(visible, s_t, -jnp.inf)
            s_ref[slot, hd, :, lo:qw] = s_t
            if bias_row is None:
                mx_ref[:, c0:c1] = _sublane_fold(s_t, jnp.maximum)
                return
            mx8 = jnp.maximum(mx_ref[:, c0:c1],
                              _sublane_fold(s_t, jnp.maximum) + bias_ref[pl.ds(bias_row, 1), c0:c1])
            mx_ref[:, c0:c1] = mx8
            m_ref[slot, :, c0:c1] = jnp.maximum(jnp.max(mx8, axis=0, keepdims=True), M_FLOOR)

        return [functools.partial(head, hd) for hd in range(n_heads)]

    def value_parts(slot, m_prev, key_blk, bias_row, lo=0):
        m_cur = m_ref[slot]
        alpha = jnp.exp2(m_prev - m_cur)
        m_eff = m_cur - bias_ref[pl.ds(bias_row, 1), :]

        def head(hd):
            r0 = hd * HEAD_DIM
            c0 = hd * qw + lo
            c1 = (hd + 1) * qw
            p_t = jnp.exp2(s_ref[slot, hd, :, lo:qw] - m_eff[:, c0:c1])
            v_t = jnp.concatenate([vT_ref[0, key_blk, r0:r0 + HEAD_DIM, :], ones_rows], axis=0)
            a0 = hd * ACC_ROWS
            acc_ref[a0:a0 + ACC_ROWS, lo:] = (alpha[:, c0:c1] * acc_ref[a0:a0 + ACC_ROWS, lo:]
                                              + _dot(v_t, p_t.astype(BF16)))

        return [functools.partial(head, hd) for hd in range(n_heads)]

    def interleave(scores, values):
        for score, value in zip(scores, values):
            score()
            value()

    for score in score_parts(0, first_blk, None, 0):
        score()

    gate = bias_ref[0:nb, :]
    blk_i = lax.broadcasted_iota(jnp.int32, (nb, cols), 0)
    blk_f = blk_i.astype(F32)
    keep = jnp.zeros((nb, cols), F32)
    for _ in range(MOBA_TOPK):
        top = jnp.max(gate, axis=0, keepdims=True)
        first = jnp.min(jnp.where(gate == top, blk_f, float(nb)), axis=0, keepdims=True)
        hit = blk_f == first
        keep = jnp.where(hit, 1.0, keep)
        gate = jnp.where(hit, -jnp.inf, gate)
    keep = jnp.where(blk_i < first_blk + query_block_of((nb, cols), 1), keep, 0.0)
    bias_ref[0:nb, :] = jnp.where(keep > 0.0, 0.0, -jnp.inf)
    col_c = query_block_of((1, cols), 1)
    for c in range(QUERY_BLOCKS):
        bias_ref[nb + c:nb + c + 1, :] = jnp.where(col_c > c, bias_ref[pl.ds(first_blk + c, 1), :], 0.0)

    mx8 = mx_ref[...] + bias_ref[nb:nb + 1, :]
    mx_ref[...] = mx8
    m_first = jnp.maximum(jnp.max(mx8, axis=0, keepdims=True), M_FLOOR)
    m_ref[0] = m_first
    m_ref[1] = m_first

    acc_ref[...] = jnp.zeros_like(acc_ref)

    def step_blocks(t):
        return jnp.where(t == 0, first_blk, t - 1), jnp.where(t == 0, nb, t - 1)

    def fused_step(t, cur):
        m_prev = m_ref[cur]
        interleave(score_parts(cur, t - 1, t - 1, None), value_parts(1 - cur, m_prev, *step_blocks(t - 1)))

    def two_steps(u):
        for cur in (1, 0):
            fused_step(2 * u + 2 - cur, cur)

    def loop_body(v, carry):
        two_steps(2 * v)
        two_steps(2 * v + 1)
        return carry

    step_pairs = first_blk // 2
    lax.fori_loop(0, step_pairs // 2, loop_body, 0)

    @pl.when((step_pairs & 1) == 1)
    def _():
        two_steps(step_pairs - 1)

    for c in range(1, QUERY_BLOCKS):
        cur = c & 1
        m_prev = m_ref[cur]
        if c == 1:
            previous = value_parts(1 - cur, m_prev, *step_blocks(first_blk))
        else:
            previous = value_parts(1 - cur, m_prev, first_blk + c - 1, nb + c - 1, lo=(c - 1) * blk)
        interleave(score_parts(cur, first_blk + c, nb + c, c), previous)
    last = QUERY_BLOCKS - 1
    for value in value_parts(last & 1, m_ref[1 - (last & 1)], first_blk + last, nb + last, lo=last * blk):
        value()

    y_t = jnp.concatenate(
        [acc_ref[hd * ACC_ROWS:hd * ACC_ROWS + HEAD_DIM, :]
         / acc_ref[hd * ACC_ROWS + HEAD_DIM:hd * ACC_ROWS + HEAD_DIM + 1, :]
         for hd in range(n_heads)], axis=0)
    y_t = y_t * _rms(y_t, axis=0)
    o_ref[0] = (y_t.T * g_ref[...]).astype(BF16)


def _moba(qT, k, vT, kmean, mng_attn):
    batch, nb, d_attn, blk = qT.shape
    seq = k.shape[1]
    n_heads = d_attn // HEAD_DIM
    qw = QUERY_BLOCKS * blk
    cols = n_heads * qw
    assert QUERY_BLOCKS >= 2 and QUERY_BLOCKS % 2 == 0 and nb % QUERY_BLOCKS == 0

    return pl.pallas_call(
        _moba_kernel,
        grid=(batch, nb // QUERY_BLOCKS),
        in_specs=[
            pl.BlockSpec((1, QUERY_BLOCKS, d_attn, blk), lambda b, g: (b, g, 0, 0)),
            pl.BlockSpec((1, seq, d_attn), lambda b, g: (b, 0, 0)),
            pl.BlockSpec((1, nb, d_attn, blk), lambda b, g: (b, 0, 0, 0)),
            pl.BlockSpec((1, nb, 1, d_attn), lambda b, g: (b, 0, 0, 0)),
            pl.BlockSpec((1, d_attn), lambda b, g: (0, 0)),
        ],
        out_specs=pl.BlockSpec((1, qw, d_attn), lambda b, g: (b, g, 0)),
        out_shape=jax.ShapeDtypeStruct((batch, seq, d_attn), BF16),
        scratch_shapes=[
            pltpu.VMEM((n_heads // 2, 2 * HEAD_DIM, 2 * qw), BF16),
            pltpu.VMEM((nb + SUBLANES, cols), F32),
            pltpu.VMEM((2, n_heads, blk, qw + SCORE_PAD_LANES), F32),
            pltpu.VMEM((SUBLANES, cols), F32),
            pltpu.VMEM((2, 1, cols), F32),
            pltpu.VMEM((n_heads * ACC_ROWS, qw), F32),
        ],
        compiler_params=pltpu.CompilerParams(
            dimension_semantics=("arbitrary", "arbitrary"), vmem_limit_bytes=VMEM_LIMIT_BYTES),
        name="moba_attention",
    )(qT, k, vT, kmean, mng_attn)


def _out_mlp_kernel(x_ref, ya_ref, ycl_ref, mod_ref, ln2_ref, wout_ref, wup_ref, wdown_ref, o_ref):
    d_attn = ya_ref.shape[2]
    d_ff = wup_ref.shape[2]
    x = x_ref[0]
    m = mod_ref[0, 0]
    gate1, shift2, scale2, gate2 = m[2:3], m[3:4], m[4:5], m[5:6]
    mix = _dot(ya_ref[0], wout_ref[0, 0:d_attn, :]) + _dot(ycl_ref[0], wout_ref[0, d_attn:, :])
    x1 = x + gate1 * mix
    h2 = ((x1 * _rms(x1) * ln2_ref[...]) * (1.0 + scale2) + shift2).astype(BF16)
    ff = jnp.zeros_like(x1)
    for c0 in range(0, d_ff, FF_CHUNK):
        up = _dot(h2, wup_ref[0, :, c0:c0 + FF_CHUNK])
        act = jnp.square(jnp.maximum(up, 0.0)).astype(BF16)
        ff = ff + _dot(act, wdown_ref[0, c0:c0 + FF_CHUNK, :])
    o_ref[0] = x1 + gate2 * ff


def _out_mlp(layer, x, ya, ycl, mod, ln2_g, w_out_b, w_up_b, w_down_b):
    batch, seq, d_model = x.shape
    tm = MLP_TOKEN_TILE
    d_attn = ya.shape[2]
    d_cl = ycl.shape[2]

    def layer_weight(w):
        return pl.BlockSpec((1,) + w.shape[1:], lambda b, s: (layer, 0, 0), pipeline_mode=pl.Buffered(1))

    return pl.pallas_call(
        _out_mlp_kernel,
        grid=(batch, seq // tm),
        in_specs=[
            pl.BlockSpec((1, tm, d_model), lambda b, s: (b, s, 0)),
            pl.BlockSpec((1, tm, d_attn), lambda b, s: (b, s, 0)),
            pl.BlockSpec((1, tm, d_cl), lambda b, s: (b, s, 0)),
            pl.BlockSpec((1, 1, N_MOD, d_model), lambda b, s: (layer, b, 0, 0)),
            pl.BlockSpec((1, d_model), lambda b, s: (0, 0)),
            layer_weight(w_out_b), layer_weight(w_up_b), layer_weight(w_down_b),
        ],
        out_specs=pl.BlockSpec((1, tm, d_model), lambda b, s: (b, s, 0)),
        out_shape=jax.ShapeDtypeStruct((batch, seq, d_model), F32),
        compiler_params=pltpu.CompilerParams(
            dimension_semantics=("arbitrary", "arbitrary"), vmem_limit_bytes=VMEM_LIMIT_BYTES),
        name="out_mlp",
    )(x, ya, ycl, mod, ln2_g, w_out_b, w_up_b, w_down_b)


def _block_diag(w):
    n, r, c = w.shape
    eye = jnp.eye(n, dtype=w.dtype)
    return (eye[:, None, :, None] * w[:, :, None, :]).reshape(n * r, n * c)


def kernel(x, c, ln1_g, ln2_g, w_ada, b_ada, w_in, q_norm_g, k_norm_g, sc_w, lru_conv_w, lru_conv_b,
           lru_wa, lru_ba, lru_wx, lru_bx, lru_lambda, mix_norm_g, w_out, w_up, w_down):
    batch, seq, d_model = x.shape
    depth = w_in.shape[0]
    d_conv = sc_w.shape[2]
    d_lru = lru_conv_w.shape[2]
    d_attn = mix_norm_g.shape[1] - d_conv - d_lru
    n_heads = d_attn // HEAD_DIM
    assert seq % TOKEN_TILE == 0 and TOKEN_TILE % MOBA_BLOCK == 0 and seq % MLP_TOKEN_TILE == 0
    assert w_in.shape[2] == 3 * d_attn + 3 * d_conv + 2 * d_lru

    mod = _modulation(c, w_ada, b_ada).reshape(depth, batch, N_MOD, d_model)
    headsum = _block_diag(jnp.full((n_heads, HEAD_DIM, HEAD_DIM), 1.0 / HEAD_DIM, F32)).astype(BF16)

    w_in_b, w_out_b, w_up_b, w_down_b = (w.astype(BF16) for w in (w_in, w_out, w_up, w_down))
    for l in range(depth):
        row = lambda v: v.reshape(1, -1)
        qT, k, vT, kmean, ycl = _mixer_in(
            l, x, mod, row(ln1_g[l]), w_in_b,
            row(jnp.tile(q_norm_g[l], n_heads)), row(jnp.tile(k_norm_g[l], n_heads)), headsum,
            sc_w[l], lru_conv_w[l], row(lru_conv_b[l]),
            _block_diag(lru_wa[l]).astype(BF16), row(lru_ba[l]),
            _block_diag(lru_wx[l]).astype(BF16), row(lru_bx[l]),
            row(lru_lambda[l]), row(mix_norm_g[l, d_attn:]),
            d_attn=d_attn, d_conv=d_conv, d_lru=d_lru)
        ya = _moba(qT, k, vT, kmean, row(mix_norm_g[l, :d_attn]))
        x = _out_mlp(l, x, ya, ycl, mod, row(ln2_g[l]), w_out_b, w_up_b, w_down_b)
    return x
```

```python
import functools
import math

import jax
import jax.numpy as jnp
from jax import lax
from jax.experimental import pallas as pl
from jax.experimental.pallas import tpu as pltpu

F32 = jnp.float32
BF16 = jnp.bfloat16

HEAD_DIM = 64
MOBA_BLOCK = 256
MOBA_TOPK = 3
LRU_C = 8.0
N_MOD = 6
EPS = 1e-6

TOKEN_TILE = 512
MLP_TOKEN_TILE = 1024
MOD_COL_TILE = 1536
FF_CHUNK = 1024
SUBLANES = 8
HALO = SUBLANES
ACC_ROWS = HEAD_DIM + 2 * SUBLANES
LOG2_E = math.log2(math.e)
QUERY_BLOCKS = 2
M_FLOOR = -1e30
VMEM_LIMIT_BYTES = 56 * 1024 * 1024


def _rms(x, axis=-1):
    return lax.rsqrt(jnp.mean(x * x, axis=axis, keepdims=True) + EPS)


def _dot(a, b):
    return jnp.dot(a, b, preferred_element_type=F32)


def _mod_kernel(c_ref, w_ref, b_ref, o_ref):
    c = c_ref[...]
    c_act = (c * jax.nn.sigmoid(c)).astype(BF16)
    o_ref[0] = _dot(c_act, w_ref[0].astype(BF16)) + b_ref[0]


def _modulation(c, w_ada, b_ada):
    depth, d_model, n_out = w_ada.shape
    batch = c.shape[0]
    return pl.pallas_call(
        _mod_kernel,
        grid=(depth, n_out // MOD_COL_TILE),
        in_specs=[
            pl.BlockSpec((batch, d_model), lambda l, j: (0, 0)),
            pl.BlockSpec((1, d_model, MOD_COL_TILE), lambda l, j: (l, 0, j)),
            pl.BlockSpec((1, 1, MOD_COL_TILE), lambda l, j: (l, 0, j)),
        ],
        out_specs=pl.BlockSpec((1, batch, MOD_COL_TILE), lambda l, j: (l, 0, j)),
        out_shape=jax.ShapeDtypeStruct((depth, batch, n_out), F32),
        compiler_params=pltpu.CompilerParams(
            dimension_semantics=("arbitrary", "arbitrary"), vmem_limit_bytes=VMEM_LIMIT_BYTES),
        name="adaln_modulation",
    )(c, w_ada, b_ada.reshape(depth, 1, n_out))


def _scan_linear_recurrence(a, u):
    n = a.shape[0]
    row = lax.broadcasted_iota(jnp.int32, a.shape, 0)
    d = 1
    while d < n:
        keep = row >= d
        a_prev = jnp.where(keep, pltpu.roll(a, d, 0), 1.0)
        u_prev = jnp.where(keep, pltpu.roll(u, d, 0), 0.0)
        u = a * u_prev + u
        a = a * a_prev
        d *= 2
    return a, u


def _gelu_tanh(x):
    return 0.5 * x * (1.0 + jnp.tanh(math.sqrt(2.0 / math.pi) * (x + 0.044715 * (x * x * x))))


def _softplus(z):
    return jnp.maximum(z, 0.0) + jnp.log1p(jnp.exp(-jnp.abs(z)))


def _mixer_in_kernel(x_ref, mod_ref, ln1_ref, w_in_ref, qg_ref, kg_ref, headsum_ref, scw_ref,
                     lcw_ref, lcb_ref, wa_ref, ba_ref, wx_ref, bx_ref, lam_ref, mng_ref,
                     qT_ref, k_ref, vT_ref, kmean_ref, ycl_ref,
                     cu_buf, lx_buf, h_carry, *, d_attn, d_conv, d_lru):
    s = pl.program_id(1)
    tm = x_ref.shape[1]
    n_blk = tm // MOBA_BLOCK

    @pl.when(s == 0)
    def _():
        cu_buf[0:HALO, :] = jnp.zeros((HALO, d_conv), F32)
        lx_buf[0:HALO, :] = jnp.zeros((HALO, d_lru), F32)
        h_carry[...] = jnp.zeros_like(h_carry)

    x = x_ref[0]
    m = mod_ref[0, 0]
    shift1, scale1 = m[0:1], m[1:2]
    h = ((x * _rms(x) * ln1_ref[...]) * (1.0 + scale1) + shift1).astype(BF16)

    def proj(off, width):
        return _dot(h, w_in_ref[0, :, off:off + width])

    o_q, o_k, o_v = 0, d_attn, 2 * d_attn
    o_b = 3 * d_attn
    o_c, o_u = o_b + d_conv, o_b + 2 * d_conv
    o_lx = o_b + 3 * d_conv
    o_lg = o_lx + d_lru

    def head_norm(t, g_ref):
        ms = _dot((t * t).astype(BF16), headsum_ref[...])
        return t * lax.rsqrt(ms + EPS) * g_ref[...]

    q = head_norm(proj(o_q, d_attn), qg_ref)
    qT = (q * (LOG2_E / math.sqrt(HEAD_DIM))).T.astype(BF16)
    for c in range(n_blk):
        qT_ref[0, c] = qT[:, c * MOBA_BLOCK:(c + 1) * MOBA_BLOCK]

    k = head_norm(proj(o_k, d_attn), kg_ref)
    k_ref[0] = k.astype(BF16)
    for c in range(n_blk):
        kmean_ref[0, c] = jnp.mean(k[c * MOBA_BLOCK:(c + 1) * MOBA_BLOCK], axis=0, keepdims=True)

    vT = proj(o_v, d_attn).T.astype(BF16)
    for c in range(n_blk):
        vT_ref[0, c] = vT[:, c * MOBA_BLOCK:(c + 1) * MOBA_BLOCK]

    sc_b = proj(o_b, d_conv)
    cu = proj(o_c, d_conv) * proj(o_u, d_conv)
    cu_buf[HALO:HALO + tm, :] = cu
    scw = scw_ref[...]
    conv = (scw[0:1] * cu_buf[HALO - 2:HALO - 2 + tm, :]
            + scw[1:2] * cu_buf[HALO - 1:HALO - 1 + tm, :]
            + scw[2:3] * cu)
    cu_buf[0:HALO, :] = cu[tm - HALO:tm]
    y_conv = sc_b * conv

    lx = proj(o_lx, d_lru)
    lx_buf[HALO:HALO + tm, :] = lx
    lcw = lcw_ref[...]
    xr = (lcw[0:1] * lx_buf[HALO - 3:HALO - 3 + tm, :]
          + lcw[1:2] * lx_buf[HALO - 2:HALO - 2 + tm, :]
          + lcw[2:3] * lx_buf[HALO - 1:HALO - 1 + tm, :]
          + lcw[3:4] * lx) + lcb_ref[...]
    lx_buf[0:HALO, :] = lx[tm - HALO:tm]
    xr_b = xr.astype(BF16)
    r = jax.nn.sigmoid(_dot(xr_b, wa_ref[...]) + ba_ref[...])
    i = jax.nn.sigmoid(_dot(xr_b, wx_ref[...]) + bx_ref[...])
    log_a = (-LRU_C) * r * _softplus(-lam_ref[...])
    a = jnp.exp(log_a)
    t = jnp.tanh(log_a)
    u = jnp.sqrt((-2.0 * t) / (1.0 - t)) * (i * xr)
    a_cum, h_loc = _scan_linear_recurrence(a, u)
    hs = h_loc + a_cum * h_carry[...]
    h_carry[...] = hs[tm - 1:tm]
    y_lru = hs * _gelu_tanh(proj(o_lg, d_lru))

    mng = mng_ref[...]
    ycl_ref[0, :, 0:d_conv] = (y_conv * _rms(y_conv) * mng[:, 0:d_conv]).astype(BF16)
    ycl_ref[0, :, d_conv:d_conv + d_lru] = (y_lru * _rms(y_lru) * mng[:, d_conv:]).astype(BF16)


def _mixer_in(layer, x, mod, ln1_g, w_in_b, qg, kg, headsum, sc_w, lcw, lcb, wa_bd, ba, wx_bd, bx, lam, mng_cl,
              *, d_attn, d_conv, d_lru):
    batch, seq, d_model = x.shape
    tm = TOKEN_TILE
    n_blk = tm // MOBA_BLOCK
    nb = seq // MOBA_BLOCK

    def const(shape):
        return pl.BlockSpec(shape, lambda b, s: (0,) * len(shape))

    kern = functools.partial(_mixer_in_kernel, d_attn=d_attn, d_conv=d_conv, d_lru=d_lru)
    return pl.pallas_call(
        kern,
        grid=(batch, seq // tm),
        in_specs=[
            pl.BlockSpec((1, tm, d_model), lambda b, s: (b, s, 0)),
            pl.BlockSpec((1, 1, N_MOD, d_model), lambda b, s: (layer, b, 0, 0)),
            const((1, d_model)),
            pl.BlockSpec((1,) + w_in_b.shape[1:], lambda b, s: (layer, 0, 0)),
            const((1, d_attn)), const((1, d_attn)), const((d_attn, d_attn)),
            const(sc_w.shape), const(lcw.shape), const((1, d_lru)),
            const((d_lru, d_lru)), const((1, d_lru)), const((d_lru, d_lru)), const((1, d_lru)),
            const((1, d_lru)), const((1, d_conv + d_lru)),
        ],
        out_specs=[
            pl.BlockSpec((1, n_blk, d_attn, MOBA_BLOCK), lambda b, s: (b, s, 0, 0)),
            pl.BlockSpec((1, tm, d_attn), lambda b, s: (b, s, 0)),
            pl.BlockSpec((1, n_blk, d_attn, MOBA_BLOCK), lambda b, s: (b, s, 0, 0)),
            pl.BlockSpec((1, n_blk, 1, d_attn), lambda b, s: (b, s, 0, 0)),
            pl.BlockSpec((1, tm, d_conv + d_lru), lambda b, s: (b, s, 0)),
        ],
        out_shape=[
            jax.ShapeDtypeStruct((batch, nb, d_attn, MOBA_BLOCK), BF16),
            jax.ShapeDtypeStruct((batch, seq, d_attn), BF16),
            jax.ShapeDtypeStruct((batch, nb, d_attn, MOBA_BLOCK), BF16),
            jax.ShapeDtypeStruct((batch, nb, 1, d_attn), F32),
            jax.ShapeDtypeStruct((batch, seq, d_conv + d_lru), BF16),
        ],
        scratch_shapes=[
            pltpu.VMEM((HALO + tm, d_conv), F32),
            pltpu.VMEM((HALO + tm, d_lru), F32),
            pltpu.VMEM((1, d_lru), F32),
        ],
        compiler_params=pltpu.CompilerParams(
            dimension_semantics=("arbitrary", "arbitrary"), vmem_limit_bytes=VMEM_LIMIT_BYTES),
        name="mixer_in",
    )(x, mod, ln1_g, w_in_b, qg, kg, headsum, sc_w, lcw, lcb, wa_bd, ba, wx_bd, bx, lam, mng_cl)


def _sublane_fold(x, op):
    tiles = [x[r:r + SUBLANES] for r in range(0, x.shape[0], SUBLANES)]
    while len(tiles) > 1:
        tiles = [op(a, b) for a, b in zip(tiles[0::2], tiles[1::2])] + tiles[len(tiles) & ~1:]
    return tiles[0]


def _moba_kernel(qT_ref, k_ref, vT_ref, kmean_ref, g_ref, o_ref,
                 qcat_ref, bias_ref, s_ref, mx_ref, m_ref, acc_ref):
    g = pl.program_id(1)
    blk = MOBA_BLOCK
    qw = QUERY_BLOCKS * blk
    d_attn = qT_ref.shape[2]
    n_heads = d_attn // HEAD_DIM
    n_pairs = n_heads // 2
    nb = kmean_ref.shape[1]
    pair = 2 * HEAD_DIM
    cols = n_heads * qw
    first_blk = QUERY_BLOCKS * g
    n_tiles = nb // QUERY_BLOCKS
    last = QUERY_BLOCKS - 1

    half = lax.broadcasted_iota(jnp.int32, (pair, qw), 0) < HEAD_DIM
    kmean = kmean_ref[0, :, 0, :]

    def query_block_of(shape, axis):
        return (lax.broadcasted_iota(jnp.int32, shape, axis) & (qw - 1)) // blk

    past_q = lax.broadcasted_iota(jnp.int32, (nb, qw), 0) < first_blk + query_block_of((nb, qw), 1)

    def query_setup(p):
        p0 = p * pair
        q_pair = jnp.concatenate([qT_ref[0, c, p0:p0 + pair, :] for c in range(QUERY_BLOCKS)], axis=1)
        zero = jnp.zeros_like(q_pair)
        for hh in range(2):
            hd = 2 * p + hh
            q_m = jnp.where(half if hh == 0 else jnp.logical_not(half), q_pair, zero)
            qcat_ref[p, :, hh * qw:(hh + 1) * qw] = q_m
            gate = _dot(kmean[:, p0:p0 + pair].astype(BF16), q_m)
            bias_ref[0:nb, hd * qw:(hd + 1) * qw] = jnp.where(past_q, gate, -jnp.inf)

    ones_rows = jnp.where(lax.broadcasted_iota(jnp.int32, (ACC_ROWS - HEAD_DIM, blk), 0) == 0,
                          1.0, 0.0).astype(BF16)
    key_pos = lax.broadcasted_iota(jnp.int32, (blk, qw), 0)
    qry_pos = lax.broadcasted_iota(jnp.int32, (blk, qw), 1)

    def score_parts(slot, key_blk, bias_row, diag):
        rows = pl.ds(pl.multiple_of(key_blk * blk, blk), blk)
        lo = 0 if diag is None else diag * blk
        if diag is not None:
            visible = (key_pos + diag * blk <= qry_pos)[:, lo:]

        def head(hd):
            p, hh = divmod(hd, 2)
            c0 = hd * qw + lo
            c1 = (hd + 1) * qw
            s_t = _dot(k_ref[0, rows, p * pair:(p + 1) * pair], qcat_ref[p, :, hh * qw + lo:(hh + 1) * qw])
            if diag is not None:
                s_t = jnp.where(visible, s_t, -jnp.inf)
            s_ref[slot, :, c0:c1] = s_t
            if bias_row is None:
                mx_ref[:, c0:c1] = _sublane_fold(s_t, jnp.maximum)
                return
            mx8 = jnp.maximum(mx_ref[:, c0:c1],
                              _sublane_fold(s_t, jnp.maximum) + bias_ref[pl.ds(bias_row, 1), c0:c1])
            mx_ref[:, c0:c1] = mx8
            m_ref[slot, :, c0:c1] = jnp.maximum(jnp.max(mx8, axis=0, keepdims=True), M_FLOOR)

        return [functools.partial(head, hd) for hd in range(n_heads)]

    def value_parts(slot, m_prev, key_blk, bias_row, lo=0):
        m_cur = m_ref[slot]
        alpha = jnp.exp2(m_prev - m_cur)
        m_eff = m_cur if bias_row is None else m_cur - bias_ref[pl.ds(bias_row, 1), :]

        def head(hd):
            r0 = hd * HEAD_DIM
            c0 = hd * qw + lo
            c1 = (hd + 1) * qw
            p_t = jnp.exp2(s_ref[slot, :, c0:c1] - m_eff[:, c0:c1])
            v_t = jnp.concatenate([vT_ref[0, key_blk, r0:r0 + HEAD_DIM, :], ones_rows], axis=0)
            a0 = hd * ACC_ROWS
            acc_ref[a0:a0 + ACC_ROWS, lo:] = (alpha[:, c0:c1] * acc_ref[a0:a0 + ACC_ROWS, lo:]
                                              + _dot(v_t, p_t.astype(BF16)))

        return [functools.partial(head, hd) for hd in range(n_heads)]

    def interleave(scores, values):
        for score, value in zip(scores, values):
            score()
            value()

    def select_blocks():
        gate = bias_ref[0:nb, :]
        blk_i = lax.broadcasted_iota(jnp.int32, (nb, cols), 0)
        blk_f = blk_i.astype(F32)
        keep = jnp.zeros((nb, cols), F32)
        for _ in range(MOBA_TOPK):
            top = jnp.max(gate, axis=0, keepdims=True)
            first = jnp.min(jnp.where(gate == top, blk_f, float(nb)), axis=0, keepdims=True)
            hit = blk_f == first
            keep = jnp.where(hit, 1.0, keep)
            gate = jnp.where(hit, -jnp.inf, gate)
        keep = jnp.where(blk_i < first_blk + query_block_of((nb, cols), 1), keep, 0.0)
        bias_ref[0:nb, :] = jnp.where(keep > 0.0, 0.0, -jnp.inf)
        col_c = query_block_of((1, cols), 1)
        for c in range(QUERY_BLOCKS):
            bias_ref[nb + c:nb + c + 1, :] = jnp.where(col_c > c, bias_ref[pl.ds(first_blk + c, 1), :], 0.0)

        mx8 = mx_ref[...] + bias_ref[nb:nb + 1, :]
        mx_ref[...] = mx8
        m_first = jnp.maximum(jnp.max(mx8, axis=0, keepdims=True), M_FLOOR)
        m_ref[0] = m_first
        m_ref[1] = m_first

    def tail(prev_first_blk):
        slot = last & 1
        values = value_parts(slot, m_ref[1 - slot], prev_first_blk + last, None, lo=last * blk)

        def finalize():
            y_t = jnp.concatenate(
                [acc_ref[hd * ACC_ROWS:hd * ACC_ROWS + HEAD_DIM, :]
                 / acc_ref[hd * ACC_ROWS + HEAD_DIM:hd * ACC_ROWS + HEAD_DIM + 1, :]
                 for hd in range(n_heads)], axis=0)
            y_t = y_t * _rms(y_t, axis=0)
            o_ref[0] = (y_t.T * g_ref[...]).astype(BF16)

        return values, finalize

    def head_of_tile(tail_of_previous):
        values, finalize = tail_of_previous if tail_of_previous else ([], None)
        for p in range(n_pairs):
            query_setup(p)
        scores = score_parts(0, first_blk, None, 0)
        for hd in range(n_heads):
            if values:
                values[hd]()
            scores[hd]()
        if finalize:
            finalize()
        select_blocks()

    prev_first_blk = first_blk - QUERY_BLOCKS
    pl.when(g == 0)(functools.partial(head_of_tile, None))

    @pl.when(jnp.logical_and(g > 0, g < n_tiles))
    def _():
        head_of_tile(tail(prev_first_blk))

    @pl.when(g == n_tiles)
    def _():
        values, finalize = tail(prev_first_blk)
        for value in values:
            value()
        finalize()

    @pl.when(g < n_tiles)
    def _():
        acc_ref[...] = jnp.zeros_like(acc_ref)

        def step_blocks(t):
            return jnp.where(t == 0, first_blk, t - 1), jnp.where(t == 0, nb, t - 1)

        def fused_step(t, cur):
            m_prev = m_ref[cur]
            interleave(score_parts(cur, t - 1, t - 1, None), value_parts(1 - cur, m_prev, *step_blocks(t - 1)))

        def two_steps(u):
            for cur in (1, 0):
                fused_step(2 * u + 2 - cur, cur)

        def loop_body(v, carry):
            two_steps(2 * v)
            two_steps(2 * v + 1)
            return carry

        step_pairs = first_blk // 2
        lax.fori_loop(0, step_pairs // 2, loop_body, 0)

        @pl.when((step_pairs & 1) == 1)
        def _():
            two_steps(step_pairs - 1)

        for c in range(1, QUERY_BLOCKS):
            cur = c & 1
            m_prev = m_ref[cur]
            if c == 1:
                previous = value_parts(1 - cur, m_prev, *step_blocks(first_blk))
            else:
                previous = value_parts(1 - cur, m_prev, first_blk + c - 1, nb + c - 1, lo=(c - 1) * blk)
            interleave(score_parts(cur, first_blk + c, nb + c, c), previous)


def _moba(qT, k, vT, kmean, mng_attn):
    batch, nb, d_attn, blk = qT.shape
    seq = k.shape[1]
    n_heads = d_attn // HEAD_DIM
    qw = QUERY_BLOCKS * blk
    cols = n_heads * qw
    assert QUERY_BLOCKS >= 2 and QUERY_BLOCKS % 2 == 0 and nb % QUERY_BLOCKS == 0
    n_tiles = nb // QUERY_BLOCKS

    return pl.pallas_call(
        _moba_kernel,
        grid=(batch, n_tiles + 1),
        in_specs=[
            pl.BlockSpec((1, QUERY_BLOCKS, d_attn, blk), lambda b, g: (b, jnp.minimum(g, n_tiles - 1), 0, 0)),
            pl.BlockSpec((1, seq, d_attn), lambda b, g: (b, 0, 0)),
            pl.BlockSpec((1, nb, d_attn, blk), lambda b, g: (b, 0, 0, 0)),
            pl.BlockSpec((1, nb, 1, d_attn), lambda b, g: (b, 0, 0, 0)),
            pl.BlockSpec((1, d_attn), lambda b, g: (0, 0)),
        ],
        out_specs=pl.BlockSpec((1, qw, d_attn), lambda b, g: (b, jnp.maximum(g - 1, 0), 0)),
        out_shape=jax.ShapeDtypeStruct((batch, seq, d_attn), BF16),
        scratch_shapes=[
            pltpu.VMEM((n_heads // 2, 2 * HEAD_DIM, 2 * qw), BF16),
            pltpu.VMEM((nb + SUBLANES, cols), F32),
            pltpu.VMEM((2, blk, cols), F32),
            pltpu.VMEM((SUBLANES, cols), F32),
            pltpu.VMEM((2, 1, cols), F32),
            pltpu.VMEM((n_heads * ACC_ROWS, qw), F32),
        ],
        compiler_params=pltpu.CompilerParams(
            dimension_semantics=("arbitrary", "arbitrary"), vmem_limit_bytes=VMEM_LIMIT_BYTES),
        name="moba_attention",
    )(qT, k, vT, kmean, mng_attn)


def _out_mlp_kernel(x_ref, ya_ref, ycl_ref, mod_ref, ln2_ref, wout_ref, wup_ref, wdown_ref, o_ref):
    d_attn = ya_ref.shape[2]
    d_ff = wup_ref.shape[2]
    x = x_ref[0]
    m = mod_ref[0, 0]
    gate1, shift2, scale2, gate2 = m[2:3], m[3:4], m[4:5], m[5:6]
    mix = _dot(ya_ref[0], wout_ref[0, 0:d_attn, :]) + _dot(ycl_ref[0], wout_ref[0, d_attn:, :])
    x1 = x + gate1 * mix
    h2 = ((x1 * _rms(x1) * ln2_ref[...]) * (1.0 + scale2) + shift2).astype(BF16)
    ff = jnp.zeros_like(x1)
    for c0 in range(0, d_ff, FF_CHUNK):
        up = _dot(h2, wup_ref[0, :, c0:c0 + FF_CHUNK])
        act = jnp.square(jnp.maximum(up, 0.0)).astype(BF16)
        ff = ff + _dot(act, wdown_ref[0, c0:c0 + FF_CHUNK, :])
    o_ref[0] = x1 + gate2 * ff


def _out_mlp(layer, x, ya, ycl, mod, ln2_g, w_out_b, w_up_b, w_down_b):
    batch, seq, d_model = x.shape
    tm = MLP_TOKEN_TILE
    d_attn = ya.shape[2]
    d_cl = ycl.shape[2]

    def layer_weight(w):
        return pl.BlockSpec((1,) + w.shape[1:], lambda b, s: (layer, 0, 0), pipeline_mode=pl.Buffered(1))

    return pl.pallas_call(
        _out_mlp_kernel,
        grid=(batch, seq // tm),
        in_specs=[
            pl.BlockSpec((1, tm, d_model), lambda b, s: (b, s, 0)),
            pl.BlockSpec((1, tm, d_attn), lambda b, s: (b, s, 0)),
            pl.BlockSpec((1, tm, d_cl), lambda b, s: (b, s, 0)),
            pl.BlockSpec((1, 1, N_MOD, d_model), lambda b, s: (layer, b, 0, 0)),
            pl.BlockSpec((1, d_model), lambda b, s: (0, 0)),
            layer_weight(w_out_b), layer_weight(w_up_b), layer_weight(w_down_b),
        ],
        out_specs=pl.BlockSpec((1, tm, d_model), lambda b, s: (b, s, 0)),
        out_shape=jax.ShapeDtypeStruct((batch, seq, d_model), F32),
        compiler_params=pltpu.CompilerParams(
            dimension_semantics=("arbitrary", "arbitrary"), vmem_limit_bytes=VMEM_LIMIT_BYTES),
        name="out_mlp",
    )(x, ya, ycl, mod, ln2_g, w_out_b, w_up_b, w_down_b)


def _block_diag(w):
    n, r, c = w.shape
    eye = jnp.eye(n, dtype=w.dtype)
    return (eye[:, None, :, None] * w[:, :, None, :]).reshape(n * r, n * c)


def kernel(x, c, ln1_g, ln2_g, w_ada, b_ada, w_in, q_norm_g, k_norm_g, sc_w, lru_conv_w, lru_conv_b,
           lru_wa, lru_ba, lru_wx, lru_bx, lru_lambda, mix_norm_g, w_out, w_up, w_down):
    batch, seq, d_model = x.shape
    depth = w_in.shape[0]
    d_conv = sc_w.shape[2]
    d_lru = lru_conv_w.shape[2]
    d_attn = mix_norm_g.shape[1] - d_conv - d_lru
    n_heads = d_attn // HEAD_DIM
    assert seq % TOKEN_TILE == 0 and TOKEN_TILE % MOBA_BLOCK == 0 and seq % MLP_TOKEN_TILE == 0
    assert w_in.shape[2] == 3 * d_attn + 3 * d_conv + 2 * d_lru

    mod = _modulation(c, w_ada, b_ada).reshape(depth, batch, N_MOD, d_model)
    headsum = _block_diag(jnp.full((n_heads, HEAD_DIM, HEAD_DIM), 1.0 / HEAD_DIM, F32)).astype(BF16)

    w_in_b, w_out_b, w_up_b, w_down_b = (w.astype(BF16) for w in (w_in, w_out, w_up, w_down))
    for l in range(depth):
        row = lambda v: v.reshape(1, -1)
        qT, k, vT, kmean, ycl = _mixer_in(
            l, x, mod, row(ln1_g[l]), w_in_b,
            row(jnp.tile(q_norm_g[l], n_heads)), row(jnp.tile(k_norm_g[l], n_heads)), headsum,
            sc_w[l], lru_conv_w[l], row(lru_conv_b[l]),
            _block_diag(lru_wa[l]).astype(BF16), row(lru_ba[l]),
            _block_diag(lru_wx[l]).astype(BF16), row(lru_bx[l]),
            row(lru_lambda[l]), row(mix_norm_g[l, d_attn:]),
            d_attn=d_attn, d_conv=d_conv, d_lru=d_lru)
        ya = _moba(qT, k, vT, kmean, row(mix_norm_g[l, :d_attn]))
        x = _out_mlp(l, x, ya, ycl, mod, row(ln2_g[l]), w_out_b, w_up_b, w_down_b)
    return x
```

```python
import functools
import math

import jax
import jax.numpy as jnp
from jax import lax
from jax.experimental import pallas as pl
from jax.experimental.pallas import tpu as pltpu

F32 = jnp.float32
BF16 = jnp.bfloat16

HEAD_DIM = 64
MOBA_BLOCK = 256
MOBA_TOPK = 3
LRU_C = 8.0
N_MOD = 6
EPS = 1e-6

TOKEN_TILE = 512
MLP_TOKEN_TILE = 1024
MOD_COL_TILE = 1536
FF_CHUNK = 1024
LANES = 128
SUBLANES = 8
HALO = SUBLANES
ACC_ROWS = HEAD_DIM + 2 * SUBLANES
LOG2_E = math.log2(math.e)
QUERY_BLOCKS = 2
M_FLOOR = -1e30
VMEM_LIMIT_BYTES = 56 * 1024 * 1024


def _rms(x, axis=-1):
    return lax.rsqrt(jnp.mean(x * x, axis=axis, keepdims=True) + EPS)


def _dot(a, b):
    return jnp.dot(a, b, preferred_element_type=F32)


def _mod_kernel(c_ref, w_ref, b_ref, o_ref):
    c = c_ref[...]
    c_act = (c * jax.nn.sigmoid(c)).astype(BF16)
    o_ref[0] = _dot(c_act, w_ref[0].astype(BF16)) + b_ref[0]


def _modulation(c, w_ada, b_ada):
    depth, d_model, n_out = w_ada.shape
    batch = c.shape[0]
    return pl.pallas_call(
        _mod_kernel,
        grid=(depth, n_out // MOD_COL_TILE),
        in_specs=[
            pl.BlockSpec((batch, d_model), lambda l, j: (0, 0)),
            pl.BlockSpec((1, d_model, MOD_COL_TILE), lambda l, j: (l, 0, j)),
            pl.BlockSpec((1, 1, MOD_COL_TILE), lambda l, j: (l, 0, j)),
        ],
        out_specs=pl.BlockSpec((1, batch, MOD_COL_TILE), lambda l, j: (l, 0, j)),
        out_shape=jax.ShapeDtypeStruct((depth, batch, n_out), F32),
        compiler_params=pltpu.CompilerParams(
            dimension_semantics=("arbitrary", "arbitrary"), vmem_limit_bytes=VMEM_LIMIT_BYTES),
        name="adaln_modulation",
    )(c, w_ada, b_ada.reshape(depth, 1, n_out))


def _scan_linear_recurrence(a, u):
    n = a.shape[0]
    row = lax.broadcasted_iota(jnp.int32, a.shape, 0)
    d = 1
    while d < n:
        keep = row >= d
        a_prev = jnp.where(keep, pltpu.roll(a, d, 0), 1.0)
        u_prev = jnp.where(keep, pltpu.roll(u, d, 0), 0.0)
        u = a * u_prev + u
        a = a * a_prev
        d *= 2
    return a, u


def _gelu_tanh(x):
    return 0.5 * x * (1.0 + jnp.tanh(math.sqrt(2.0 / math.pi) * (x + 0.044715 * (x * x * x))))


def _softplus(z):
    return jnp.maximum(z, 0.0) + jnp.log1p(jnp.exp(-jnp.abs(z)))


def _mixer_in_kernel(x_ref, mod_ref, ln1_ref, w_in_ref, wqv_ref, qgt_ref, kg_ref, headsum_ref, scw_ref,
                     lcw_ref, lcb_ref, wa_ref, ba_ref, wx_ref, bx_ref, lam_ref, mng_ref,
                     qT_ref, k_ref, vT_ref, kmean_ref, ycl_ref,
                     cu_buf, lx_buf, h_carry, *, d_attn, d_conv, d_lru):
    s = pl.program_id(1)
    tm = x_ref.shape[1]
    n_blk = tm // MOBA_BLOCK

    @pl.when(s == 0)
    def _():
        cu_buf[0:HALO, :] = jnp.zeros((HALO, d_conv), F32)
        lx_buf[0:HALO, :] = jnp.zeros((HALO, d_lru), F32)
        h_carry[...] = jnp.zeros_like(h_carry)

    x = x_ref[0]
    m = mod_ref[0, 0]
    shift1, scale1 = m[0:1], m[1:2]
    h = ((x * _rms(x) * ln1_ref[...]) * (1.0 + scale1) + shift1).astype(BF16)

    def proj(off, width):
        return _dot(h, w_in_ref[0, :, off:off + width])

    o_q, o_k, o_v = 0, d_attn, 2 * d_attn
    o_b = 3 * d_attn
    o_c, o_u = o_b + d_conv, o_b + 2 * d_conv
    o_lx = o_b + 3 * d_conv
    o_lg = o_lx + d_lru

    def head_norm(t, g_ref):
        ms = _dot((t * t).astype(BF16), headsum_ref[...])
        return t * lax.rsqrt(ms + EPS) * g_ref[...]

    qv_t = lax.dot_general(wqv_ref[0], h, (((1,), (1,)), ((), ())), preferred_element_type=F32)
    q_t = qv_t[0:d_attn]
    sq = q_t * q_t
    inv = jnp.concatenate(
        [jnp.broadcast_to(lax.rsqrt(jnp.mean(sq[r:r + HEAD_DIM], axis=0, keepdims=True) + EPS), (HEAD_DIM, tm))
         for r in range(0, d_attn, HEAD_DIM)], axis=0)
    gain = jnp.concatenate([qgt_ref[...]] * (tm // qgt_ref.shape[1]), axis=1)
    qT = (q_t * inv * gain * (LOG2_E / math.sqrt(HEAD_DIM))).astype(BF16)
    for c in range(n_blk):
        qT_ref[0, c] = qT[:, c * MOBA_BLOCK:(c + 1) * MOBA_BLOCK]

    k = head_norm(proj(o_k, d_attn), kg_ref)
    k_ref[0] = k.astype(BF16)
    for c in range(n_blk):
        kmean_ref[0, c] = jnp.mean(k[c * MOBA_BLOCK:(c + 1) * MOBA_BLOCK], axis=0, keepdims=True)

    vT = qv_t[d_attn:].astype(BF16)
    for c in range(n_blk):
        vT_ref[0, c] = vT[:, c * MOBA_BLOCK:(c + 1) * MOBA_BLOCK]

    sc_b = proj(o_b, d_conv)
    cu = proj(o_c, d_conv) * proj(o_u, d_conv)
    cu_buf[HALO:HALO + tm, :] = cu
    scw = scw_ref[...]
    conv = (scw[0:1] * cu_buf[HALO - 2:HALO - 2 + tm, :]
            + scw[1:2] * cu_buf[HALO - 1:HALO - 1 + tm, :]
            + scw[2:3] * cu)
    cu_buf[0:HALO, :] = cu[tm - HALO:tm]
    y_conv = sc_b * conv

    lx = proj(o_lx, d_lru)
    lx_buf[HALO:HALO + tm, :] = lx
    lcw = lcw_ref[...]
    xr = (lcw[0:1] * lx_buf[HALO - 3:HALO - 3 + tm, :]
          + lcw[1:2] * lx_buf[HALO - 2:HALO - 2 + tm, :]
          + lcw[2:3] * lx_buf[HALO - 1:HALO - 1 + tm, :]
          + lcw[3:4] * lx) + lcb_ref[...]
    lx_buf[0:HALO, :] = lx[tm - HALO:tm]
    xr_b = xr.astype(BF16)
    r = jax.nn.sigmoid(_dot(xr_b, wa_ref[...]) + ba_ref[...])
    i = jax.nn.sigmoid(_dot(xr_b, wx_ref[...]) + bx_ref[...])
    log_a = (-LRU_C) * r * _softplus(-lam_ref[...])
    a = jnp.exp(log_a)
    t = jnp.tanh(log_a)
    u = jnp.sqrt((-2.0 * t) / (1.0 - t)) * (i * xr)
    a_cum, h_loc = _scan_linear_recurrence(a, u)
    hs = h_loc + a_cum * h_carry[...]
    h_carry[...] = hs[tm - 1:tm]
    y_lru = hs * _gelu_tanh(proj(o_lg, d_lru))

    mng = mng_ref[...]
    ycl_ref[0, :, 0:d_conv] = (y_conv * _rms(y_conv) * mng[:, 0:d_conv]).astype(BF16)
    ycl_ref[0, :, d_conv:d_conv + d_lru] = (y_lru * _rms(y_lru) * mng[:, d_conv:]).astype(BF16)


def _mixer_in(layer, x, mod, ln1_g, w_in_b, w_qv_t, qg_t, kg, headsum, sc_w, lcw, lcb, wa_bd, ba, wx_bd, bx, lam, mng_cl,
              *, d_attn, d_conv, d_lru):
    batch, seq, d_model = x.shape
    tm = TOKEN_TILE
    n_blk = tm // MOBA_BLOCK
    nb = seq // MOBA_BLOCK

    def const(shape):
        return pl.BlockSpec(shape, lambda b, s: (0,) * len(shape))

    kern = functools.partial(_mixer_in_kernel, d_attn=d_attn, d_conv=d_conv, d_lru=d_lru)
    return pl.pallas_call(
        kern,
        grid=(batch, seq // tm),
        in_specs=[
            pl.BlockSpec((1, tm, d_model), lambda b, s: (b, s, 0)),
            pl.BlockSpec((1, 1, N_MOD, d_model), lambda b, s: (layer, b, 0, 0)),
            const((1, d_model)),
            pl.BlockSpec((1,) + w_in_b.shape[1:], lambda b, s: (layer, 0, 0)),
            pl.BlockSpec((1,) + w_qv_t.shape[1:], lambda b, s: (layer, 0, 0)),
            const(qg_t.shape), const((1, d_attn)), const((d_attn, d_attn)),
            const(sc_w.shape), const(lcw.shape), const((1, d_lru)),
            const((d_lru, d_lru)), const((1, d_lru)), const((d_lru, d_lru)), const((1, d_lru)),
            const((1, d_lru)), const((1, d_conv + d_lru)),
        ],
        out_specs=[
            pl.BlockSpec((1, n_blk, d_attn, MOBA_BLOCK), lambda b, s: (b, s, 0, 0)),
            pl.BlockSpec((1, tm, d_attn), lambda b, s: (b, s, 0)),
            pl.BlockSpec((1, n_blk, d_attn, MOBA_BLOCK), lambda b, s: (b, s, 0, 0)),
            pl.BlockSpec((1, n_blk, 1, d_attn), lambda b, s: (b, s, 0, 0)),
            pl.BlockSpec((1, tm, d_conv + d_lru), lambda b, s: (b, s, 0)),
        ],
        out_shape=[
            jax.ShapeDtypeStruct((batch, nb, d_attn, MOBA_BLOCK), BF16),
            jax.ShapeDtypeStruct((batch, seq, d_attn), BF16),
            jax.ShapeDtypeStruct((batch, nb, d_attn, MOBA_BLOCK), BF16),
            jax.ShapeDtypeStruct((batch, nb, 1, d_attn), F32),
            jax.ShapeDtypeStruct((batch, seq, d_conv + d_lru), BF16),
        ],
        scratch_shapes=[
            pltpu.VMEM((HALO + tm, d_conv), F32),
            pltpu.VMEM((HALO + tm, d_lru), F32),
            pltpu.VMEM((1, d_lru), F32),
        ],
        compiler_params=pltpu.CompilerParams(
            dimension_semantics=("arbitrary", "arbitrary"), vmem_limit_bytes=VMEM_LIMIT_BYTES),
        name="mixer_in",
    )(x, mod, ln1_g, w_in_b, w_qv_t, qg_t, kg, headsum, sc_w, lcw, lcb, wa_bd, ba, wx_bd, bx, lam, mng_cl)


def _sublane_fold(x, op):
    tiles = [x[r:r + SUBLANES] for r in range(0, x.shape[0], SUBLANES)]
    while len(tiles) > 1:
        tiles = [op(a, b) for a, b in zip(tiles[0::2], tiles[1::2])] + tiles[len(tiles) & ~1:]
    return tiles[0]


def _moba_kernel(qT_ref, k_ref, vT_ref, kmean_ref, g_ref, o_ref,
                 qcat_ref, bias_ref, s_ref, mx_ref, m_ref, acc_ref):
    g = pl.program_id(1)
    blk = MOBA_BLOCK
    qw = QUERY_BLOCKS * blk
    d_attn = qT_ref.shape[2]
    n_heads = d_attn // HEAD_DIM
    n_pairs = n_heads // 2
    nb = kmean_ref.shape[1]
    pair = 2 * HEAD_DIM
    cols = n_heads * qw
    first_blk = QUERY_BLOCKS * g

    half = lax.broadcasted_iota(jnp.int32, (pair, qw), 0) < HEAD_DIM
    kmean = kmean_ref[0, :, 0, :]

    def query_block_of(shape, axis):
        return (lax.broadcasted_iota(jnp.int32, shape, axis) & (qw - 1)) // blk

    past_q = lax.broadcasted_iota(jnp.int32, (nb, qw), 0) < first_blk + query_block_of((nb, qw), 1)
    for p in range(n_pairs):
        p0 = p * pair
        q_pair = jnp.concatenate([qT_ref[0, c, p0:p0 + pair, :] for c in range(QUERY_BLOCKS)], axis=1)
        zero = jnp.zeros_like(q_pair)
        for hh in range(2):
            hd = 2 * p + hh
            q_m = jnp.where(half if hh == 0 else jnp.logical_not(half), q_pair, zero)
            qcat_ref[p, :, hh * qw:(hh + 1) * qw] = q_m
            gate = _dot(kmean[:, p0:p0 + pair].astype(BF16), q_m)
            bias_ref[0:nb, hd * qw:(hd + 1) * qw] = jnp.where(past_q, gate, -jnp.inf)

    ones_rows = jnp.where(lax.broadcasted_iota(jnp.int32, (ACC_ROWS - HEAD_DIM, blk), 0) == 0,
                          1.0, 0.0).astype(BF16)
    key_pos = lax.broadcasted_iota(jnp.int32, (blk, qw), 0)
    qry_pos = lax.broadcasted_iota(jnp.int32, (blk, qw), 1)

    def score_parts(slot, key_blk, bias_row, diag):
        rows = pl.ds(pl.multiple_of(key_blk * blk, blk), blk)
        lo = 0 if diag is None else diag * blk
        if diag is not None:
            visible = (key_pos + diag * blk <= qry_pos)[:, lo:]

        def head(hd):
            p, hh = divmod(hd, 2)
            c0 = hd * qw + lo
            c1 = (hd + 1) * qw
            s_t = _dot(k_ref[0, rows, p * pair:(p + 1) * pair], qcat_ref[p, :, hh * qw + lo:(hh + 1) * qw])
            if diag is not None:
                s_t = jnp.where(visible, s_t, -jnp.inf)
            s_ref[slot, :, c0:c1] = s_t
            if bias_row is None:
                mx_ref[:, c0:c1] = _sublane_fold(s_t, jnp.maximum)
                return
            mx8 = jnp.maximum(mx_ref[:, c0:c1],
                              _sublane_fold(s_t, jnp.maximum) + bias_ref[pl.ds(bias_row, 1), c0:c1])
            mx_ref[:, c0:c1] = mx8
            m_ref[slot, :, c0:c1] = jnp.maximum(jnp.max(mx8, axis=0, keepdims=True), M_FLOOR)

        return [functools.partial(head, hd) for hd in range(n_heads)]

    def value_parts(slot, m_prev, key_blk, bias_row, lo=0):
        m_cur = m_ref[slot]
        alpha = jnp.exp2(m_prev - m_cur)
        m_eff = m_cur - bias_ref[pl.ds(bias_row, 1), :]

        def head(hd):
            r0 = hd * HEAD_DIM
            c0 = hd * qw + lo
            c1 = (hd + 1) * qw
            p_t = jnp.exp2(s_ref[slot, :, c0:c1] - m_eff[:, c0:c1])
            v_t = jnp.concatenate([vT_ref[0, key_blk, r0:r0 + HEAD_DIM, :], ones_rows], axis=0)
            a0 = hd * ACC_ROWS
            acc_ref[a0:a0 + ACC_ROWS, lo:] = (alpha[:, c0:c1] * acc_ref[a0:a0 + ACC_ROWS, lo:]
                                              + _dot(v_t, p_t.astype(BF16)))

        return [functools.partial(head, hd) for hd in range(n_heads)]

    def interleave(scores, values):
        for score, value in zip(scores, values):
            score()
            value()

    for score in score_parts(0, first_blk, None, 0):
        score()

    gate = bias_ref[0:nb, :]
    blk_i = lax.broadcasted_iota(jnp.int32, (nb, cols), 0)
    blk_f = blk_i.astype(F32)
    keep = jnp.zeros((nb, cols), F32)
    for _ in range(MOBA_TOPK):
        top = jnp.max(gate, axis=0, keepdims=True)
        first = jnp.min(jnp.where(gate == top, blk_f, float(nb)), axis=0, keepdims=True)
        hit = blk_f == first
        keep = jnp.where(hit, 1.0, keep)
        gate = jnp.where(hit, -jnp.inf, gate)
    keep = jnp.where(blk_i < first_blk + query_block_of((nb, cols), 1), keep, 0.0)
    bias_ref[0:nb, :] = jnp.where(keep > 0.0, 0.0, -jnp.inf)
    col_c = query_block_of((1, cols), 1)
    for c in range(QUERY_BLOCKS):
        bias_ref[nb + c:nb + c + 1, :] = jnp.where(col_c > c, bias_ref[pl.ds(first_blk + c, 1), :], 0.0)

    mx8 = mx_ref[...] + bias_ref[nb:nb + 1, :]
    mx_ref[...] = mx8
    m_first = jnp.maximum(jnp.max(mx8, axis=0, keepdims=True), M_FLOOR)
    m_ref[0] = m_first
    m_ref[1] = m_first

    acc_ref[...] = jnp.zeros_like(acc_ref)

    def step_blocks(t):
        return jnp.where(t == 0, first_blk, t - 1), jnp.where(t == 0, nb, t - 1)

    def fused_step(t, cur):
        m_prev = m_ref[cur]
        interleave(score_parts(cur, t - 1, t - 1, None), value_parts(1 - cur, m_prev, *step_blocks(t - 1)))

    def two_steps(u):
        for cur in (1, 0):
            fused_step(2 * u + 2 - cur, cur)

    def loop_body(v, carry):
        two_steps(2 * v)
        two_steps(2 * v + 1)
        return carry

    step_pairs = first_blk // 2
    lax.fori_loop(0, step_pairs // 2, loop_body, 0)

    @pl.when((step_pairs & 1) == 1)
    def _():
        two_steps(step_pairs - 1)

    for c in range(1, QUERY_BLOCKS):
        cur = c & 1
        m_prev = m_ref[cur]
        if c == 1:
            previous = value_parts(1 - cur, m_prev, *step_blocks(first_blk))
        else:
            previous = value_parts(1 - cur, m_prev, first_blk + c - 1, nb + c - 1, lo=(c - 1) * blk)
        interleave(score_parts(cur, first_blk + c, nb + c, c), previous)
    last = QUERY_BLOCKS - 1
    for value in value_parts(last & 1, m_ref[1 - (last & 1)], first_blk + last, nb + last, lo=last * blk):
        value()

    y_t = jnp.concatenate(
        [acc_ref[hd * ACC_ROWS:hd * ACC_ROWS + HEAD_DIM, :]
         / acc_ref[hd * ACC_ROWS + HEAD_DIM:hd * ACC_ROWS + HEAD_DIM + 1, :]
         for hd in range(n_heads)], axis=0)
    y_t = y_t * _rms(y_t, axis=0)
    o_ref[0] = (y_t.T * g_ref[...]).astype(BF16)


def _moba(qT, k, vT, kmean, mng_attn):
    batch, nb, d_attn, blk = qT.shape
    seq = k.shape[1]
    n_heads = d_attn // HEAD_DIM
    qw = QUERY_BLOCKS * blk
    cols = n_heads * qw
    assert QUERY_BLOCKS >= 2 and QUERY_BLOCKS % 2 == 0 and nb % QUERY_BLOCKS == 0

    return pl.pallas_call(
        _moba_kernel,
        grid=(batch, nb // QUERY_BLOCKS),
        in_specs=[
            pl.BlockSpec((1, QUERY_BLOCKS, d_attn, blk), lambda b, g: (b, g, 0, 0)),
            pl.BlockSpec((1, seq, d_attn), lambda b, g: (b, 0, 0)),
            pl.BlockSpec((1, nb, d_attn, blk), lambda b, g: (b, 0, 0, 0)),
            pl.BlockSpec((1, nb, 1, d_attn), lambda b, g: (b, 0, 0, 0)),
            pl.BlockSpec((1, d_attn), lambda b, g: (0, 0)),
        ],
        out_specs=pl.BlockSpec((1, qw, d_attn), lambda b, g: (b, g, 0)),
        out_shape=jax.ShapeDtypeStruct((batch, seq, d_attn), BF16),
        scratch_shapes=[
            pltpu.VMEM((n_heads // 2, 2 * HEAD_DIM, 2 * qw), BF16),
            pltpu.VMEM((nb + SUBLANES, cols), F32),
            pltpu.VMEM((2, blk, cols), F32),
            pltpu.VMEM((SUBLANES, cols), F32),
            pltpu.VMEM((2, 1, cols), F32),
            pltpu.VMEM((n_heads * ACC_ROWS, qw), F32),
        ],
        compiler_params=pltpu.CompilerParams(
            dimension_semantics=("arbitrary", "arbitrary"), vmem_limit_bytes=VMEM_LIMIT_BYTES),
        name="moba_attention",
    )(qT, k, vT, kmean, mng_attn)


def _out_mlp_kernel(x_ref, ya_ref, ycl_ref, mod_ref, ln2_ref, wout_ref, wup_ref, wdown_ref, o_ref):
    d_attn = ya_ref.shape[2]
    d_ff = wup_ref.shape[2]
    x = x_ref[0]
    m = mod_ref[0, 0]
    gate1, shift2, scale2, gate2 = m[2:3], m[3:4], m[4:5], m[5:6]
    mix = _dot(ya_ref[0], wout_ref[0, 0:d_attn, :]) + _dot(ycl_ref[0], wout_ref[0, d_attn:, :])
    x1 = x + gate1 * mix
    h2 = ((x1 * _rms(x1) * ln2_ref[...]) * (1.0 + scale2) + shift2).astype(BF16)
    ff = jnp.zeros_like(x1)
    for c0 in range(0, d_ff, FF_CHUNK):
        up = _dot(h2, wup_ref[0, :, c0:c0 + FF_CHUNK])
        act = jnp.square(jnp.maximum(up, 0.0)).astype(BF16)
        ff = ff + _dot(act, wdown_ref[0, c0:c0 + FF_CHUNK, :])
    o_ref[0] = x1 + gate2 * ff


def _out_mlp(layer, x, ya, ycl, mod, ln2_g, w_out_b, w_up_b, w_down_b):
    batch, seq, d_model = x.shape
    tm = MLP_TOKEN_TILE
    d_attn = ya.shape[2]
    d_cl = ycl.shape[2]

    def layer_weight(w):
        return pl.BlockSpec((1,) + w.shape[1:], lambda b, s: (layer, 0, 0), pipeline_mode=pl.Buffered(1))

    return pl.pallas_call(
        _out_mlp_kernel,
        grid=(batch, seq // tm),
        in_specs=[
            pl.BlockSpec((1, tm, d_model), lambda b, s: (b, s, 0)),
            pl.BlockSpec((1, tm, d_attn), lambda b, s: (b, s, 0)),
            pl.BlockSpec((1, tm, d_cl), lambda b, s: (b, s, 0)),
            pl.BlockSpec((1, 1, N_MOD, d_model), lambda b, s: (layer, b, 0, 0)),
            pl.BlockSpec((1, d_model), lambda b, s: (0, 0)),
            layer_weight(w_out_b), layer_weight(w_up_b), layer_weight(w_down_b),
        ],
        out_specs=pl.BlockSpec((1, tm, d_model), lambda b, s: (b, s, 0)),
        out_shape=jax.ShapeDtypeStruct((batch, seq, d_model), F32),
        compiler_params=pltpu.CompilerParams(
            dimension_semantics=("arbitrary", "arbitrary"), vmem_limit_bytes=VMEM_LIMIT_BYTES),
        name="out_mlp",
    )(x, ya, ycl, mod, ln2_g, w_out_b, w_up_b, w_down_b)


def _block_diag(w):
    n, r, c = w.shape
    eye = jnp.eye(n, dtype=w.dtype)
    return (eye[:, None, :, None] * w[:, :, None, :]).reshape(n * r, n * c)


def kernel(x, c, ln1_g, ln2_g, w_ada, b_ada, w_in, q_norm_g, k_norm_g, sc_w, lru_conv_w, lru_conv_b,
           lru_wa, lru_ba, lru_wx, lru_bx, lru_lambda, mix_norm_g, w_out, w_up, w_down):
    batch, seq, d_model = x.shape
    depth = w_in.shape[0]
    d_conv = sc_w.shape[2]
    d_lru = lru_conv_w.shape[2]
    d_attn = mix_norm_g.shape[1] - d_conv - d_lru
    n_heads = d_attn // HEAD_DIM
    assert seq % TOKEN_TILE == 0 and TOKEN_TILE % MOBA_BLOCK == 0 and seq % MLP_TOKEN_TILE == 0
    assert w_in.shape[2] == 3 * d_attn + 3 * d_conv + 2 * d_lru

    mod = _modulation(c, w_ada, b_ada).reshape(depth, batch, N_MOD, d_model)
    headsum = _block_diag(jnp.full((n_heads, HEAD_DIM, HEAD_DIM), 1.0 / HEAD_DIM, F32)).astype(BF16)

    w_in_b, w_out_b, w_up_b, w_down_b = (w.astype(BF16) for w in (w_in, w_out, w_up, w_down))
    w_qv_t = jnp.concatenate([w_in_b[:, :, 0:d_attn], w_in_b[:, :, 2 * d_attn:3 * d_attn]], axis=2).transpose(0, 2, 1)
    for l in range(depth):
        row = lambda v: v.reshape(1, -1)
        qT, k, vT, kmean, ycl = _mixer_in(
            l, x, mod, row(ln1_g[l]), w_in_b, w_qv_t,
            jnp.broadcast_to(jnp.tile(q_norm_g[l], n_heads)[:, None], (d_attn, LANES)),
            row(jnp.tile(k_norm_g[l], n_heads)), headsum,
            sc_w[l], lru_conv_w[l], row(lru_conv_b[l]),
            _block_diag(lru_wa[l]).astype(BF16), row(lru_ba[l]),
            _block_diag(lru_wx[l]).astype(BF16), row(lru_bx[l]),
            row(lru_lambda[l]), row(mix_norm_g[l, d_attn:]),
            d_attn=d_attn, d_conv=d_conv, d_lru=d_lru)
        ya = _moba(qT, k, vT, kmean, row(mix_norm_g[l, :d_attn]))
        x = _out_mlp(l, x, ya, ycl, mod, row(ln2_g[l]), w_out_b, w_up_b, w_down_b)
    return x
```

```python
import functools
import math

import jax
import jax.numpy as jnp
from jax import lax
from jax.experimental import pallas as pl
from jax.experimental.pallas import tpu as pltpu

F32 = jnp.float32
BF16 = jnp.bfloat16

HEAD_DIM = 64
MOBA_BLOCK = 256
MOBA_TOPK = 3
LRU_C = 8.0
N_MOD = 6
EPS = 1e-6

TOKEN_TILE = 512
MLP_TOKEN_TILE = 1024
MOD_COL_TILE = 1536
FF_CHUNK = 1024
LANES = 128
SUBLANES = 8
HALO = SUBLANES
ACC_ROWS = HEAD_DIM + 2 * SUBLANES
LOG2_E = math.log2(math.e)
QUERY_BLOCKS = 2
M_FLOOR = -1e30
VMEM_LIMIT_BYTES = 56 * 1024 * 1024


def _rms(x, axis=-1):
    return lax.rsqrt(jnp.mean(x * x, axis=axis, keepdims=True) + EPS)


def _dot(a, b):
    return jnp.dot(a, b, preferred_element_type=F32)


def _mod_kernel(c_ref, w_ref, b_ref, o_ref):
    c = c_ref[...]
    c_act = (c * jax.nn.sigmoid(c)).astype(BF16)
    o_ref[0] = _dot(c_act, w_ref[0].astype(BF16)) + b_ref[0]


def _modulation(c, w_ada, b_ada):
    depth, d_model, n_out = w_ada.shape
    batch = c.shape[0]
    return pl.pallas_call(
        _mod_kernel,
        grid=(depth, n_out // MOD_COL_TILE),
        in_specs=[
            pl.BlockSpec((batch, d_model), lambda l, j: (0, 0)),
            pl.BlockSpec((1, d_model, MOD_COL_TILE), lambda l, j: (l, 0, j)),
            pl.BlockSpec((1, 1, MOD_COL_TILE), lambda l, j: (l, 0, j)),
        ],
        out_specs=pl.BlockSpec((1, batch, MOD_COL_TILE), lambda l, j: (l, 0, j)),
        out_shape=jax.ShapeDtypeStruct((depth, batch, n_out), F32),
        compiler_params=pltpu.CompilerParams(
            dimension_semantics=("arbitrary", "arbitrary"), vmem_limit_bytes=VMEM_LIMIT_BYTES),
        name="adaln_modulation",
    )(c, w_ada, b_ada.reshape(depth, 1, n_out))


def _scan_linear_recurrence(a, u):
    n = a.shape[0]
    row = lax.broadcasted_iota(jnp.int32, a.shape, 0)
    d = 1
    while d < n:
        keep = row >= d
        a_prev = jnp.where(keep, pltpu.roll(a, d, 0), 1.0)
        u_prev = jnp.where(keep, pltpu.roll(u, d, 0), 0.0)
        u = a * u_prev + u
        a = a * a_prev
        d *= 2
    return a, u


def _gelu_tanh(x):
    return 0.5 * x * (1.0 + jnp.tanh(math.sqrt(2.0 / math.pi) * (x + 0.044715 * (x * x * x))))


def _softplus(z):
    return jnp.maximum(z, 0.0) + jnp.log1p(jnp.exp(-jnp.abs(z)))


def _mixer_in_kernel(x_ref, mod_ref, ln1_ref, w_in_ref, wq_ref, wv_ref, qgt_ref, kg_ref, headsum_ref, scw_ref,
                     lcw_ref, lcb_ref, wa_ref, ba_ref, wx_ref, bx_ref, lam_ref, mng_ref,
                     qT_ref, k_ref, vT_ref, kmean_ref, ycl_ref,
                     cu_buf, lx_buf, h_carry, *, d_attn, d_conv, d_lru):
    s = pl.program_id(1)
    tm = x_ref.shape[1]
    n_blk = tm // MOBA_BLOCK

    @pl.when(s == 0)
    def _():
        cu_buf[0:HALO, :] = jnp.zeros((HALO, d_conv), F32)
        lx_buf[0:HALO, :] = jnp.zeros((HALO, d_lru), F32)
        h_carry[...] = jnp.zeros_like(h_carry)

    x = x_ref[0]
    m = mod_ref[0, 0]
    shift1, scale1 = m[0:1], m[1:2]
    h = ((x * _rms(x) * ln1_ref[...]) * (1.0 + scale1) + shift1).astype(BF16)

    def proj(off, width):
        return _dot(h, w_in_ref[0, :, off:off + width])

    o_q, o_k, o_v = 0, d_attn, 2 * d_attn
    o_b = 3 * d_attn
    o_c, o_u = o_b + d_conv, o_b + 2 * d_conv
    o_lx = o_b + 3 * d_conv
    o_lg = o_lx + d_lru

    def head_norm(t, g_ref):
        ms = _dot((t * t).astype(BF16), headsum_ref[...])
        return t * lax.rsqrt(ms + EPS) * g_ref[...]

    qv_t = lax.dot_general(jnp.concatenate([wq_ref[0], wv_ref[0]], axis=0), h, (((1,), (1,)), ((), ())),
                           preferred_element_type=F32)
    q_t = qv_t[0:d_attn]
    sq = q_t * q_t
    inv = jnp.concatenate(
        [jnp.broadcast_to(lax.rsqrt(jnp.mean(sq[r:r + HEAD_DIM], axis=0, keepdims=True) + EPS), (HEAD_DIM, tm))
         for r in range(0, d_attn, HEAD_DIM)], axis=0)
    gain = jnp.concatenate([qgt_ref[...]] * (tm // qgt_ref.shape[1]), axis=1)
    qT = (q_t * inv * gain * (LOG2_E / math.sqrt(HEAD_DIM))).astype(BF16)
    for c in range(n_blk):
        qT_ref[0, c] = qT[:, c * MOBA_BLOCK:(c + 1) * MOBA_BLOCK]

    k = head_norm(proj(o_k, d_attn), kg_ref)
    k_ref[0] = k.astype(BF16)
    for c in range(n_blk):
        kmean_ref[0, c] = jnp.mean(k[c * MOBA_BLOCK:(c + 1) * MOBA_BLOCK], axis=0, keepdims=True)

    vT = qv_t[d_attn:].astype(BF16)
    for c in range(n_blk):
        vT_ref[0, c] = vT[:, c * MOBA_BLOCK:(c + 1) * MOBA_BLOCK]

    sc_b = proj(o_b, d_conv)
    cu = proj(o_c, d_conv) * proj(o_u, d_conv)
    cu_buf[HALO:HALO + tm, :] = cu
    scw = scw_ref[...]
    conv = (scw[0:1] * cu_buf[HALO - 2:HALO - 2 + tm, :]
            + scw[1:2] * cu_buf[HALO - 1:HALO - 1 + tm, :]
            + scw[2:3] * cu)
    cu_buf[0:HALO, :] = cu[tm - HALO:tm]
    y_conv = sc_b * conv

    lx = proj(o_lx, d_lru)
    lx_buf[HALO:HALO + tm, :] = lx
    lcw = lcw_ref[...]
    xr = (lcw[0:1] * lx_buf[HALO - 3:HALO - 3 + tm, :]
          + lcw[1:2] * lx_buf[HALO - 2:HALO - 2 + tm, :]
          + lcw[2:3] * lx_buf[HALO - 1:HALO - 1 + tm, :]
          + lcw[3:4] * lx) + lcb_ref[...]
    lx_buf[0:HALO, :] = lx[tm - HALO:tm]
    xr_b = xr.astype(BF16)
    r = jax.nn.sigmoid(_dot(xr_b, wa_ref[...]) + ba_ref[...])
    i = jax.nn.sigmoid(_dot(xr_b, wx_ref[...]) + bx_ref[...])
    log_a = (-LRU_C) * r * _softplus(-lam_ref[...])
    a = jnp.exp(log_a)
    t = jnp.tanh(log_a)
    u = jnp.sqrt((-2.0 * t) / (1.0 - t)) * (i * xr)
    a_cum, h_loc = _scan_linear_recurrence(a, u)
    hs = h_loc + a_cum * h_carry[...]
    h_carry[...] = hs[tm - 1:tm]
    y_lru = hs * _gelu_tanh(proj(o_lg, d_lru))

    mng = mng_ref[...]
    ycl_ref[0, :, 0:d_conv] = (y_conv * _rms(y_conv) * mng[:, 0:d_conv]).astype(BF16)
    ycl_ref[0, :, d_conv:d_conv + d_lru] = (y_lru * _rms(y_lru) * mng[:, d_conv:]).astype(BF16)


def _mixer_in(layer, x, mod, ln1_g, w_in_b, w_qkv_t, qg_t, kg, headsum, sc_w, lcw, lcb, wa_bd, ba, wx_bd, bx, lam, mng_cl,
              *, d_attn, d_conv, d_lru):
    batch, seq, d_model = x.shape
    tm = TOKEN_TILE
    n_blk = tm // MOBA_BLOCK
    nb = seq // MOBA_BLOCK

    def const(shape):
        return pl.BlockSpec(shape, lambda b, s: (0,) * len(shape))

    kern = functools.partial(_mixer_in_kernel, d_attn=d_attn, d_conv=d_conv, d_lru=d_lru)
    return pl.pallas_call(
        kern,
        grid=(batch, seq // tm),
        in_specs=[
            pl.BlockSpec((1, tm, d_model), lambda b, s: (b, s, 0)),
            pl.BlockSpec((1, 1, N_MOD, d_model), lambda b, s: (layer, b, 0, 0)),
            const((1, d_model)),
            pl.BlockSpec((1,) + w_in_b.shape[1:], lambda b, s: (layer, 0, 0)),
            pl.BlockSpec((1, d_attn, d_model), lambda b, s: (layer, 0, 0)),
            pl.BlockSpec((1, d_attn, d_model), lambda b, s: (layer, 2, 0)),
            const(qg_t.shape), const((1, d_attn)), const((d_attn, d_attn)),
            const(sc_w.shape), const(lcw.shape), const((1, d_lru)),
            const((d_lru, d_lru)), const((1, d_lru)), const((d_lru, d_lru)), const((1, d_lru)),
            const((1, d_lru)), const((1, d_conv + d_lru)),
        ],
        out_specs=[
            pl.BlockSpec((1, n_blk, d_attn, MOBA_BLOCK), lambda b, s: (b, s, 0, 0)),
            pl.BlockSpec((1, tm, d_attn), lambda b, s: (b, s, 0)),
            pl.BlockSpec((1, n_blk, d_attn, MOBA_BLOCK), lambda b, s: (b, s, 0, 0)),
            pl.BlockSpec((1, n_blk, 1, d_attn), lambda b, s: (b, s, 0, 0)),
            pl.BlockSpec((1, tm, d_conv + d_lru), lambda b, s: (b, s, 0)),
        ],
        out_shape=[
            jax.ShapeDtypeStruct((batch, nb, d_attn, MOBA_BLOCK), BF16),
            jax.ShapeDtypeStruct((batch, seq, d_attn), BF16),
            jax.ShapeDtypeStruct((batch, nb, d_attn, MOBA_BLOCK), BF16),
            jax.ShapeDtypeStruct((batch, nb, 1, d_attn), F32),
            jax.ShapeDtypeStruct((batch, seq, d_conv + d_lru), BF16),
        ],
        scratch_shapes=[
            pltpu.VMEM((HALO + tm, d_conv), F32),
            pltpu.VMEM((HALO + tm, d_lru), F32),
            pltpu.VMEM((1, d_lru), F32),
        ],
        compiler_params=pltpu.CompilerParams(
            dimension_semantics=("arbitrary", "arbitrary"), vmem_limit_bytes=VMEM_LIMIT_BYTES),
        name="mixer_in",
    )(x, mod, ln1_g, w_in_b, w_qkv_t, w_qkv_t, qg_t, kg, headsum, sc_w, lcw, lcb, wa_bd, ba, wx_bd, bx, lam, mng_cl)


def _sublane_fold(x, op):
    tiles = [x[r:r + SUBLANES] for r in range(0, x.shape[0], SUBLANES)]
    while len(tiles) > 1:
        tiles = [op(a, b) for a, b in zip(tiles[0::2], tiles[1::2])] + tiles[len(tiles) & ~1:]
    return tiles[0]


def _moba_kernel(qT_ref, k_ref, vT_ref, kmean_ref, g_ref, o_ref,
                 qcat_ref, bias_ref, s_ref, mx_ref, m_ref, acc_ref):
    g = pl.program_id(1)
    blk = MOBA_BLOCK
    qw = QUERY_BLOCKS * blk
    d_attn = qT_ref.shape[2]
    n_heads = d_attn // HEAD_DIM
    n_pairs = n_heads // 2
    nb = kmean_ref.shape[1]
    pair = 2 * HEAD_DIM
    cols = n_heads * qw
    first_blk = QUERY_BLOCKS * g

    half = lax.broadcasted_iota(jnp.int32, (pair, qw), 0) < HEAD_DIM
    kmean = kmean_ref[0, :, 0, :]

    def query_block_of(shape, axis):
        return (lax.broadcasted_iota(jnp.int32, shape, axis) & (qw - 1)) // blk

    past_q = lax.broadcasted_iota(jnp.int32, (nb, qw), 0) < first_blk + query_block_of((nb, qw), 1)
    for p in range(n_pairs):
        p0 = p * pair
        q_pair = jnp.concatenate([qT_ref[0, c, p0:p0 + pair, :] for c in range(QUERY_BLOCKS)], axis=1)
        zero = jnp.zeros_like(q_pair)
        for hh in range(2):
            hd = 2 * p + hh
            q_m = jnp.where(half if hh == 0 else jnp.logical_not(half), q_pair, zero)
            qcat_ref[p, :, hh * qw:(hh + 1) * qw] = q_m
            gate = _dot(kmean[:, p0:p0 + pair].astype(BF16), q_m)
            bias_ref[0:nb, hd * qw:(hd + 1) * qw] = jnp.where(past_q, gate, -jnp.inf)

    ones_rows = jnp.where(lax.broadcasted_iota(jnp.int32, (ACC_ROWS - HEAD_DIM, blk), 0) == 0,
                          1.0, 0.0).astype(BF16)
    key_pos = lax.broadcasted_iota(jnp.int32, (blk, qw), 0)
    qry_pos = lax.broadcasted_iota(jnp.int32, (blk, qw), 1)

    def score_parts(slot, key_blk, bias_row, diag):
        rows = pl.ds(pl.multiple_of(key_blk * blk, blk), blk)
        lo = 0 if diag is None else diag * blk
        if diag is not None:
            visible = (key_pos + diag * blk <= qry_pos)[:, lo:]

        def head(hd):
            p, hh = divmod(hd, 2)
            c0 = hd * qw + lo
            c1 = (hd + 1) * qw
            s_t = _dot(k_ref[0, rows, p * pair:(p + 1) * pair], qcat_ref[p, :, hh * qw + lo:(hh + 1) * qw])
            if diag is not None:
                s_t = jnp.where(visible, s_t, -jnp.inf)
            s_ref[slot, :, c0:c1] = s_t
            if bias_row is None:
                mx_ref[:, c0:c1] = _sublane_fold(s_t, jnp.maximum)
                return
            mx8 = jnp.maximum(mx_ref[:, c0:c1],
                              _sublane_fold(s_t, jnp.maximum) + bias_ref[pl.ds(bias_row, 1), c0:c1])
            mx_ref[:, c0:c1] = mx8
            m_ref[slot, :, c0:c1] = jnp.maximum(jnp.max(mx8, axis=0, keepdims=True), M_FLOOR)

        return [functools.partial(head, hd) for hd in range(n_heads)]

    def value_parts(slot, m_prev, key_blk, bias_row, lo=0):
        m_cur = m_ref[slot]
        alpha = jnp.exp2(m_prev - m_cur)
        m_eff = m_cur - bias_ref[pl.ds(bias_row, 1), :]

        def head(hd):
            r0 = hd * HEAD_DIM
            c0 = hd * qw + lo
            c1 = (hd + 1) * qw
            p_t = jnp.exp2(s_ref[slot, :, c0:c1] - m_eff[:, c0:c1])
            v_t = jnp.concatenate([vT_ref[0, key_blk, r0:r0 + HEAD_DIM, :], ones_rows], axis=0)
            a0 = hd * ACC_ROWS
            acc_ref[a0:a0 + ACC_ROWS, lo:] = (alpha[:, c0:c1] * acc_ref[a0:a0 + ACC_ROWS, lo:]
                                              + _dot(v_t, p_t.astype(BF16)))

        return [functools.partial(head, hd) for hd in range(n_heads)]

    def interleave(scores, values):
        for score, value in zip(scores, values):
            score()
            value()

    for score in score_parts(0, first_blk, None, 0):
        score()

    gate = bias_ref[0:nb, :]
    blk_i = lax.broadcasted_iota(jnp.int32, (nb, cols), 0)
    blk_f = blk_i.astype(F32)
    keep = jnp.zeros((nb, cols), F32)
    for _ in range(MOBA_TOPK):
        top = jnp.max(gate, axis=0, keepdims=True)
        first = jnp.min(jnp.where(gate == top, blk_f, float(nb)), axis=0, keepdims=True)
        hit = blk_f == first
        keep = jnp.where(hit, 1.0, keep)
        gate = jnp.where(hit, -jnp.inf, gate)
    keep = jnp.where(blk_i < first_blk + query_block_of((nb, cols), 1), keep, 0.0)
    bias_ref[0:nb, :] = jnp.where(keep > 0.0, 0.0, -jnp.inf)
    col_c = query_block_of((1, cols), 1)
    for c in range(QUERY_BLOCKS):
        bias_ref[nb + c:nb + c + 1, :] = jnp.where(col_c > c, bias_ref[pl.ds(first_blk + c, 1), :], 0.0)

    mx8 = mx_ref[...] + bias_ref[nb:nb + 1, :]
    mx_ref[...] = mx8
    m_first = jnp.maximum(jnp.max(mx8, axis=0, keepdims=True), M_FLOOR)
    m_ref[0] = m_first
    m_ref[1] = m_first

    acc_ref[...] = jnp.zeros_like(acc_ref)

    def step_blocks(t):
        return jnp.where(t == 0, first_blk, t - 1), jnp.where(t == 0, nb, t - 1)

    def fused_step(t, cur):
        m_prev = m_ref[cur]
        interleave(score_parts(cur, t - 1, t - 1, None), value_parts(1 - cur, m_prev, *step_blocks(t - 1)))

    def two_steps(u):
        for cur in (1, 0):
            fused_step(2 * u + 2 - cur, cur)

    def loop_body(v, carry):
        two_steps(2 * v)
        two_steps(2 * v + 1)
        return carry

    step_pairs = first_blk // 2
    lax.fori_loop(0, step_pairs // 2, loop_body, 0)

    @pl.when((step_pairs & 1) == 1)
    def _():
        two_steps(step_pairs - 1)

    for c in range(1, QUERY_BLOCKS):
        cur = c & 1
        m_prev = m_ref[cur]
        if c == 1:
            previous = value_parts(1 - cur, m_prev, *step_blocks(first_blk))
        else:
            previous = value_parts(1 - cur, m_prev, first_blk + c - 1, nb + c - 1, lo=(c - 1) * blk)
        interleave(score_parts(cur, first_blk + c, nb + c, c), previous)
    last = QUERY_BLOCKS - 1
    for value in value_parts(last & 1, m_ref[1 - (last & 1)], first_blk + last, nb + last, lo=last * blk):
        value()

    y_t = jnp.concatenate(
        [acc_ref[hd * ACC_ROWS:hd * ACC_ROWS + HEAD_DIM, :]
         / acc_ref[hd * ACC_ROWS + HEAD_DIM:hd * ACC_ROWS + HEAD_DIM + 1, :]
         for hd in range(n_heads)], axis=0)
    y_t = y_t * _rms(y_t, axis=0)
    o_ref[0] = (y_t.T * g_ref[...]).astype(BF16)


def _moba(qT, k, vT, kmean, mng_attn):
    batch, nb, d_attn, blk = qT.shape
    seq = k.shape[1]
    n_heads = d_attn // HEAD_DIM
    qw = QUERY_BLOCKS * blk
    cols = n_heads * qw
    assert QUERY_BLOCKS >= 2 and QUERY_BLOCKS % 2 == 0 and nb % QUERY_BLOCKS == 0

    return pl.pallas_call(
        _moba_kernel,
        grid=(batch, nb // QUERY_BLOCKS),
        in_specs=[
            pl.BlockSpec((1, QUERY_BLOCKS, d_attn, blk), lambda b, g: (b, g, 0, 0)),
            pl.BlockSpec((1, seq, d_attn), lambda b, g: (b, 0, 0)),
            pl.BlockSpec((1, nb, d_attn, blk), lambda b, g: (b, 0, 0, 0)),
            pl.BlockSpec((1, nb, 1, d_attn), lambda b, g: (b, 0, 0, 0)),
            pl.BlockSpec((1, d_attn), lambda b, g: (0, 0)),
        ],
        out_specs=pl.BlockSpec((1, qw, d_attn), lambda b, g: (b, g, 0)),
        out_shape=jax.ShapeDtypeStruct((batch, seq, d_attn), BF16),
        scratch_shapes=[
            pltpu.VMEM((n_heads // 2, 2 * HEAD_DIM, 2 * qw), BF16),
            pltpu.VMEM((nb + SUBLANES, cols), F32),
            pltpu.VMEM((2, blk, cols), F32),
            pltpu.VMEM((SUBLANES, cols), F32),
            pltpu.VMEM((2, 1, cols), F32),
            pltpu.VMEM((n_heads * ACC_ROWS, qw), F32),
        ],
        compiler_params=pltpu.CompilerParams(
            dimension_semantics=("arbitrary", "arbitrary"), vmem_limit_bytes=VMEM_LIMIT_BYTES),
        name="moba_attention",
    )(qT, k, vT, kmean, mng_attn)


def _out_mlp_kernel(x_ref, ya_ref, ycl_ref, mod_ref, ln2_ref, wout_ref, wup_ref, wdown_ref, o_ref):
    d_attn = ya_ref.shape[2]
    d_ff = wup_ref.shape[2]
    x = x_ref[0]
    m = mod_ref[0, 0]
    gate1, shift2, scale2, gate2 = m[2:3], m[3:4], m[4:5], m[5:6]
    mix = _dot(ya_ref[0], wout_ref[0, 0:d_attn, :]) + _dot(ycl_ref[0], wout_ref[0, d_attn:, :])
    x1 = x + gate1 * mix
    h2 = ((x1 * _rms(x1) * ln2_ref[...]) * (1.0 + scale2) + shift2).astype(BF16)
    ff = jnp.zeros_like(x1)
    for c0 in range(0, d_ff, FF_CHUNK):
        up = _dot(h2, wup_ref[0, :, c0:c0 + FF_CHUNK])
        act = jnp.square(jnp.maximum(up, 0.0)).astype(BF16)
        ff = ff + _dot(act, wdown_ref[0, c0:c0 + FF_CHUNK, :])
    o_ref[0] = x1 + gate2 * ff


def _out_mlp(layer, x, ya, ycl, mod, ln2_g, w_out_b, w_up_b, w_down_b):
    batch, seq, d_model = x.shape
    tm = MLP_TOKEN_TILE
    d_attn = ya.shape[2]
    d_cl = ycl.shape[2]

    def layer_weight(w):
        return pl.BlockSpec((1,) + w.shape[1:], lambda b, s: (layer, 0, 0), pipeline_mode=pl.Buffered(1))

    return pl.pallas_call(
        _out_mlp_kernel,
        grid=(batch, seq // tm),
        in_specs=[
            pl.BlockSpec((1, tm, d_model), lambda b, s: (b, s, 0)),
            pl.BlockSpec((1, tm, d_attn), lambda b, s: (b, s, 0)),
            pl.BlockSpec((1, tm, d_cl), lambda b, s: (b, s, 0)),
            pl.BlockSpec((1, 1, N_MOD, d_model), lambda b, s: (layer, b, 0, 0)),
            pl.BlockSpec((1, d_model), lambda b, s: (0, 0)),
            layer_weight(w_out_b), layer_weight(w_up_b), layer_weight(w_down_b),
        ],
        out_specs=pl.BlockSpec((1, tm, d_model), lambda b, s: (b, s, 0)),
        out_shape=jax.ShapeDtypeStruct((batch, seq, d_model), F32),
        compiler_params=pltpu.CompilerParams(
            dimension_semantics=("arbitrary", "arbitrary"), vmem_limit_bytes=VMEM_LIMIT_BYTES),
        name="out_mlp",
    )(x, ya, ycl, mod, ln2_g, w_out_b, w_up_b, w_down_b)


def _block_diag(w):
    n, r, c = w.shape
    eye = jnp.eye(n, dtype=w.dtype)
    return (eye[:, None, :, None] * w[:, :, None, :]).reshape(n * r, n * c)


def kernel(x, c, ln1_g, ln2_g, w_ada, b_ada, w_in, q_norm_g, k_norm_g, sc_w, lru_conv_w, lru_conv_b,
           lru_wa, lru_ba, lru_wx, lru_bx, lru_lambda, mix_norm_g, w_out, w_up, w_down):
    batch, seq, d_model = x.shape
    depth = w_in.shape[0]
    d_conv = sc_w.shape[2]
    d_lru = lru_conv_w.shape[2]
    d_attn = mix_norm_g.shape[1] - d_conv - d_lru
    n_heads = d_attn // HEAD_DIM
    assert seq % TOKEN_TILE == 0 and TOKEN_TILE % MOBA_BLOCK == 0 and seq % MLP_TOKEN_TILE == 0
    assert w_in.shape[2] == 3 * d_attn + 3 * d_conv + 2 * d_lru

    mod = _modulation(c, w_ada, b_ada).reshape(depth, batch, N_MOD, d_model)
    headsum = _block_diag(jnp.full((n_heads, HEAD_DIM, HEAD_DIM), 1.0 / HEAD_DIM, F32)).astype(BF16)

    w_in_b, w_out_b, w_up_b, w_down_b = (w.astype(BF16) for w in (w_in, w_out, w_up, w_down))
    w_qkv_t = jnp.swapaxes(w_in[:, :, 0:3 * d_attn], 1, 2).astype(BF16)
    for l in range(depth):
        row = lambda v: v.reshape(1, -1)
        qT, k, vT, kmean, ycl = _mixer_in(
            l, x, mod, row(ln1_g[l]), w_in_b, w_qkv_t,
            jnp.broadcast_to(jnp.tile(q_norm_g[l], n_heads)[:, None], (d_attn, LANES)),
            row(jnp.tile(k_norm_g[l], n_heads)), headsum,
            sc_w[l], lru_conv_w[l], row(lru_conv_b[l]),
            _block_diag(lru_wa[l]).astype(BF16), row(lru_ba[l]),
            _block_diag(lru_wx[l]).astype(BF16), row(lru_bx[l]),
            row(lru_lambda[l]), row(mix_norm_g[l, d_attn:]),
            d_attn=d_attn, d_conv=d_conv, d_lru=d_lru)
        ya = _moba(qT, k, vT, kmean, row(mix_norm_g[l, :d_attn]))
        x = _out_mlp(l, x, ya, ycl, mod, row(ln2_g[l]), w_out_b, w_up_b, w_down_b)
    return x
```

```python
import functools
import math

import jax
import jax.numpy as jnp
from jax import lax
from jax.experimental import pallas as pl
from jax.experimental.pallas import tpu as pltpu

F32 = jnp.float32
BF16 = jnp.bfloat16

HEAD_DIM = 64
MOBA_BLOCK = 256
MOBA_TOPK = 3
LRU_C = 8.0
N_MOD = 6
EPS = 1e-6

TOKEN_TILE = 512
MLP_TOKEN_TILE = 1024
MOD_COL_TILE = 1536
FF_CHUNK = 1024
LANES = 128
SUBLANES = 8
HALO = SUBLANES
ACC_ROWS = HEAD_DIM + 2 * SUBLANES
LOG2_E = math.log2(math.e)
QUERY_BLOCKS = 2
M_FLOOR = -1e30
VMEM_LIMIT_BYTES = 56 * 1024 * 1024


def _rms(x, axis=-1):
    return lax.rsqrt(jnp.mean(x * x, axis=axis, keepdims=True) + EPS)


def _dot(a, b):
    return jnp.dot(a, b, preferred_element_type=F32)


def _mod_kernel(c_ref, w_ref, b_ref, o_ref):
    c = c_ref[...]
    c_act = (c * jax.nn.sigmoid(c)).astype(BF16)
    o_ref[0] = _dot(c_act, w_ref[0].astype(BF16)) + b_ref[0]


def _modulation(c, w_ada, b_ada):
    depth, d_model, n_out = w_ada.shape
    batch = c.shape[0]
    return pl.pallas_call(
        _mod_kernel,
        grid=(depth, n_out // MOD_COL_TILE),
        in_specs=[
            pl.BlockSpec((batch, d_model), lambda l, j: (0, 0)),
            pl.BlockSpec((1, d_model, MOD_COL_TILE), lambda l, j: (l, 0, j)),
            pl.BlockSpec((1, 1, MOD_COL_TILE), lambda l, j: (l, 0, j)),
        ],
        out_specs=pl.BlockSpec((1, batch, MOD_COL_TILE), lambda l, j: (l, 0, j)),
        out_shape=jax.ShapeDtypeStruct((depth, batch, n_out), F32),
        compiler_params=pltpu.CompilerParams(
            dimension_semantics=("arbitrary", "arbitrary"), vmem_limit_bytes=VMEM_LIMIT_BYTES),
        name="adaln_modulation",
    )(c, w_ada, b_ada.reshape(depth, 1, n_out))


def _transpose_cast_kernel(w_ref, o_ref):
    o_ref[0, 0] = w_ref[0].T.astype(BF16)


def _transposed_qv_weights(w_in, d_attn):
    depth, d_model, _ = w_in.shape
    return pl.pallas_call(
        _transpose_cast_kernel,
        grid=(depth, 2),
        in_specs=[pl.BlockSpec((1, d_model, d_attn), lambda l, j: (l, 0, 2 * j))],
        out_specs=pl.BlockSpec((1, 1, d_attn, d_model), lambda l, j: (l, j, 0, 0)),
        out_shape=jax.ShapeDtypeStruct((depth, 2, d_attn, d_model), BF16),
        compiler_params=pltpu.CompilerParams(
            dimension_semantics=("arbitrary", "arbitrary"), vmem_limit_bytes=VMEM_LIMIT_BYTES),
        name="transpose_qv_weights",
    )(w_in)


def _scan_linear_recurrence(a, u):
    n = a.shape[0]
    row = lax.broadcasted_iota(jnp.int32, a.shape, 0)
    d = 1
    while d < n:
        keep = row >= d
        a_prev = jnp.where(keep, pltpu.roll(a, d, 0), 1.0)
        u_prev = jnp.where(keep, pltpu.roll(u, d, 0), 0.0)
        u = a * u_prev + u
        a = a * a_prev
        d *= 2
    return a, u


def _gelu_tanh(x):
    return 0.5 * x * (1.0 + jnp.tanh(math.sqrt(2.0 / math.pi) * (x + 0.044715 * (x * x * x))))


def _softplus(z):
    return jnp.maximum(z, 0.0) + jnp.log1p(jnp.exp(-jnp.abs(z)))


def _mixer_in_kernel(x_ref, mod_ref, ln1_ref, w_in_ref, wq_ref, wv_ref, qgt_ref, kg_ref, headsum_ref, scw_ref,
                     lcw_ref, lcb_ref, wa_ref, ba_ref, wx_ref, bx_ref, lam_ref, mng_ref,
                     qT_ref, k_ref, vT_ref, kmean_ref, ycl_ref,
                     cu_buf, lx_buf, h_carry, *, d_attn, d_conv, d_lru):
    s = pl.program_id(1)
    tm = x_ref.shape[1]
    n_blk = tm // MOBA_BLOCK

    @pl.when(s == 0)
    def _():
        cu_buf[0:HALO, :] = jnp.zeros((HALO, d_conv), F32)
        lx_buf[0:HALO, :] = jnp.zeros((HALO, d_lru), F32)
        h_carry[...] = jnp.zeros_like(h_carry)

    x = x_ref[0]
    m = mod_ref[0, 0]
    shift1, scale1 = m[0:1], m[1:2]
    h = ((x * _rms(x) * ln1_ref[...]) * (1.0 + scale1) + shift1).astype(BF16)

    def proj(off, width):
        return _dot(h, w_in_ref[0, :, off:off + width])

    o_q, o_k, o_v = 0, d_attn, 2 * d_attn
    o_b = 3 * d_attn
    o_c, o_u = o_b + d_conv, o_b + 2 * d_conv
    o_lx = o_b + 3 * d_conv
    o_lg = o_lx + d_lru

    def head_norm(t, g_ref):
        ms = _dot((t * t).astype(BF16), headsum_ref[...])
        return t * lax.rsqrt(ms + EPS) * g_ref[...]

    qv_t = lax.dot_general(jnp.concatenate([wq_ref[0, 0], wv_ref[0, 0]], axis=0), h, (((1,), (1,)), ((), ())),
                           preferred_element_type=F32)
    q_t = qv_t[0:d_attn]
    sq = q_t * q_t
    inv = jnp.concatenate(
        [jnp.broadcast_to(lax.rsqrt(jnp.mean(sq[r:r + HEAD_DIM], axis=0, keepdims=True) + EPS), (HEAD_DIM, tm))
         for r in range(0, d_attn, HEAD_DIM)], axis=0)
    gain = jnp.concatenate([qgt_ref[...]] * (tm // qgt_ref.shape[1]), axis=1)
    qT = (q_t * inv * gain * (LOG2_E / math.sqrt(HEAD_DIM))).astype(BF16)
    for c in range(n_blk):
        qT_ref[0, c] = qT[:, c * MOBA_BLOCK:(c + 1) * MOBA_BLOCK]

    k = head_norm(proj(o_k, d_attn), kg_ref)
    k_ref[0] = k.astype(BF16)
    for c in range(n_blk):
        kmean_ref[0, c] = jnp.mean(k[c * MOBA_BLOCK:(c + 1) * MOBA_BLOCK], axis=0, keepdims=True)

    vT = qv_t[d_attn:].astype(BF16)
    for c in range(n_blk):
        vT_ref[0, c] = vT[:, c * MOBA_BLOCK:(c + 1) * MOBA_BLOCK]

    sc_b = proj(o_b, d_conv)
    cu = proj(o_c, d_conv) * proj(o_u, d_conv)
    cu_buf[HALO:HALO + tm, :] = cu
    scw = scw_ref[...]
    conv = (scw[0:1] * cu_buf[HALO - 2:HALO - 2 + tm, :]
            + scw[1:2] * cu_buf[HALO - 1:HALO - 1 + tm, :]
            + scw[2:3] * cu)
    cu_buf[0:HALO, :] = cu[tm - HALO:tm]
    y_conv = sc_b * conv

    lx = proj(o_lx, d_lru)
    lx_buf[HALO:HALO + tm, :] = lx
    lcw = lcw_ref[...]
    xr = (lcw[0:1] * lx_buf[HALO - 3:HALO - 3 + tm, :]
          + lcw[1:2] * lx_buf[HALO - 2:HALO - 2 + tm, :]
          + lcw[2:3] * lx_buf[HALO - 1:HALO - 1 + tm, :]
          + lcw[3:4] * lx) + lcb_ref[...]
    lx_buf[0:HALO, :] = lx[tm - HALO:tm]
    xr_b = xr.astype(BF16)
    r = jax.nn.sigmoid(_dot(xr_b, wa_ref[...]) + ba_ref[...])
    i = jax.nn.sigmoid(_dot(xr_b, wx_ref[...]) + bx_ref[...])
    log_a = (-LRU_C) * r * _softplus(-lam_ref[...])
    a = jnp.exp(log_a)
    t = jnp.tanh(log_a)
    u = jnp.sqrt((-2.0 * t) / (1.0 - t)) * (i * xr)
    a_cum, h_loc = _scan_linear_recurrence(a, u)
    hs = h_loc + a_cum * h_carry[...]
    h_carry[...] = hs[tm - 1:tm]
    y_lru = hs * _gelu_tanh(proj(o_lg, d_lru))

    mng = mng_ref[...]
    ycl_ref[0, :, 0:d_conv] = (y_conv * _rms(y_conv) * mng[:, 0:d_conv]).astype(BF16)
    ycl_ref[0, :, d_conv:d_conv + d_lru] = (y_lru * _rms(y_lru) * mng[:, d_conv:]).astype(BF16)


def _mixer_in(layer, x, mod, ln1_g, w_in_b, w_qv_t, qg_t, kg, headsum, sc_w, lcw, lcb, wa_bd, ba, wx_bd, bx, lam, mng_cl,
              *, d_attn, d_conv, d_lru):
    batch, seq, d_model = x.shape
    tm = TOKEN_TILE
    n_blk = tm // MOBA_BLOCK
    nb = seq // MOBA_BLOCK

    def const(shape):
        return pl.BlockSpec(shape, lambda b, s: (0,) * len(shape))

    kern = functools.partial(_mixer_in_kernel, d_attn=d_attn, d_conv=d_conv, d_lru=d_lru)
    return pl.pallas_call(
        kern,
        grid=(batch, seq // tm),
        in_specs=[
            pl.BlockSpec((1, tm, d_model), lambda b, s: (b, s, 0)),
            pl.BlockSpec((1, 1, N_MOD, d_model), lambda b, s: (layer, b, 0, 0)),
            const((1, d_model)),
            pl.BlockSpec((1,) + w_in_b.shape[1:], lambda b, s: (layer, 0, 0)),
            pl.BlockSpec((1, 1, d_attn, d_model), lambda b, s: (layer, 0, 0, 0)),
            pl.BlockSpec((1, 1, d_attn, d_model), lambda b, s: (layer, 1, 0, 0)),
            const(qg_t.shape), const((1, d_attn)), const((d_attn, d_attn)),
            const(sc_w.shape), const(lcw.shape), const((1, d_lru)),
            const((d_lru, d_lru)), const((1, d_lru)), const((d_lru, d_lru)), const((1, d_lru)),
            const((1, d_lru)), const((1, d_conv + d_lru)),
        ],
        out_specs=[
            pl.BlockSpec((1, n_blk, d_attn, MOBA_BLOCK), lambda b, s: (b, s, 0, 0)),
            pl.BlockSpec((1, tm, d_attn), lambda b, s: (b, s, 0)),
            pl.BlockSpec((1, n_blk, d_attn, MOBA_BLOCK), lambda b, s: (b, s, 0, 0)),
            pl.BlockSpec((1, n_blk, 1, d_attn), lambda b, s: (b, s, 0, 0)),
            pl.BlockSpec((1, tm, d_conv + d_lru), lambda b, s: (b, s, 0)),
        ],
        out_shape=[
            jax.ShapeDtypeStruct((batch, nb, d_attn, MOBA_BLOCK), BF16),
            jax.ShapeDtypeStruct((batch, seq, d_attn), BF16),
            jax.ShapeDtypeStruct((batch, nb, d_attn, MOBA_BLOCK), BF16),
            jax.ShapeDtypeStruct((batch, nb, 1, d_attn), F32),
            jax.ShapeDtypeStruct((batch, seq, d_conv + d_lru), BF16),
        ],
        scratch_shapes=[
            pltpu.VMEM((HALO + tm, d_conv), F32),
            pltpu.VMEM((HALO + tm, d_lru), F32),
            pltpu.VMEM((1, d_lru), F32),
        ],
        compiler_params=pltpu.CompilerParams(
            dimension_semantics=("arbitrary", "arbitrary"), vmem_limit_bytes=VMEM_LIMIT_BYTES),
        name="mixer_in",
    )(x, mod, ln1_g, w_in_b, w_qv_t, w_qv_t, qg_t, kg, headsum, sc_w, lcw, lcb, wa_bd, ba, wx_bd, bx, lam, mng_cl)


def _sublane_fold(x, op):
    tiles = [x[r:r + SUBLANES] for r in range(0, x.shape[0], SUBLANES)]
    while len(tiles) > 1:
        tiles = [op(a, b) for a, b in zip(tiles[0::2], tiles[1::2])] + tiles[len(tiles) & ~1:]
    return tiles[0]


def _moba_kernel(qT_ref, k_ref, vT_ref, kmean_ref, g_ref, o_ref,
                 qcat_ref, bias_ref, s_ref, mx_ref, m_ref, acc_ref):
    g = pl.program_id(1)
    blk = MOBA_BLOCK
    qw = QUERY_BLOCKS * blk
    d_attn = qT_ref.shape[2]
    n_heads = d_attn // HEAD_DIM
    n_pairs = n_heads // 2
    nb = kmean_ref.shape[1]
    pair = 2 * HEAD_DIM
    cols = n_heads * qw
    first_blk = QUERY_BLOCKS * g

    half = lax.broadcasted_iota(jnp.int32, (pair, qw), 0) < HEAD_DIM
    kmean = kmean_ref[0, :, 0, :]

    def query_block_of(shape, axis):
        return (lax.broadcasted_iota(jnp.int32, shape, axis) & (qw - 1)) // blk

    past_q = lax.broadcasted_iota(jnp.int32, (nb, qw), 0) < first_blk + query_block_of((nb, qw), 1)
    for p in range(n_pairs):
        p0 = p * pair
        q_pair = jnp.concatenate([qT_ref[0, c, p0:p0 + pair, :] for c in range(QUERY_BLOCKS)], axis=1)
        zero = jnp.zeros_like(q_pair)
        for hh in range(2):
            hd = 2 * p + hh
            q_m = jnp.where(half if hh == 0 else jnp.logical_not(half), q_pair, zero)
            qcat_ref[p, :, hh * qw:(hh + 1) * qw] = q_m
            gate = _dot(kmean[:, p0:p0 + pair].astype(BF16), q_m)
            bias_ref[0:nb, hd * qw:(hd + 1) * qw] = jnp.where(past_q, gate, -jnp.inf)

    ones_rows = jnp.where(lax.broadcasted_iota(jnp.int32, (ACC_ROWS - HEAD_DIM, blk), 0) == 0,
                          1.0, 0.0).astype(BF16)
    key_pos = lax.broadcasted_iota(jnp.int32, (blk, qw), 0)
    qry_pos = lax.broadcasted_iota(jnp.int32, (blk, qw), 1)

    def score_parts(slot, key_blk, bias_row, diag):
        rows = pl.ds(pl.multiple_of(key_blk * blk, blk), blk)
        lo = 0 if diag is None else diag * blk
        if diag is not None:
            visible = (key_pos + diag * blk <= qry_pos)[:, lo:]

        def head(hd):
            p, hh = divmod(hd, 2)
            c0 = hd * qw + lo
            c1 = (hd + 1) * qw
            s_t = _dot(k_ref[0, rows, p * pair:(p + 1) * pair], qcat_ref[p, :, hh * qw + lo:(hh + 1) * qw])
            if diag is not None:
                s_t = jnp.where(visible, s_t, -jnp.inf)
            s_ref[slot, :, c0:c1] = s_t
            if bias_row is None:
                mx_ref[:, c0:c1] = _sublane_fold(s_t, jnp.maximum)
                return
            mx8 = jnp.maximum(mx_ref[:, c0:c1],
                              _sublane_fold(s_t, jnp.maximum) + bias_ref[pl.ds(bias_row, 1), c0:c1])
            mx_ref[:, c0:c1] = mx8
            m_ref[slot, :, c0:c1] = jnp.maximum(jnp.max(mx8, axis=0, keepdims=True), M_FLOOR)

        return [functools.partial(head, hd) for hd in range(n_heads)]

    def value_parts(slot, m_prev, key_blk, bias_row, lo=0):
        m_cur = m_ref[slot]
        alpha = jnp.exp2(m_prev - m_cur)
        m_eff = m_cur - bias_ref[pl.ds(bias_row, 1), :]

        def head(hd):
            r0 = hd * HEAD_DIM
            c0 = hd * qw + lo
            c1 = (hd + 1) * qw
            p_t = jnp.exp2(s_ref[slot, :, c0:c1] - m_eff[:, c0:c1])
            v_t = jnp.concatenate([vT_ref[0, key_blk, r0:r0 + HEAD_DIM, :], ones_rows], axis=0)
            a0 = hd * ACC_ROWS
            acc_ref[a0:a0 + ACC_ROWS, lo:] = (alpha[:, c0:c1] * acc_ref[a0:a0 + ACC_ROWS, lo:]
                                              + _dot(v_t, p_t.astype(BF16)))

        return [functools.partial(head, hd) for hd in range(n_heads)]

    def interleave(scores, values):
        for score, value in zip(scores, values):
            score()
            value()

    for score in score_parts(0, first_blk, None, 0):
        score()

    gate = bias_ref[0:nb, :]
    blk_i = lax.broadcasted_iota(jnp.int32, (nb, cols), 0)
    blk_f = blk_i.astype(F32)
    keep = jnp.zeros((nb, cols), F32)
    for _ in range(MOBA_TOPK):
        top = jnp.max(gate, axis=0, keepdims=True)
        first = jnp.min(jnp.where(gate == top, blk_f, float(nb)), axis=0, keepdims=True)
        hit = blk_f == first
        keep = jnp.where(hit, 1.0, keep)
        gate = jnp.where(hit, -jnp.inf, gate)
    keep = jnp.where(blk_i < first_blk + query_block_of((nb, cols), 1), keep, 0.0)
    bias_ref[0:nb, :] = jnp.where(keep > 0.0, 0.0, -jnp.inf)
    col_c = query_block_of((1, cols), 1)
    for c in range(QUERY_BLOCKS):
        bias_ref[nb + c:nb + c + 1, :] = jnp.where(col_c > c, bias_ref[pl.ds(first_blk + c, 1), :], 0.0)

    mx8 = mx_ref[...] + bias_ref[nb:nb + 1, :]
    mx_ref[...] = mx8
    m_first = jnp.maximum(jnp.max(mx8, axis=0, keepdims=True), M_FLOOR)
    m_ref[0] = m_first
    m_ref[1] = m_first

    acc_ref[...] = jnp.zeros_like(acc_ref)

    def step_blocks(t):
        return jnp.where(t == 0, first_blk, t - 1), jnp.where(t == 0, nb, t - 1)

    def fused_step(t, cur):
        m_prev = m_ref[cur]
        interleave(score_parts(cur, t - 1, t - 1, None), value_parts(1 - cur, m_prev, *step_blocks(t - 1)))

    def two_steps(u):
        for cur in (1, 0):
            fused_step(2 * u + 2 - cur, cur)

    def loop_body(v, carry):
        two_steps(2 * v)
        two_steps(2 * v + 1)
        return carry

    step_pairs = first_blk // 2
    lax.fori_loop(0, step_pairs // 2, loop_body, 0)

    @pl.when((step_pairs & 1) == 1)
    def _():
        two_steps(step_pairs - 1)

    for c in range(1, QUERY_BLOCKS):
        cur = c & 1
        m_prev = m_ref[cur]
        if c == 1:
            previous = value_parts(1 - cur, m_prev, *step_blocks(first_blk))
        else:
            previous = value_parts(1 - cur, m_prev, first_blk + c - 1, nb + c - 1, lo=(c - 1) * blk)
        interleave(score_parts(cur, first_blk + c, nb + c, c), previous)
    last = QUERY_BLOCKS - 1
    for value in value_parts(last & 1, m_ref[1 - (last & 1)], first_blk + last, nb + last, lo=last * blk):
        value()

    y_t = jnp.concatenate(
        [acc_ref[hd * ACC_ROWS:hd * ACC_ROWS + HEAD_DIM, :]
         / acc_ref[hd * ACC_ROWS + HEAD_DIM:hd * ACC_ROWS + HEAD_DIM + 1, :]
         for hd in range(n_heads)], axis=0)
    y_t = y_t * _rms(y_t, axis=0)
    o_ref[0] = (y_t.T * g_ref[...]).astype(BF16)


def _moba(qT, k, vT, kmean, mng_attn):
    batch, nb, d_attn, blk = qT.shape
    seq = k.shape[1]
    n_heads = d_attn // HEAD_DIM
    qw = QUERY_BLOCKS * blk
    cols = n_heads * qw
    assert QUERY_BLOCKS >= 2 and QUERY_BLOCKS % 2 == 0 and nb % QUERY_BLOCKS == 0

    return pl.pallas_call(
        _moba_kernel,
        grid=(batch, nb // QUERY_BLOCKS),
        in_specs=[
            pl.BlockSpec((1, QUERY_BLOCKS, d_attn, blk), lambda b, g: (b, g, 0, 0)),
            pl.BlockSpec((1, seq, d_attn), lambda b, g: (b, 0, 0)),
            pl.BlockSpec((1, nb, d_attn, blk), lambda b, g: (b, 0, 0, 0)),
            pl.BlockSpec((1, nb, 1, d_attn), lambda b, g: (b, 0, 0, 0)),
            pl.BlockSpec((1, d_attn), lambda b, g: (0, 0)),
        ],
        out_specs=pl.BlockSpec((1, qw, d_attn), lambda b, g: (b, g, 0)),
        out_shape=jax.ShapeDtypeStruct((batch, seq, d_attn), BF16),
        scratch_shapes=[
            pltpu.VMEM((n_heads // 2, 2 * HEAD_DIM, 2 * qw), BF16),
            pltpu.VMEM((nb + SUBLANES, cols), F32),
            pltpu.VMEM((2, blk, cols), F32),
            pltpu.VMEM((SUBLANES, cols), F32),
            pltpu.VMEM((2, 1, cols), F32),
            pltpu.VMEM((n_heads * ACC_ROWS, qw), F32),
        ],
        compiler_params=pltpu.CompilerParams(
            dimension_semantics=("arbitrary", "arbitrary"), vmem_limit_bytes=VMEM_LIMIT_BYTES),
        name="moba_attention",
    )(qT, k, vT, kmean, mng_attn)


def _out_mlp_kernel(x_ref, ya_ref, ycl_ref, mod_ref, ln2_ref, wout_ref, wup_ref, wdown_ref, o_ref):
    d_attn = ya_ref.shape[2]
    d_ff = wup_ref.shape[2]
    x = x_ref[0]
    m = mod_ref[0, 0]
    gate1, shift2, scale2, gate2 = m[2:3], m[3:4], m[4:5], m[5:6]
    mix = _dot(ya_ref[0], wout_ref[0, 0:d_attn, :]) + _dot(ycl_ref[0], wout_ref[0, d_attn:, :])
    x1 = x + gate1 * mix
    h2 = ((x1 * _rms(x1) * ln2_ref[...]) * (1.0 + scale2) + shift2).astype(BF16)
    ff = jnp.zeros_like(x1)
    for c0 in range(0, d_ff, FF_CHUNK):
        up = _dot(h2, wup_ref[0, :, c0:c0 + FF_CHUNK])
        act = jnp.square(jnp.maximum(up, 0.0)).astype(BF16)
        ff = ff + _dot(act, wdown_ref[0, c0:c0 + FF_CHUNK, :])
    o_ref[0] = x1 + gate2 * ff


def _out_mlp(layer, x, ya, ycl, mod, ln2_g, w_out_b, w_up_b, w_down_b):
    batch, seq, d_model = x.shape
    tm = MLP_TOKEN_TILE
    d_attn = ya.shape[2]
    d_cl = ycl.shape[2]

    def layer_weight(w):
        return pl.BlockSpec((1,) + w.shape[1:], lambda b, s: (layer, 0, 0), pipeline_mode=pl.Buffered(1))

    return pl.pallas_call(
        _out_mlp_kernel,
        grid=(batch, seq // tm),
        in_specs=[
            pl.BlockSpec((1, tm, d_model), lambda b, s: (b, s, 0)),
            pl.BlockSpec((1, tm, d_attn), lambda b, s: (b, s, 0)),
            pl.BlockSpec((1, tm, d_cl), lambda b, s: (b, s, 0)),
            pl.BlockSpec((1, 1, N_MOD, d_model), lambda b, s: (layer, b, 0, 0)),
            pl.BlockSpec((1, d_model), lambda b, s: (0, 0)),
            layer_weight(w_out_b), layer_weight(w_up_b), layer_weight(w_down_b),
        ],
        out_specs=pl.BlockSpec((1, tm, d_model), lambda b, s: (b, s, 0)),
        out_shape=jax.ShapeDtypeStruct((batch, seq, d_model), F32),
        compiler_params=pltpu.CompilerParams(
            dimension_semantics=("arbitrary", "arbitrary"), vmem_limit_bytes=VMEM_LIMIT_BYTES),
        name="out_mlp",
    )(x, ya, ycl, mod, ln2_g, w_out_b, w_up_b, w_down_b)


def _block_diag(w):
    n, r, c = w.shape
    eye = jnp.eye(n, dtype=w.dtype)
    return (eye[:, None, :, None] * w[:, :, None, :]).reshape(n * r, n * c)


def kernel(x, c, ln1_g, ln2_g, w_ada, b_ada, w_in, q_norm_g, k_norm_g, sc_w, lru_conv_w, lru_conv_b,
           lru_wa, lru_ba, lru_wx, lru_bx, lru_lambda, mix_norm_g, w_out, w_up, w_down):
    batch, seq, d_model = x.shape
    depth = w_in.shape[0]
    d_conv = sc_w.shape[2]
    d_lru = lru_conv_w.shape[2]
    d_attn = mix_norm_g.shape[1] - d_conv - d_lru
    n_heads = d_attn // HEAD_DIM
    assert seq % TOKEN_TILE == 0 and TOKEN_TILE % MOBA_BLOCK == 0 and seq % MLP_TOKEN_TILE == 0
    assert w_in.shape[2] == 3 * d_attn + 3 * d_conv + 2 * d_lru

    mod = _modulation(c, w_ada, b_ada).reshape(depth, batch, N_MOD, d_model)
    headsum = _block_diag(jnp.full((n_heads, HEAD_DIM, HEAD_DIM), 1.0 / HEAD_DIM, F32)).astype(BF16)

    w_in_b, w_out_b, w_up_b, w_down_b = (w.astype(BF16) for w in (w_in, w_out, w_up, w_down))
    w_qv_t = _transposed_qv_weights(w_in, d_attn)
    for l in range(depth):
        row = lambda v: v.reshape(1, -1)
        qT, k, vT, kmean, ycl = _mixer_in(
            l, x, mod, row(ln1_g[l]), w_in_b, w_qv_t,
            jnp.broadcast_to(jnp.tile(q_norm_g[l], n_heads)[:, None], (d_attn, LANES)),
            row(jnp.tile(k_norm_g[l], n_heads)), headsum,
            sc_w[l], lru_conv_w[l], row(lru_conv_b[l]),
            _block_diag(lru_wa[l]).astype(BF16), row(lru_ba[l]),
            _block_diag(lru_wx[l]).astype(BF16), row(lru_bx[l]),
            row(lru_lambda[l]), row(mix_norm_g[l, d_attn:]),
            d_attn=d_attn, d_conv=d_conv, d_lru=d_lru)
        ya = _moba(qT, k, vT, kmean, row(mix_norm_g[l, :d_attn]))
        x = _out_mlp(l, x, ya, ycl, mod, row(ln2_g[l]), w_out_b, w_up_b, w_down_b)
    return x
```

```python
import functools
import math

import jax
import jax.numpy as jnp
from jax import lax
from jax.experimental import pallas as pl
from jax.experimental.pallas import tpu as pltpu

F32 = jnp.float32
BF16 = jnp.bfloat16

HEAD_DIM = 64
MOBA_BLOCK = 256
MOBA_TOPK = 3
LRU_C = 8.0
N_MOD = 6
EPS = 1e-6

TOKEN_TILE = 512
MLP_TOKEN_TILE = 1024
MOD_COL_TILE = 1536
FF_CHUNK = 1024
LANES = 128
SUBLANES = 8
HALO = SUBLANES
ACC_ROWS = HEAD_DIM + 2 * SUBLANES
LOG2_E = math.log2(math.e)
QUERY_BLOCKS = 2
M_FLOOR = -1e30
VMEM_LIMIT_BYTES = 56 * 1024 * 1024


def _rms(x, axis=-1):
    return lax.rsqrt(jnp.mean(x * x, axis=axis, keepdims=True) + EPS)


def _dot(a, b):
    return jnp.dot(a, b, preferred_element_type=F32)


def _mod_kernel(c_ref, w_ref, b_ref, o_ref):
    c = c_ref[...]
    c_act = (c * jax.nn.sigmoid(c)).astype(BF16)
    o_ref[0] = _dot(c_act, w_ref[0].astype(BF16)) + b_ref[0]


def _modulation(c, w_ada, b_ada):
    depth, d_model, n_out = w_ada.shape
    batch = c.shape[0]
    return pl.pallas_call(
        _mod_kernel,
        grid=(depth, n_out // MOD_COL_TILE),
        in_specs=[
            pl.BlockSpec((batch, d_model), lambda l, j: (0, 0)),
            pl.BlockSpec((1, d_model, MOD_COL_TILE), lambda l, j: (l, 0, j)),
            pl.BlockSpec((1, 1, MOD_COL_TILE), lambda l, j: (l, 0, j)),
        ],
        out_specs=pl.BlockSpec((1, batch, MOD_COL_TILE), lambda l, j: (l, 0, j)),
        out_shape=jax.ShapeDtypeStruct((depth, batch, n_out), F32),
        compiler_params=pltpu.CompilerParams(
            dimension_semantics=("arbitrary", "arbitrary"), vmem_limit_bytes=VMEM_LIMIT_BYTES),
        name="adaln_modulation",
    )(c, w_ada, b_ada.reshape(depth, 1, n_out))


def _transpose_cast_kernel(w_ref, o_ref):
    o_ref[0, 0] = w_ref[0].T.astype(BF16)


def _transposed_qkv_weights(w_in, d_attn):
    depth, d_model, _ = w_in.shape
    return pl.pallas_call(
        _transpose_cast_kernel,
        grid=(depth, 3),
        in_specs=[pl.BlockSpec((1, d_model, d_attn), lambda l, j: (l, 0, j))],
        out_specs=pl.BlockSpec((1, 1, d_attn, d_model), lambda l, j: (l, j, 0, 0)),
        out_shape=jax.ShapeDtypeStruct((depth, 3, d_attn, d_model), BF16),
        compiler_params=pltpu.CompilerParams(
            dimension_semantics=("arbitrary", "arbitrary"), vmem_limit_bytes=VMEM_LIMIT_BYTES),
        name="transpose_qkv_weights",
    )(w_in)


def _scan_linear_recurrence(a, u):
    n = a.shape[0]
    row = lax.broadcasted_iota(jnp.int32, a.shape, 0)
    d = 1
    while d < n:
        keep = row >= d
        a_prev = jnp.where(keep, pltpu.roll(a, d, 0), 1.0)
        u_prev = jnp.where(keep, pltpu.roll(u, d, 0), 0.0)
        u = a * u_prev + u
        a = a * a_prev
        d *= 2
    return a, u


def _gelu_tanh(x):
    return 0.5 * x * (1.0 + jnp.tanh(math.sqrt(2.0 / math.pi) * (x + 0.044715 * (x * x * x))))


def _softplus(z):
    return jnp.maximum(z, 0.0) + jnp.log1p(jnp.exp(-jnp.abs(z)))


def _mixer_in_kernel(x_ref, mod_ref, ln1_ref, w_in_ref, wq_ref, wk_ref, wv_ref, qgt_ref, kgt_ref, scw_ref,
                     lcw_ref, lcb_ref, wa_ref, ba_ref, wx_ref, bx_ref, lam_ref, mng_ref,
                     qT_ref, k_ref, vT_ref, kmean_ref, ycl_ref,
                     cu_buf, lx_buf, h_carry, *, d_attn, d_conv, d_lru):
    s = pl.program_id(1)
    tm = x_ref.shape[1]
    n_blk = tm // MOBA_BLOCK

    @pl.when(s == 0)
    def _():
        cu_buf[0:HALO, :] = jnp.zeros((HALO, d_conv), F32)
        lx_buf[0:HALO, :] = jnp.zeros((HALO, d_lru), F32)
        h_carry[...] = jnp.zeros_like(h_carry)

    x = x_ref[0]
    m = mod_ref[0, 0]
    shift1, scale1 = m[0:1], m[1:2]
    h = ((x * _rms(x) * ln1_ref[...]) * (1.0 + scale1) + shift1).astype(BF16)

    def proj(off, width):
        return _dot(h, w_in_ref[0, :, off:off + width])

    o_q, o_k, o_v = 0, d_attn, 2 * d_attn
    o_b = 3 * d_attn
    o_c, o_u = o_b + d_conv, o_b + 2 * d_conv
    o_lx = o_b + 3 * d_conv
    o_lg = o_lx + d_lru

    qkv_t = lax.dot_general(jnp.concatenate([wq_ref[0, 0], wk_ref[0, 0], wv_ref[0, 0]], axis=0), h,
                            (((1,), (1,)), ((), ())), preferred_element_type=F32)

    def head_norm_t(t, g_ref):
        sq = t * t
        inv = jnp.concatenate(
            [jnp.broadcast_to(lax.rsqrt(jnp.mean(sq[r:r + HEAD_DIM], axis=0, keepdims=True) + EPS), (HEAD_DIM, tm))
             for r in range(0, d_attn, HEAD_DIM)], axis=0)
        return t * inv * jnp.concatenate([g_ref[...]] * (tm // g_ref.shape[1]), axis=1)

    qT = (head_norm_t(qkv_t[0:d_attn], qgt_ref) * (LOG2_E / math.sqrt(HEAD_DIM))).astype(BF16)
    for c in range(n_blk):
        qT_ref[0, c] = qT[:, c * MOBA_BLOCK:(c + 1) * MOBA_BLOCK]

    k = head_norm_t(qkv_t[d_attn:2 * d_attn], kgt_ref).T
    k_ref[0] = k.astype(BF16)
    for c in range(n_blk):
        kmean_ref[0, c] = jnp.mean(k[c * MOBA_BLOCK:(c + 1) * MOBA_BLOCK], axis=0, keepdims=True)

    vT = qkv_t[2 * d_attn:].astype(BF16)
    for c in range(n_blk):
        vT_ref[0, c] = vT[:, c * MOBA_BLOCK:(c + 1) * MOBA_BLOCK]

    sc_b = proj(o_b, d_conv)
    cu = proj(o_c, d_conv) * proj(o_u, d_conv)
    cu_buf[HALO:HALO + tm, :] = cu
    scw = scw_ref[...]
    conv = (scw[0:1] * cu_buf[HALO - 2:HALO - 2 + tm, :]
            + scw[1:2] * cu_buf[HALO - 1:HALO - 1 + tm, :]
            + scw[2:3] * cu)
    cu_buf[0:HALO, :] = cu[tm - HALO:tm]
    y_conv = sc_b * conv

    lx = proj(o_lx, d_lru)
    lx_buf[HALO:HALO + tm, :] = lx
    lcw = lcw_ref[...]
    xr = (lcw[0:1] * lx_buf[HALO - 3:HALO - 3 + tm, :]
          + lcw[1:2] * lx_buf[HALO - 2:HALO - 2 + tm, :]
          + lcw[2:3] * lx_buf[HALO - 1:HALO - 1 + tm, :]
          + lcw[3:4] * lx) + lcb_ref[...]
    lx_buf[0:HALO, :] = lx[tm - HALO:tm]
    xr_b = xr.astype(BF16)
    r = jax.nn.sigmoid(_dot(xr_b, wa_ref[...]) + ba_ref[...])
    i = jax.nn.sigmoid(_dot(xr_b, wx_ref[...]) + bx_ref[...])
    log_a = (-LRU_C) * r * _softplus(-lam_ref[...])
    a = jnp.exp(log_a)
    t = jnp.tanh(log_a)
    u = jnp.sqrt((-2.0 * t) / (1.0 - t)) * (i * xr)
    a_cum, h_loc = _scan_linear_recurrence(a, u)
    hs = h_loc + a_cum * h_carry[...]
    h_carry[...] = hs[tm - 1:tm]
    y_lru = hs * _gelu_tanh(proj(o_lg, d_lru))

    mng = mng_ref[...]
    ycl_ref[0, :, 0:d_conv] = (y_conv * _rms(y_conv) * mng[:, 0:d_conv]).astype(BF16)
    ycl_ref[0, :, d_conv:d_conv + d_lru] = (y_lru * _rms(y_lru) * mng[:, d_conv:]).astype(BF16)


def _mixer_in(layer, x, mod, ln1_g, w_in_b, w_qkv_t, qg_t, kg_t, sc_w, lcw, lcb, wa_bd, ba, wx_bd, bx, lam, mng_cl,
              *, d_attn, d_conv, d_lru):
    batch, seq, d_model = x.shape
    tm = TOKEN_TILE
    n_blk = tm // MOBA_BLOCK
    nb = seq // MOBA_BLOCK

    def const(shape):
        return pl.BlockSpec(shape, lambda b, s: (0,) * len(shape))

    kern = functools.partial(_mixer_in_kernel, d_attn=d_attn, d_conv=d_conv, d_lru=d_lru)
    return pl.pallas_call(
        kern,
        grid=(batch, seq // tm),
        in_specs=[
            pl.BlockSpec((1, tm, d_model), lambda b, s: (b, s, 0)),
            pl.BlockSpec((1, 1, N_MOD, d_model), lambda b, s: (layer, b, 0, 0)),
            const((1, d_model)),
            pl.BlockSpec((1,) + w_in_b.shape[1:], lambda b, s: (layer, 0, 0)),
            pl.BlockSpec((1, 1, d_attn, d_model), lambda b, s: (layer, 0, 0, 0)),
            pl.BlockSpec((1, 1, d_attn, d_model), lambda b, s: (layer, 1, 0, 0)),
            pl.BlockSpec((1, 1, d_attn, d_model), lambda b, s: (layer, 2, 0, 0)),
            const(qg_t.shape), const(kg_t.shape),
            const(sc_w.shape), const(lcw.shape), const((1, d_lru)),
            const((d_lru, d_lru)), const((1, d_lru)), const((d_lru, d_lru)), const((1, d_lru)),
            const((1, d_lru)), const((1, d_conv + d_lru)),
        ],
        out_specs=[
            pl.BlockSpec((1, n_blk, d_attn, MOBA_BLOCK), lambda b, s: (b, s, 0, 0)),
            pl.BlockSpec((1, tm, d_attn), lambda b, s: (b, s, 0)),
            pl.BlockSpec((1, n_blk, d_attn, MOBA_BLOCK), lambda b, s: (b, s, 0, 0)),
            pl.BlockSpec((1, n_blk, 1, d_attn), lambda b, s: (b, s, 0, 0)),
            pl.BlockSpec((1, tm, d_conv + d_lru), lambda b, s: (b, s, 0)),
        ],
        out_shape=[
            jax.ShapeDtypeStruct((batch, nb, d_attn, MOBA_BLOCK), BF16),
            jax.ShapeDtypeStruct((batch, seq, d_attn), BF16),
            jax.ShapeDtypeStruct((batch, nb, d_attn, MOBA_BLOCK), BF16),
            jax.ShapeDtypeStruct((batch, nb, 1, d_attn), F32),
            jax.ShapeDtypeStruct((batch, seq, d_conv + d_lru), BF16),
        ],
        scratch_shapes=[
            pltpu.VMEM((HALO + tm, d_conv), F32),
            pltpu.VMEM((HALO + tm, d_lru), F32),
            pltpu.VMEM((1, d_lru), F32),
        ],
        compiler_params=pltpu.CompilerParams(
            dimension_semantics=("arbitrary", "arbitrary"), vmem_limit_bytes=VMEM_LIMIT_BYTES),
        name="mixer_in",
    )(x, mod, ln1_g, w_in_b, w_qkv_t, w_qkv_t, w_qkv_t, qg_t, kg_t, sc_w, lcw, lcb, wa_bd, ba, wx_bd, bx, lam, mng_cl)


def _sublane_fold(x, op):
    tiles = [x[r:r + SUBLANES] for r in range(0, x.shape[0], SUBLANES)]
    while len(tiles) > 1:
        tiles = [op(a, b) for a, b in zip(tiles[0::2], tiles[1::2])] + tiles[len(tiles) & ~1:]
    return tiles[0]


def _moba_kernel(qT_ref, k_ref, vT_ref, kmean_ref, g_ref, o_ref,
                 qcat_ref, bias_ref, s_ref, mx_ref, m_ref, acc_ref):
    g = pl.program_id(1)
    blk = MOBA_BLOCK
    qw = QUERY_BLOCKS * blk
    d_attn = qT_ref.shape[2]
    n_heads = d_attn // HEAD_DIM
    n_pairs = n_heads // 2
    nb = kmean_ref.shape[1]
    pair = 2 * HEAD_DIM
    cols = n_heads * qw
    first_blk = QUERY_BLOCKS * g

    half = lax.broadcasted_iota(jnp.int32, (pair, qw), 0) < HEAD_DIM
    kmean = kmean_ref[0, :, 0, :]

    def query_block_of(shape, axis):
        return (lax.broadcasted_iota(jnp.int32, shape, axis) & (qw - 1)) // blk

    past_q = lax.broadcasted_iota(jnp.int32, (nb, qw), 0) < first_blk + query_block_of((nb, qw), 1)
    for p in range(n_pairs):
        p0 = p * pair
        q_pair = jnp.concatenate([qT_ref[0, c, p0:p0 + pair, :] for c in range(QUERY_BLOCKS)], axis=1)
        zero = jnp.zeros_like(q_pair)
        for hh in range(2):
            hd = 2 * p + hh
            q_m = jnp.where(half if hh == 0 else jnp.logical_not(half), q_pair, zero)
            qcat_ref[p, :, hh * qw:(hh + 1) * qw] = q_m
            gate = _dot(kmean[:, p0:p0 + pair].astype(BF16), q_m)
            bias_ref[0:nb, hd * qw:(hd + 1) * qw] = jnp.where(past_q, gate, -jnp.inf)

    ones_rows = jnp.where(lax.broadcasted_iota(jnp.int32, (ACC_ROWS - HEAD_DIM, blk), 0) == 0,
                          1.0, 0.0).astype(BF16)
    key_pos = lax.broadcasted_iota(jnp.int32, (blk, qw), 0)
    qry_pos = lax.broadcasted_iota(jnp.int32, (blk, qw), 1)

    def score_parts(slot, key_blk, bias_row, diag):
        rows = pl.ds(pl.multiple_of(key_blk * blk, blk), blk)
        lo = 0 if diag is None else diag * blk
        if diag is not None:
            visible = (key_pos + diag * blk <= qry_pos)[:, lo:]

        def head(hd):
            p, hh = divmod(hd, 2)
            c0 = hd * qw + lo
            c1 = (hd + 1) * qw
            s_t = _dot(k_ref[0, rows, p * pair:(p + 1) * pair], qcat_ref[p, :, hh * qw + lo:(hh + 1) * qw])
            if diag is not None:
                s_t = jnp.where(visible, s_t, -jnp.inf)
            s_ref[slot, :, c0:c1] = s_t
            if bias_row is None:
                mx_ref[:, c0:c1] = _sublane_fold(s_t, jnp.maximum)
                return
            mx8 = jnp.maximum(mx_ref[:, c0:c1],
                              _sublane_fold(s_t, jnp.maximum) + bias_ref[pl.ds(bias_row, 1), c0:c1])
            mx_ref[:, c0:c1] = mx8
            m_ref[slot, :, c0:c1] = jnp.maximum(jnp.max(mx8, axis=0, keepdims=True), M_FLOOR)

        return [functools.partial(head, hd) for hd in range(n_heads)]

    def value_parts(slot, m_prev, key_blk, bias_row, lo=0):
        m_cur = m_ref[slot]
        alpha = jnp.exp2(m_prev - m_cur)
        m_eff = m_cur - bias_ref[pl.ds(bias_row, 1), :]

        def head(hd):
            r0 = hd * HEAD_DIM
            c0 = hd * qw + lo
            c1 = (hd + 1) * qw
            p_t = jnp.exp2(s_ref[slot, :, c0:c1] - m_eff[:, c0:c1])
            v_t = jnp.concatenate([vT_ref[0, key_blk, r0:r0 + HEAD_DIM, :], ones_rows], axis=0)
            a0 = hd * ACC_ROWS
            acc_ref[a0:a0 + ACC_ROWS, lo:] = (alpha[:, c0:c1] * acc_ref[a0:a0 + ACC_ROWS, lo:]
                                              + _dot(v_t, p_t.astype(BF16)))

        return [functools.partial(head, hd) for hd in range(n_heads)]

    def interleave(scores, values):
        for score, value in zip(scores, values):
            score()
            value()

    for score in score_parts(0, first_blk, None, 0):
        score()

    gate = bias_ref[0:nb, :]
    blk_i = lax.broadcasted_iota(jnp.int32, (nb, cols), 0)
    blk_f = blk_i.astype(F32)
    keep = jnp.zeros((nb, cols), F32)
    for _ in range(MOBA_TOPK):
        top = jnp.max(gate, axis=0, keepdims=True)
        first = jnp.min(jnp.where(gate == top, blk_f, float(nb)), axis=0, keepdims=True)
        hit = blk_f == first
        keep = jnp.where(hit, 1.0, keep)
        gate = jnp.where(hit, -jnp.inf, gate)
    keep = jnp.where(blk_i < first_blk + query_block_of((nb, cols), 1), keep, 0.0)
    bias_ref[0:nb, :] = jnp.where(keep > 0.0, 0.0, -jnp.inf)
    col_c = query_block_of((1, cols), 1)
    for c in range(QUERY_BLOCKS):
        bias_ref[nb + c:nb + c + 1, :] = jnp.where(col_c > c, bias_ref[pl.ds(first_blk + c, 1), :], 0.0)

    mx8 = mx_ref[...] + bias_ref[nb:nb + 1, :]
    mx_ref[...] = mx8
    m_first = jnp.maximum(jnp.max(mx8, axis=0, keepdims=True), M_FLOOR)
    m_ref[0] = m_first
    m_ref[1] = m_first

    acc_ref[...] = jnp.zeros_like(acc_ref)

    def step_blocks(t):
        return jnp.where(t == 0, first_blk, t - 1), jnp.where(t == 0, nb, t - 1)

    def fused_step(t, cur):
        m_prev = m_ref[cur]
        interleave(score_parts(cur, t - 1, t - 1, None), value_parts(1 - cur, m_prev, *step_blocks(t - 1)))

    def two_steps(u):
        for cur in (1, 0):
            fused_step(2 * u + 2 - cur, cur)

    def loop_body(v, carry):
        two_steps(2 * v)
        two_steps(2 * v + 1)
        return carry

    step_pairs = first_blk // 2
    lax.fori_loop(0, step_pairs // 2, loop_body, 0)

    @pl.when((step_pairs & 1) == 1)
    def _():
        two_steps(step_pairs - 1)

    for c in range(1, QUERY_BLOCKS):
        cur = c & 1
        m_prev = m_ref[cur]
        if c == 1:
            previous = value_parts(1 - cur, m_prev, *step_blocks(first_blk))
        else:
            previous = value_parts(1 - cur, m_prev, first_blk + c - 1, nb + c - 1, lo=(c - 1) * blk)
        interleave(score_parts(cur, first_blk + c, nb + c, c), previous)
    last = QUERY_BLOCKS - 1
    for value in value_parts(last & 1, m_ref[1 - (last & 1)], first_blk + last, nb + last, lo=last * blk):
        value()

    y_t = jnp.concatenate(
        [acc_ref[hd * ACC_ROWS:hd * ACC_ROWS + HEAD_DIM, :]
         / acc_ref[hd * ACC_ROWS + HEAD_DIM:hd * ACC_ROWS + HEAD_DIM + 1, :]
         for hd in range(n_heads)], axis=0)
    y_t = y_t * _rms(y_t, axis=0)
    o_ref[0] = (y_t.T * g_ref[...]).astype(BF16)


def _moba(qT, k, vT, kmean, mng_attn):
    batch, nb, d_attn, blk = qT.shape
    seq = k.shape[1]
    n_heads = d_attn // HEAD_DIM
    qw = QUERY_BLOCKS * blk
    cols = n_heads * qw
    assert QUERY_BLOCKS >= 2 and QUERY_BLOCKS % 2 == 0 and nb % QUERY_BLOCKS == 0

    return pl.pallas_call(
        _moba_kernel,
        grid=(batch, nb // QUERY_BLOCKS),
        in_specs=[
            pl.BlockSpec((1, QUERY_BLOCKS, d_attn, blk), lambda b, g: (b, g, 0, 0)),
            pl.BlockSpec((1, seq, d_attn), lambda b, g: (b, 0, 0)),
            pl.BlockSpec((1, nb, d_attn, blk), lambda b, g: (b, 0, 0, 0)),
            pl.BlockSpec((1, nb, 1, d_attn), lambda b, g: (b, 0, 0, 0)),
            pl.BlockSpec((1, d_attn), lambda b, g: (0, 0)),
        ],
        out_specs=pl.BlockSpec((1, qw, d_attn), lambda b, g: (b, g, 0)),
        out_shape=jax.ShapeDtypeStruct((batch, seq, d_attn), BF16),
        scratch_shapes=[
            pltpu.VMEM((n_heads // 2, 2 * HEAD_DIM, 2 * qw), BF16),
            pltpu.VMEM((nb + SUBLANES, cols), F32),
            pltpu.VMEM((2, blk, cols), F32),
            pltpu.VMEM((SUBLANES, cols), F32),
            pltpu.VMEM((2, 1, cols), F32),
            pltpu.VMEM((n_heads * ACC_ROWS, qw), F32),
        ],
        compiler_params=pltpu.CompilerParams(
            dimension_semantics=("arbitrary", "arbitrary"), vmem_limit_bytes=VMEM_LIMIT_BYTES),
        name="moba_attention",
    )(qT, k, vT, kmean, mng_attn)


def _out_mlp_kernel(x_ref, ya_ref, ycl_ref, mod_ref, ln2_ref, wout_ref, wup_ref, wdown_ref, o_ref):
    d_attn = ya_ref.shape[2]
    d_ff = wup_ref.shape[2]
    x = x_ref[0]
    m = mod_ref[0, 0]
    gate1, shift2, scale2, gate2 = m[2:3], m[3:4], m[4:5], m[5:6]
    mix = _dot(ya_ref[0], wout_ref[0, 0:d_attn, :]) + _dot(ycl_ref[0], wout_ref[0, d_attn:, :])
    x1 = x + gate1 * mix
    h2 = ((x1 * _rms(x1) * ln2_ref[...]) * (1.0 + scale2) + shift2).astype(BF16)
    ff = jnp.zeros_like(x1)
    for c0 in range(0, d_ff, FF_CHUNK):
        up = _dot(h2, wup_ref[0, :, c0:c0 + FF_CHUNK])
        act = jnp.square(jnp.maximum(up, 0.0)).astype(BF16)
        ff = ff + _dot(act, wdown_ref[0, c0:c0 + FF_CHUNK, :])
    o_ref[0] = x1 + gate2 * ff


def _out_mlp(layer, x, ya, ycl, mod, ln2_g, w_out_b, w_up_b, w_down_b):
    batch, seq, d_model = x.shape
    tm = MLP_TOKEN_TILE
    d_attn = ya.shape[2]
    d_cl = ycl.shape[2]

    def layer_weight(w):
        return pl.BlockSpec((1,) + w.shape[1:], lambda b, s: (layer, 0, 0), pipeline_mode=pl.Buffered(1))

    return pl.pallas_call(
        _out_mlp_kernel,
        grid=(batch, seq // tm),
        in_specs=[
            pl.BlockSpec((1, tm, d_model), lambda b, s: (b, s, 0)),
            pl.BlockSpec((1, tm, d_attn), lambda b, s: (b, s, 0)),
            pl.BlockSpec((1, tm, d_cl), lambda b, s: (b, s, 0)),
            pl.BlockSpec((1, 1, N_MOD, d_model), lambda b, s: (layer, b, 0, 0)),
            pl.BlockSpec((1, d_model), lambda b, s: (0, 0)),
            layer_weight(w_out_b), layer_weight(w_up_b), layer_weight(w_down_b),
        ],
        out_specs=pl.BlockSpec((1, tm, d_model), lambda b, s: (b, s, 0)),
        out_shape=jax.ShapeDtypeStruct((batch, seq, d_model), F32),
        compiler_params=pltpu.CompilerParams(
            dimension_semantics=("arbitrary", "arbitrary"), vmem_limit_bytes=VMEM_LIMIT_BYTES),
        name="out_mlp",
    )(x, ya, ycl, mod, ln2_g, w_out_b, w_up_b, w_down_b)


def _block_diag(w):
    n, r, c = w.shape
    eye = jnp.eye(n, dtype=w.dtype)
    return (eye[:, None, :, None] * w[:, :, None, :]).reshape(n * r, n * c)


def kernel(x, c, ln1_g, ln2_g, w_ada, b_ada, w_in, q_norm_g, k_norm_g, sc_w, lru_conv_w, lru_conv_b,
           lru_wa, lru_ba, lru_wx, lru_bx, lru_lambda, mix_norm_g, w_out, w_up, w_down):
    batch, seq, d_model = x.shape
    depth = w_in.shape[0]
    d_conv = sc_w.shape[2]
    d_lru = lru_conv_w.shape[2]
    d_attn = mix_norm_g.shape[1] - d_conv - d_lru
    n_heads = d_attn // HEAD_DIM
    assert seq % TOKEN_TILE == 0 and TOKEN_TILE % MOBA_BLOCK == 0 and seq % MLP_TOKEN_TILE == 0
    assert w_in.shape[2] == 3 * d_attn + 3 * d_conv + 2 * d_lru

    mod = _modulation(c, w_ada, b_ada).reshape(depth, batch, N_MOD, d_model)

    w_in_b, w_out_b, w_up_b, w_down_b = (w.astype(BF16) for w in (w_in, w_out, w_up, w_down))
    w_qkv_t = _transposed_qkv_weights(w_in, d_attn)
    gain_rows = lambda g: jnp.broadcast_to(jnp.tile(g, n_heads)[:, None], (d_attn, LANES))
    for l in range(depth):
        row = lambda v: v.reshape(1, -1)
        qT, k, vT, kmean, ycl = _mixer_in(
            l, x, mod, row(ln1_g[l]), w_in_b, w_qkv_t, gain_rows(q_norm_g[l]), gain_rows(k_norm_g[l]),
            sc_w[l], lru_conv_w[l], row(lru_conv_b[l]),
            _block_diag(lru_wa[l]).astype(BF16), row(lru_ba[l]),
            _block_diag(lru_wx[l]).astype(BF16), row(lru_bx[l]),
            row(lru_lambda[l]), row(mix_norm_g[l, d_attn:]),
            d_attn=d_attn, d_conv=d_conv, d_lru=d_lru)
        ya = _moba(qT, k, vT, kmean, row(mix_norm_g[l, :d_attn]))
        x = _out_mlp(l, x, ya, ycl, mod, row(ln2_g[l]), w_out_b, w_up_b, w_down_b)
    return x
```
